```python
import math
import jax, jax.numpy as jnp
from jax import lax
import numpy as np

D_MODEL = 1024
BATCH = 8
SEQ = 4096
DEPTH = 2

GRID_W = 64
CTX_LEN = 256
N_BRANCH = 4
BRANCH_W = 256
HEAD_DIM = 64
N_HEADS = BRANCH_W // HEAD_DIM
S5_GROUP = 16
S5_GROUPS = BRANCH_W // S5_GROUP
S5_STATE = 64
NA_ROWS = 8
NA_COLS = 16
HG_CHUNK = 16
RET_CHUNK = 128
N_EXPERTS = 16
EC_CAPACITY = 2
D_EXPERT = 2816
ROPE_BASE = 10000.0
EPS = 1e-6
IN_SPLITS = (BRANCH_W, 3 * BRANCH_W, 5 * BRANCH_W, 4 * BRANCH_W, N_BRANCH * D_MODEL)
D_IN = sum(IN_SPLITS)

kernel_name = "hybrid_s5_natten_hgrn2_retnet_ec_dit"


def rms_norm(x, w=None):
    xf = x.astype(jnp.float32)
    y = xf * lax.rsqrt(jnp.mean(jnp.square(xf), axis=-1, keepdims=True) + EPS)
    if w is not None:
        y = y * w.astype(jnp.float32)
    return y.astype(x.dtype)


def modulate(h, shift, scale):
    return h * (1 + scale) + shift


def to_heads(t):
    b, l, _ = t.shape
    return jnp.transpose(t.reshape(b, l, N_HEADS, -1), (0, 2, 1, 3))


def from_heads(t):
    b, h, l, d = t.shape
    return jnp.transpose(t, (0, 2, 1, 3)).reshape(b, l, h * d)


def axial_rope(n_tokens):
    t = np.arange(n_tokens)
    quarter = HEAD_DIM // 4
    inv = ROPE_BASE ** (-np.arange(quarter) / quarter)
    ang = np.concatenate([(t // GRID_W)[:, None] * inv, (t % GRID_W)[:, None] * inv], axis=1)
    return jnp.asarray(np.cos(ang), jnp.float32), jnp.asarray(np.sin(ang), jnp.float32)


def apply_rope(x, cos, sin):
    half = x.shape[-1] // 2
    x1, x2 = x[..., :half], x[..., half:]
    cos = cos[None, :, None, :]
    sin = sin[None, :, None, :]
    return jnp.concatenate([x1 * cos - x2 * sin, x2 * cos + x1 * sin], axis=-1)


def s5_scan(u, lam_re, lam_im, log_step, b_re, b_im, s0_re, s0_im):
    dt = jnp.exp(log_step)[:, None]
    mag = jnp.exp(lam_re * dt)
    ar = mag * jnp.cos(lam_im * dt)
    ai = mag * jnp.sin(lam_im * dt)
    den = lam_re * lam_re + lam_im * lam_im
    zr = ((ar - 1.0) * lam_re + ai * lam_im) / den
    zi = (ai * lam_re - (ar - 1.0) * lam_im) / den
    bb_re = zr[..., None] * b_re - zi[..., None] * b_im
    bb_im = zr[..., None] * b_im + zi[..., None] * b_re
    bu_re = jnp.einsum('blgp,gnp->blgn', u, bb_re)
    bu_im = jnp.einsum('blgp,gnp->blgn', u, bb_im)
    bu_re = bu_re.at[:, 0].add(ar * s0_re - ai * s0_im)
    bu_im = bu_im.at[:, 0].add(ar * s0_im + ai * s0_re)
    a_re = jnp.broadcast_to(ar, bu_re.shape)
    a_im = jnp.broadcast_to(ai, bu_im.shape)

    def combine(e1, e2):
        a1r, a1i, b1r, b1i = e1
        a2r, a2i, b2r, b2i = e2
        return (a1r * a2r - a1i * a2i, a1r * a2i + a1i * a2r,
                a2r * b1r - a2i * b1i + b2r, a2r * b1i + a2i * b1r + b2i)

    _, _, x_re, x_im = lax.associative_scan(combine, (a_re, a_im, bu_re, bu_im), axis=1)
    return x_re, x_im


def s5_readout(x_re, x_im, c_re, c_im):
    return jnp.einsum('blgn,gpn->blgp', x_re, c_re) - jnp.einsum('blgn,gpn->blgp', x_im, c_im)


def s5_mixer(u_ctx, u_lat, lam_re, lam_im, log_step, b_re, b_im, c_re, c_im, d_skip, w_glu, with_ctx_out):
    f32 = jnp.float32
    out_dtype = u_lat.dtype
    uc, ul = u_ctx.astype(f32), u_lat.astype(f32)
    gc = uc.reshape(uc.shape[0], uc.shape[1], S5_GROUPS, S5_GROUP)
    gl = ul.reshape(ul.shape[0], ul.shape[1], S5_GROUPS, S5_GROUP)
    zero = jnp.zeros((ul.shape[0], S5_GROUPS, S5_STATE), f32)
    y_ctx, y_lat = 0.0, 0.0
    for d in range(2):
        rev = (lambda t: t) if d == 0 else (lambda t: jnp.flip(t, axis=1))
        prm = (lam_re[d].astype(f32), lam_im[d].astype(f32), log_step[d].astype(f32),
               b_re[d].astype(f32), b_im[d].astype(f32))
        cr, ci = c_re[d].astype(f32), c_im[d].astype(f32)
        xc_re, xc_im = s5_scan(rev(gc), *prm, zero, zero)
        xl_re, xl_im = s5_scan(rev(gl), *prm, xc_re[:, -1], xc_im[:, -1])
        y_lat = y_lat + rev(s5_readout(xl_re, xl_im, cr, ci))
        if with_ctx_out:
            y_ctx = y_ctx + rev(s5_readout(xc_re, xc_im, cr, ci))

    def finish(y, u):
        y = y.reshape(u.shape) + d_skip.astype(f32) * u
        z = jax.nn.gelu(y)
        return (z * jax.nn.sigmoid(z @ w_glu.astype(f32))).astype(out_dtype)

    return (finish(y_ctx, uc) if with_ctx_out else None), finish(y_lat, ul)


def neighbourhood_attention(qkv_ctx, qkv_lat, q_norm_w, k_norm_w, rpb, with_ctx_out):
    b, l, _ = qkv_lat.shape
    lc = qkv_ctx.shape[1]
    rows = l // GRID_W
    wr = min(NA_ROWS, rows)
    scale = HEAD_DIM ** -0.5

    def split(t, n):
        q, k, v = jnp.split(t, 3, axis=-1)
        q = rms_norm(q.reshape(b, n, N_HEADS, HEAD_DIM), q_norm_w)
        k = rms_norm(k.reshape(b, n, N_HEADS, HEAD_DIM), k_norm_w)
        return q, k, v.reshape(b, n, N_HEADS, HEAD_DIM)

    qc, kc, vc = split(qkv_ctx, lc)
    ql, kl, vl = split(qkv_lat, l)
    qg = ql.reshape(b, rows, GRID_W, N_HEADS, HEAD_DIM)
    kg = kl.reshape(b, rows, GRID_W, N_HEADS, HEAD_DIM)
    vg = vl.reshape(b, rows, GRID_W, N_HEADS, HEAD_DIM)
    r = np.arange(rows)
    row_idx = np.clip(r - wr // 2, 0, rows - wr)[:, None] + np.arange(wr)[None, :]
    kband = kg[:, row_idx]
    vband = vg[:, row_idx]
    col = np.arange(GRID_W)
    col_start = np.clip(col - NA_COLS // 2, 0, GRID_W - NA_COLS)
    col_ok = (col[None, :] >= col_start[:, None]) & (col[None, :] < col_start[:, None] + NA_COLS)
    dr_idx = (row_idx - r[:, None] + NA_ROWS - 1).reshape(rows, 1, wr, 1)
    dc_idx = np.clip(col[None, :] - col[:, None] + NA_COLS - 1, 0, 2 * NA_COLS - 2).reshape(1, GRID_W, 1, GRID_W)
    bias = rpb[:, dr_idx, dc_idx]
    s_loc = jnp.einsum('brihd,brajhd->bhriaj', qg, kband).astype(jnp.float32) * scale + bias[None].astype(jnp.float32)
    s_loc = jnp.where(jnp.asarray(col_ok).reshape(1, 1, 1, GRID_W, 1, GRID_W), s_loc, -jnp.inf)
    s_ctx = jnp.einsum('brihd,bchd->bhric', qg, kc).astype(jnp.float32) * scale
    n_loc = wr * GRID_W
    s_all = jnp.concatenate([s_loc.reshape(b, N_HEADS, rows, GRID_W, n_loc), s_ctx], axis=-1)
    p = jax.nn.softmax(s_all, axis=-1).astype(vl.dtype)
    p_loc = p[..., :n_loc].reshape(b, N_HEADS, rows, GRID_W, wr, GRID_W)
    p_ctx = p[..., n_loc:]
    o_lat = jnp.einsum('bhriaj,brajhd->brihd', p_loc, vband) + jnp.einsum('bhric,bchd->brihd', p_ctx, vc)
    o_lat = o_lat.reshape(b, l, BRANCH_W)
    o_ctx = None
    if with_ctx_out:
        sc = jnp.einsum('bqhd,bkhd->bhqk', qc, kc).astype(jnp.float32) * scale
        pc = jax.nn.softmax(sc, axis=-1).astype(vc.dtype)
        o_ctx = jnp.einsum('bhqk,bkhd->bqhd', pc, vc).reshape(b, lc, BRANCH_W)
    return o_ctx, o_lat


def bidir_prefix(run, ctx_dirs, lat_dirs, with_ctx_out):
    o_ctx, o_lat = None, None
    for d in range(2):
        rev = (lambda t: t) if d == 0 else (lambda t: jnp.flip(t, axis=2))
        oc, sc = run(d, [rev(t) for t in ctx_dirs[d]], None)
        ol, _ = run(d, [rev(t) for t in lat_dirs[d]], sc)
        o_lat = rev(ol) if o_lat is None else o_lat + rev(ol)
        if with_ctx_out:
            o_ctx = rev(oc) if o_ctx is None else o_ctx + rev(oc)
    return o_ctx, o_lat


def blocks(t, chunk):
    b, h, l, d = t.shape
    return jnp.moveaxis(t.reshape(b, h, l // chunk, chunk, d), 2, 0)


def unblocks(t):
    n, b, h, c, d = t.shape
    return jnp.moveaxis(t, 0, 2).reshape(b, h, n * c, d)


def gated_chunk_scan(q, k, v, log_f, s0, chunk):
    b, h, _, dk = q.shape
    dv = v.shape[-1]
    if s0 is None:
        s0 = jnp.zeros((b, h, dk, dv), jnp.float32)
    qb, kb, vb, gb = blocks(q, chunk), blocks(k, chunk), blocks(v, chunk), blocks(log_f, chunk)
    cum = jnp.cumsum(gb, axis=3)
    last = cum[:, :, :, -1:, :]
    q_dec = qb * jnp.exp(cum)
    k_dec = kb * jnp.exp(-cum)
    k_end = kb * jnp.exp(last - cum)
    causal = jnp.tril(jnp.ones((chunk, chunk), bool))
    att = jnp.where(causal, jnp.einsum('nbhid,nbhjd->nbhij', q_dec, k_dec), 0.0)
    intra = jnp.einsum('nbhij,nbhje->nbhie', att, vb)
    kv = jnp.einsum('nbhjd,nbhje->nbhde', k_end, vb)
    decay_end = jnp.exp(last[:, :, :, 0, :])

    def step(s, xs):
        kv_n, d_n = xs
        return d_n[..., None] * s + kv_n, s

    s_fin, s_prev = lax.scan(step, s0, (kv, decay_end))
    cross = jnp.einsum('nbhid,nbhde->nbhie', q_dec, s_prev)
    return unblocks(intra + cross), s_fin


def hgrn2_mixer(p_ctx, p_lat, lower_bound, norm_w, with_ctx_out):
    f32 = jnp.float32
    out_dtype = p_lat.dtype
    lb = lower_bound.astype(f32)

    def prep(p):
        q, f_fw, f_bw, i, g = jnp.split(p.astype(f32), 5, axis=-1)
        qh = to_heads(jax.nn.silu(q))
        vh = to_heads(i)
        dirs = []
        for f_logit in (f_fw, f_bw):
            fg = lb + (1.0 - lb) * jax.nn.sigmoid(f_logit)
            dirs.append((qh, to_heads(1.0 - fg), vh, to_heads(jnp.log(fg))))
        return dirs, g

    ctx_dirs, g_ctx = prep(p_ctx)
    lat_dirs, g_lat = prep(p_lat)
    run = lambda d, a, s0: gated_chunk_scan(a[0], a[1], a[2], a[3], s0, HG_CHUNK)
    o_ctx, o_lat = bidir_prefix(run, ctx_dirs, lat_dirs, with_ctx_out)

    def finish(o, g):
        return (from_heads(rms_norm(o, norm_w)) * jax.nn.silu(g)).astype(out_dtype)

    return (finish(o_ctx, g_ctx) if with_ctx_out else None), finish(o_lat, g_lat)


def retention_chunk_scan(q, k, v, log_gamma, s0, chunk):
    b, h, _, dk = q.shape
    dv = v.shape[-1]
    if s0 is None:
        s0 = jnp.zeros((b, h, dk, dv), jnp.float32)
    qb, kb, vb = blocks(q, chunk), blocks(k, chunk), blocks(v, chunk)
    idx = jnp.arange(chunk, dtype=jnp.float32)
    diff = idx[:, None] - idx[None, :]
    lg = log_gamma[:, None, None]
    decay = jnp.where(diff >= 0, jnp.exp(jnp.maximum(diff, 0.0) * lg), 0.0)
    xi = jnp.exp((idx + 1.0)[None, :] * log_gamma[:, None])[..., None]
    zeta = jnp.exp((chunk - 1.0 - idx)[None, :] * log_gamma[:, None])[..., None]
    gamma_c = jnp.exp(chunk * log_gamma)[:, None, None]
    intra = jnp.einsum('nbhij,nbhje->nbhie', jnp.einsum('nbhid,nbhjd->nbhij', qb, kb) * decay, vb)
    kv = jnp.einsum('nbhjd,nbhje->nbhde', kb * zeta, vb)

    def step(s, kv_n):
        return gamma_c * s + kv_n, s

    s_fin, s_prev = lax.scan(step, s0, kv)
    cross = jnp.einsum('nbhid,nbhde->nbhie', qb, s_prev) * xi
    return unblocks(intra + cross), s_fin


def retention_mixer(p_ctx, p_lat, decay_logit, with_ctx_out):
    f32 = jnp.float32
    out_dtype = p_lat.dtype
    cos, sin = axial_rope(p_lat.shape[1])

    def prep(p, rope):
        b, n, _ = p.shape
        q, k, v, g = jnp.split(p.astype(f32), 4, axis=-1)
        q = q.reshape(b, n, N_HEADS, HEAD_DIM)
        k = k.reshape(b, n, N_HEADS, HEAD_DIM)
        if rope:
            q, k = apply_rope(q, cos, sin), apply_rope(k, cos, sin)
        qh = jnp.transpose(q, (0, 2, 1, 3))
        kh = jnp.transpose(k, (0, 2, 1, 3)) * (HEAD_DIM ** -0.5)
        return (qh, kh, to_heads(v)), g

    qkv_c, g_ctx = prep(p_ctx, False)
    qkv_l, g_lat = prep(p_lat, True)
    log_gamma = jax.nn.log_sigmoid(decay_logit.astype(f32))
    run = lambda d, a, s0: retention_chunk_scan(a[0], a[1], a[2], log_gamma[d], s0, RET_CHUNK)
    o_ctx, o_lat = bidir_prefix(run, [qkv_c, qkv_c], [qkv_l, qkv_l], with_ctx_out)

    def finish(o, g):
        return (from_heads(rms_norm(o)) * jax.nn.silu(g)).astype(out_dtype)

    return (finish(o_ctx, g_ctx) if with_ctx_out else None), finish(o_lat, g_lat)


def merge_branches(outs, gate_logits, w_branch, w_out):
    gates = jnp.split(gate_logits, N_BRANCH, axis=-1)
    m = 0.0
    for br in range(N_BRANCH):
        m = m + jax.nn.sigmoid(gates[br]) * (outs[br] @ w_branch[br])
    return m @ w_out


def token_mixers(h_ctx, h_lat, w_in, s5_lam_re, s5_lam_im, s5_log_step, s5_b_re, s5_b_im, s5_c_re, s5_c_im,
                 s5_d, s5_glu_w, na_q_norm, na_k_norm, na_rpb, hg_lb, hg_norm_w, ret_decay_logit,
                 w_branch, w_out, with_ctx_out):
    cuts = [int(v) for v in np.cumsum(IN_SPLITS)[:-1]]
    pl = jnp.split(h_lat @ w_in, cuts, axis=-1)
    pc = jnp.split(h_ctx @ w_in, cuts, axis=-1)
    s5_c, s5_l = s5_mixer(pc[0], pl[0], s5_lam_re, s5_lam_im, s5_log_step, s5_b_re, s5_b_im,
                          s5_c_re, s5_c_im, s5_d, s5_glu_w, with_ctx_out)
    na_c, na_l = neighbourhood_attention(pc[1], pl[1], na_q_norm, na_k_norm, na_rpb, with_ctx_out)
    hg_c, hg_l = hgrn2_mixer(pc[2], pl[2], hg_lb, hg_norm_w, with_ctx_out)
    rt_c, rt_l = retention_mixer(pc[3], pl[3], ret_decay_logit, with_ctx_out)
    out_l = merge_branches((s5_l, na_l, hg_l, rt_l), pl[4], w_branch, w_out)
    out_c = merge_branches((s5_c, na_c, hg_c, rt_c), pc[4], w_branch, w_out) if with_ctx_out else None
    return out_c, out_l


def expert_choice_ffn(h, router_w, w_gate, w_up, w_down):
    b, n, _ = h.shape
    cap = EC_CAPACITY * n // N_EXPERTS
    aff = jax.nn.softmax(jnp.einsum('bnd,de->bne', h.astype(jnp.float32), router_w.astype(jnp.float32)), axis=-1)
    g, idx = lax.top_k(jnp.swapaxes(aff, 1, 2), cap)
    idx_e = jnp.moveaxis(idx, 1, 0)
    g_e = jnp.moveaxis(g, 1, 0)
    b_idx = jnp.arange(b)[:, None]

    def one_expert(args):
        wg, wu, wd, ie, ge = args
        xe = h[b_idx, ie]
        ye = (jax.nn.silu(xe @ wg) * (xe @ wu)) @ wd
        return ye * ge[..., None].astype(ye.dtype)

    y = lax.map(one_expert, (w_gate, w_up, w_down, idx_e, g_e))
    return jnp.zeros_like(h).at[jnp.arange(b)[None, :, None], idx_e].add(y.astype(h.dtype))


def setup_inputs(seed: int = 0) -> dict:
    key = jax.random.key(seed)
    ks = iter(jax.random.split(key, 40))
    f32 = jnp.float32

    def nrm(shape, s):
        return s * jax.random.normal(next(ks), shape, f32)

    D, H, G, N, P, E, F = D_MODEL, N_HEADS, S5_GROUPS, S5_STATE, S5_GROUP, N_EXPERTS, D_EXPERT
    lam_im_base = jnp.pi * jnp.arange(N, dtype=f32)
    ret_base = jnp.log(jnp.exp2(5.0 + jnp.arange(H, dtype=f32)) - 1.0)
    return {
        'x': nrm((BATCH, SEQ, D), 1.0),
        'c': nrm((BATCH, D), 1.0),
        'ctx': nrm((BATCH, CTX_LEN, D), 1.0),
        'c_ctx': nrm((D,), 1.0),
        'ada_w': nrm((DEPTH, D, 6 * D), 0.02),
        'ada_b': nrm((DEPTH, 6 * D), 0.01),
        'norm_mix_w': 1.0 + nrm((DEPTH, D), 0.02),
        'norm_ffn_w': 1.0 + nrm((DEPTH, D), 0.02),
        'w_in': nrm((DEPTH, D, D_IN), D ** -0.5),
        's5_lam_re': -0.5 + nrm((DEPTH, 2, G, N), 0.01),
        's5_lam_im': lam_im_base + nrm((DEPTH, 2, G, N), 0.01),
        's5_log_step': jax.random.uniform(next(ks), (DEPTH, 2, G), f32, math.log(1e-3), math.log(1e-1)),
        's5_b_re': nrm((DEPTH, 2, G, N, P), (2.0 * P) ** -0.5),
        's5_b_im': nrm((DEPTH, 2, G, N, P), (2.0 * P) ** -0.5),
        's5_c_re': nrm((DEPTH, 2, G, P, N), N ** -0.5),
        's5_c_im': nrm((DEPTH, 2, G, P, N), N ** -0.5),
        's5_d': nrm((DEPTH, BRANCH_W), 1.0),
        's5_glu_w': nrm((DEPTH, BRANCH_W, BRANCH_W), BRANCH_W ** -0.5),
        'na_q_norm': 1.0 + nrm((DEPTH, HEAD_DIM), 0.02),
        'na_k_norm': 1.0 + nrm((DEPTH, HEAD_DIM), 0.02),
        'na_rpb': nrm((DEPTH, H, 2 * NA_ROWS - 1, 2 * NA_COLS - 1), 0.02),
        'hg_lower_bounds': nrm((DEPTH, BRANCH_W), 0.1),
        'hg_norm_w': 1.0 + nrm((DEPTH, HEAD_DIM), 0.02),
        'ret_decay_logit': ret_base + nrm((DEPTH, 2, H), 0.01),
        'w_branch': nrm((DEPTH, N_BRANCH, BRANCH_W, D), BRANCH_W ** -0.5),
        'w_out': nrm((DEPTH, D, D), D ** -0.5),
        'router_w': nrm((DEPTH, D, E), D ** -0.5),
        'ex_w_gate': nrm((DEPTH, E, D, F), D ** -0.5),
        'ex_w_up': nrm((DEPTH, E, D, F), D ** -0.5),
        'ex_w_down': nrm((DEPTH, E, F, D), F ** -0.5),
    }


def reference(x, c, ctx, c_ctx, ada_w, ada_b, norm_mix_w, norm_ffn_w, w_in, s5_lam_re, s5_lam_im,
              s5_log_step, s5_b_re, s5_b_im, s5_c_re, s5_c_im, s5_d, s5_glu_w, na_q_norm, na_k_norm,
              na_rpb, hg_lower_bounds, hg_norm_w, ret_decay_logit, w_branch, w_out, router_w,
              ex_w_gate, ex_w_up, ex_w_down):
    lb_p = jax.nn.softmax(hg_lower_bounds.astype(jnp.float32), axis=0)
    lower_bounds = jnp.cumsum(lb_p, axis=0) - lb_p[0]
    cond_lat = jax.nn.silu(c)
    cond_ctx = jax.nn.silu(c_ctx)[None]
    xc = ctx
    for li in range(DEPTH):
        last = li == DEPTH - 1
        mod_l = [m[:, None] for m in jnp.split(cond_lat @ ada_w[li] + ada_b[li], 6, axis=-1)]
        mod_c = [m[:, None] for m in jnp.split(cond_ctx @ ada_w[li] + ada_b[li], 6, axis=-1)]
        sh1_l, sc1_l, g1_l, sh2_l, sc2_l, g2_l = mod_l
        sh1_c, sc1_c, g1_c, sh2_c, sc2_c, g2_c = mod_c
        h_l = modulate(rms_norm(x, norm_mix_w[li]), sh1_l, sc1_l)
        h_c = modulate(rms_norm(xc, norm_mix_w[li]), sh1_c, sc1_c)
        mix_c, mix_l = token_mixers(h_c, h_l, w_in[li], s5_lam_re[li], s5_lam_im[li], s5_log_step[li],
                                    s5_b_re[li], s5_b_im[li], s5_c_re[li], s5_c_im[li], s5_d[li], s5_glu_w[li],
                                    na_q_norm[li], na_k_norm[li], na_rpb[li], lower_bounds[li], hg_norm_w[li],
                                    ret_decay_logit[li], w_branch[li], w_out[li], not last)
        x = x + g1_l * mix_l
        h_l = modulate(rms_norm(x, norm_ffn_w[li]), sh2_l, sc2_l)
        x = x + g2_l * expert_choice_ffn(h_l, router_w[li], ex_w_gate[li], ex_w_up[li], ex_w_down[li])
        if not last:
            xc = xc + g1_c * mix_c
            h_c = modulate(rms_norm(xc, norm_ffn_w[li]), sh2_c, sc2_c)
            xc = xc + g2_c * expert_choice_ffn(h_c, router_w[li], ex_w_gate[li], ex_w_up[li], ex_w_down[li])
    return x
```

```python
import functools
import math

import jax
import jax.numpy as jnp
import numpy as np
from jax import lax
from jax.experimental import pallas as pl
from jax.experimental.pallas import tpu as pltpu

D_MODEL = 1024
DEPTH = 2
GRID_W = 64
N_BRANCH = 4
BRANCH_W = 256
HEAD_DIM = 64
N_HEADS = BRANCH_W // HEAD_DIM
S5_GROUP = 16
S5_GROUPS = BRANCH_W // S5_GROUP
S5_STATE = 64
NA_ROWS = 8
NA_COLS = 16
HG_CHUNK = 16
RET_CHUNK = 128
N_EXPERTS = 16
EC_CAPACITY = 2
D_EXPERT = 2816
ROPE_BASE = 10000.0
EPS = 1e-6
IN_SPLITS = (BRANCH_W, 3 * BRANCH_W, 5 * BRANCH_W, 4 * BRANCH_W, N_BRANCH * D_MODEL)
D_IN = sum(IN_SPLITS)

F32 = jnp.float32
BF16 = jnp.bfloat16

V7X_VMEM_BYTES = 64 * 1024 * 1024
VMEM_LIMIT = V7X_VMEM_BYTES - 8 * 1024 * 1024


def _params(*sem):
    return pltpu.CompilerParams(dimension_semantics=sem, vmem_limit_bytes=VMEM_LIMIT)


def _norm_mod(x, norm_w, shift, scale):
    y = x * lax.rsqrt(jnp.mean(x * x, axis=-1, keepdims=True) + EPS) * norm_w
    return y * (1.0 + scale) + shift


def _in_proj_kernel(x_ref, nw_ref, sh_ref, sc_ref, w_ref, *out_refs, col_chunk):
    hb = _norm_mod(x_ref[0], nw_ref[...], sh_ref[0], sc_ref[0]).astype(BF16)
    off = 0
    for o_ref, width in zip(out_refs, IN_SPLITS):
        for c0 in range(0, width, col_chunk):
            cw = min(col_chunk, width - c0)
            o_ref[0, :, c0:c0 + cw] = jnp.dot(
                hb, w_ref[:, off + c0:off + c0 + cw], preferred_element_type=F32).astype(o_ref.dtype)
        off += width


def in_proj(x, norm_w, shift, scale, w_in_bf16, tm):
    b, n, d = x.shape
    tok = lambda bi, i: (bi, i, 0)
    per_b = lambda bi, i: (bi, 0, 0)
    const2 = lambda bi, i: (0, 0)
    return pl.pallas_call(
        functools.partial(_in_proj_kernel, col_chunk=512),
        grid=(b, n // tm),
        in_specs=[
            pl.BlockSpec((1, tm, d), tok),
            pl.BlockSpec((1, d), const2),
            pl.BlockSpec((1, 1, d), per_b),
            pl.BlockSpec((1, 1, d), per_b),
            pl.BlockSpec((d, D_IN), const2, pipeline_mode=pl.Buffered(1)),
        ],
        out_specs=[pl.BlockSpec((1, tm, w), tok) for w in IN_SPLITS],
        out_shape=[jax.ShapeDtypeStruct((b, n, w), BF16) for w in IN_SPLITS],
        compiler_params=_params("parallel", "parallel"),
        name="in_proj",
    )(x, norm_w, shift, scale, w_in_bf16)


def _merge_kernel(o0_ref, o1_ref, o2_ref, o3_ref, gate_ref, wb_ref, wo_ref, x_ref, g1_ref,
                  nw_ref, sh_ref, sc_ref, rw_ref, xo_ref, h_ref, aff_ref):
    m = None
    for br, o_ref in enumerate((o0_ref, o1_ref, o2_ref, o3_ref)):
        proj = jnp.dot(o_ref[0], wb_ref[br], preferred_element_type=F32)
        gate = jax.nn.sigmoid(gate_ref[0, :, br * D_MODEL:(br + 1) * D_MODEL].astype(F32))
        m = gate * proj if m is None else m + gate * proj
    mix = jnp.dot(m.astype(BF16), wo_ref[...], preferred_element_type=F32)
    x_new = x_ref[0] + g1_ref[0] * mix
    xo_ref[0] = x_new
    h = _norm_mod(x_new, nw_ref[...], sh_ref[0], sc_ref[0])
    h_ref[0] = h.astype(BF16)
    logits = lax.dot_general(rw_ref[...], h, (((1,), (1,)), ((), ())),
                             precision=lax.Precision.HIGHEST, preferred_element_type=F32)
    logits = logits - jnp.max(logits, axis=0, keepdims=True)
    e = jnp.exp(logits)
    aff_ref[0] = e / jnp.sum(e, axis=0, keepdims=True)


def merge_and_route(outs, gates, w_branch_bf16, w_out_bf16, x, g1, norm_w, shift, scale, router_w_t, tm):
    b, n, d = x.shape
    tok = lambda bi, i: (bi, i, 0)
    per_b = lambda bi, i: (bi, 0, 0)
    const2 = lambda bi, i: (0, 0)
    const3 = lambda bi, i: (0, 0, 0)
    return pl.pallas_call(
        _merge_kernel,
        grid=(b, n // tm),
        in_specs=[pl.BlockSpec((1, tm, BRANCH_W), tok)] * N_BRANCH + [
            pl.BlockSpec((1, tm, N_BRANCH * d), tok),
            pl.BlockSpec((N_BRANCH, BRANCH_W, d), const3),
            pl.BlockSpec((d, d), const2),
            pl.BlockSpec((1, tm, d), tok),
            pl.BlockSpec((1, 1, d), per_b),
            pl.BlockSpec((1, d), const2),
            pl.BlockSpec((1, 1, d), per_b),
            pl.BlockSpec((1, 1, d), per_b),
            pl.BlockSpec((N_EXPERTS, d), const2),
        ],
        out_specs=[
            pl.BlockSpec((1, tm, d), tok),
            pl.BlockSpec((1, tm, d), tok),
            pl.BlockSpec((1, N_EXPERTS, tm), lambda bi, i: (bi, 0, i)),
        ],
        out_shape=[
            jax.ShapeDtypeStruct((b, n, d), F32),
            jax.ShapeDtypeStruct((b, n, d), BF16),
            jax.ShapeDtypeStruct((b, N_EXPERTS, n), F32),
        ],
        compiler_params=_params("parallel", "parallel"),
        name="merge_route",
    )(*outs, gates, w_branch_bf16, w_out_bf16, x, g1, norm_w, shift, scale, router_w_t)


def _expert_kernel(x_ref, g_ref, wg_ref, wu_ref, wd_ref, y_ref, acc_ref):
    f = pl.program_id(2)
    xb = x_ref[0]
    gate = jnp.dot(xb, wg_ref[0].astype(BF16), preferred_element_type=F32)
    up = jnp.dot(xb, wu_ref[0].astype(BF16), preferred_element_type=F32)
    act = (jax.nn.silu(gate) * up).astype(BF16)
    part = jnp.dot(act, wd_ref[0].astype(BF16), preferred_element_type=F32)

    @pl.when(f == 0)
    def _():
        acc_ref[...] = part

    @pl.when(f > 0)
    def _():
        acc_ref[...] += part

    @pl.when(f == pl.num_programs(2) - 1)
    def _():
        y_ref[0] = acc_ref[...] * g_ref[0]


def expert_ffn(xe, ge, w_gate, w_up, w_down, tm, tf):
    e, t, d = xe.shape
    f = w_gate.shape[-1]
    return pl.pallas_call(
        _expert_kernel,
        grid=(e, t // tm, f // tf),
        in_specs=[
            pl.BlockSpec((1, tm, d), lambda ei, i, fi: (ei, i, 0)),
            pl.BlockSpec((1, tm, 1), lambda ei, i, fi: (ei, i, 0)),
            pl.BlockSpec((1, d, tf), lambda ei, i, fi: (ei, 0, fi)),
            pl.BlockSpec((1, d, tf), lambda ei, i, fi: (ei, 0, fi)),
            pl.BlockSpec((1, tf, d), lambda ei, i, fi: (ei, fi, 0)),
        ],
        out_specs=pl.BlockSpec((1, tm, d), lambda ei, i, fi: (ei, i, 0)),
        out_shape=jax.ShapeDtypeStruct((e, t, d), F32),
        scratch_shapes=[pltpu.VMEM((tm, d), F32)],
        compiler_params=_params("parallel", "parallel", "arbitrary"),
        name="expert_ffn",
    )(xe, ge, w_gate, w_up, w_down)


def expert_choice_ffn(h_bf16, aff_t, w_gate, w_up, w_down):
    b, n, d = h_bf16.shape
    cap = EC_CAPACITY * n // N_EXPERTS
    g, idx = lax.top_k(aff_t, cap)
    idx_e = jnp.moveaxis(idx, 1, 0)
    g_e = jnp.moveaxis(g, 1, 0)
    xe = h_bf16[jnp.arange(b)[None, :, None], idx_e]
    t = b * cap
    tm = min(t, 2048)
    y = expert_ffn(xe.reshape(N_EXPERTS, t, d), g_e.reshape(N_EXPERTS, t, 1), w_gate, w_up, w_down, tm, 256)
    y = y.reshape(N_EXPERTS, b, cap, d)
    return jnp.zeros((b, n, d), F32).at[jnp.arange(b)[None, :, None], idx_e].add(y)


def _rms(x, w=None):
    y = x * lax.rsqrt(jnp.mean(jnp.square(x), axis=-1, keepdims=True) + EPS)
    return y if w is None else y * w


def _to_heads(t):
    b, l, _ = t.shape
    return jnp.transpose(t.reshape(b, l, N_HEADS, -1), (0, 2, 1, 3))


def _from_heads(t):
    b, h, l, d = t.shape
    return jnp.transpose(t, (0, 2, 1, 3)).reshape(b, l, h * d)


def _s5_scan(u, lam_re, lam_im, log_step, b_re, b_im, s0_re, s0_im):
    dt = jnp.exp(log_step)[:, None]
    mag = jnp.exp(lam_re * dt)
    ar = mag * jnp.cos(lam_im * dt)
    ai = mag * jnp.sin(lam_im * dt)
    den = lam_re * lam_re + lam_im * lam_im
    zr = ((ar - 1.0) * lam_re + ai * lam_im) / den
    zi = (ai * lam_re - (ar - 1.0) * lam_im) / den
    bb_re = zr[..., None] * b_re - zi[..., None] * b_im
    bb_im = zr[..., None] * b_im + zi[..., None] * b_re
    bu_re = jnp.einsum('blgp,gnp->blgn', u, bb_re)
    bu_im = jnp.einsum('blgp,gnp->blgn', u, bb_im)
    bu_re = bu_re.at[:, 0].add(ar * s0_re - ai * s0_im)
    bu_im = bu_im.at[:, 0].add(ar * s0_im + ai * s0_re)
    a_re = jnp.broadcast_to(ar, bu_re.shape)
    a_im = jnp.broadcast_to(ai, bu_im.shape)

    def combine(e1, e2):
        a1r, a1i, b1r, b1i = e1
        a2r, a2i, b2r, b2i = e2
        return (a1r * a2r - a1i * a2i, a1r * a2i + a1i * a2r,
                a2r * b1r - a2i * b1i + b2r, a2r * b1i + a2i * b1r + b2i)

    _, _, x_re, x_im = lax.associative_scan(combine, (a_re, a_im, bu_re, bu_im), axis=1)
    return x_re, x_im


def _s5_readout(x_re, x_im, c_re, c_im):
    return jnp.einsum('blgn,gpn->blgp', x_re, c_re) - jnp.einsum('blgn,gpn->blgp', x_im, c_im)


def s5_mixer(u_ctx, u_lat, lam_re, lam_im, log_step, b_re, b_im, c_re, c_im, d_skip, w_glu, with_ctx_out):
    uc, ul = u_ctx.astype(F32), u_lat.astype(F32)
    gc = uc.reshape(uc.shape[0], uc.shape[1], S5_GROUPS, S5_GROUP)
    gl = ul.reshape(ul.shape[0], ul.shape[1], S5_GROUPS, S5_GROUP)
    zero = jnp.zeros((ul.shape[0], S5_GROUPS, S5_STATE), F32)
    y_ctx, y_lat = 0.0, 0.0
    for d in range(2):
        rev = (lambda t: t) if d == 0 else (lambda t: jnp.flip(t, axis=1))
        prm = (lam_re[d], lam_im[d], log_step[d], b_re[d], b_im[d])
        xc_re, xc_im = _s5_scan(rev(gc), *prm, zero, zero)
        xl_re, xl_im = _s5_scan(rev(gl), *prm, xc_re[:, -1], xc_im[:, -1])
        y_lat = y_lat + rev(_s5_readout(xl_re, xl_im, c_re[d], c_im[d]))
        if with_ctx_out:
            y_ctx = y_ctx + rev(_s5_readout(xc_re, xc_im, c_re[d], c_im[d]))

    def finish(y, u):
        y = y.reshape(u.shape) + d_skip * u
        z = jax.nn.gelu(y)
        return (z * jax.nn.sigmoid(z @ w_glu)).astype(BF16)

    return (finish(y_ctx, uc) if with_ctx_out else None), finish(y_lat, ul)


def neighbourhood_attention(qkv_ctx, qkv_lat, q_norm_w, k_norm_w, rpb, with_ctx_out):
    qkv_ctx, qkv_lat = qkv_ctx.astype(F32), qkv_lat.astype(F32)
    b, l, _ = qkv_lat.shape
    lc = qkv_ctx.shape[1]
    rows = l // GRID_W
    wr = min(NA_ROWS, rows)
    scale = HEAD_DIM ** -0.5

    def split(t, n):
        q, k, v = jnp.split(t, 3, axis=-1)
        q = _rms(q.reshape(b, n, N_HEADS, HEAD_DIM), q_norm_w)
        k = _rms(k.reshape(b, n, N_HEADS, HEAD_DIM), k_norm_w)
        return q, k, v.reshape(b, n, N_HEADS, HEAD_DIM)

    qc, kc, vc = split(qkv_ctx, lc)
    ql, kl, vl = split(qkv_lat, l)
    qg = ql.reshape(b, rows, GRID_W, N_HEADS, HEAD_DIM)
    kg = kl.reshape(b, rows, GRID_W, N_HEADS, HEAD_DIM)
    vg = vl.reshape(b, rows, GRID_W, N_HEADS, HEAD_DIM)
    r = np.arange(rows)
    row_idx = np.clip(r - wr // 2, 0, rows - wr)[:, None] + np.arange(wr)[None, :]
    kband = kg[:, row_idx]
    vband = vg[:, row_idx]
    col = np.arange(GRID_W)
    col_start = np.clip(col - NA_COLS // 2, 0, GRID_W - NA_COLS)
    col_ok = (col[None, :] >= col_start[:, None]) & (col[None, :] < col_start[:, None] + NA_COLS)
    dr_idx = (row_idx - r[:, None] + NA_ROWS - 1).reshape(rows, 1, wr, 1)
    dc_idx = np.clip(col[None, :] - col[:, None] + NA_COLS - 1, 0, 2 * NA_COLS - 2).reshape(1, GRID_W, 1, GRID_W)
    bias = rpb[:, dr_idx, dc_idx]
    s_loc = jnp.einsum('brihd,brajhd->bhriaj', qg, kband) * scale + bias[None]
    s_loc = jnp.where(jnp.asarray(col_ok).reshape(1, 1, 1, GRID_W, 1, GRID_W), s_loc, -jnp.inf)
    s_ctx = jnp.einsum('brihd,bchd->bhric', qg, kc) * scale
    n_loc = wr * GRID_W
    s_all = jnp.concatenate([s_loc.reshape(b, N_HEADS, rows, GRID_W, n_loc), s_ctx], axis=-1)
    p = jax.nn.softmax(s_all, axis=-1)
    p_loc = p[..., :n_loc].reshape(b, N_HEADS, rows, GRID_W, wr, GRID_W)
    p_ctx = p[..., n_loc:]
    o_lat = jnp.einsum('bhriaj,brajhd->brihd', p_loc, vband) + jnp.einsum('bhric,bchd->brihd', p_ctx, vc)
    o_lat = o_lat.reshape(b, l, BRANCH_W).astype(BF16)
    o_ctx = None
    if with_ctx_out:
        sc = jnp.einsum('bqhd,bkhd->bhqk', qc, kc) * scale
        pc = jax.nn.softmax(sc, axis=-1)
        o_ctx = jnp.einsum('bhqk,bkhd->bqhd', pc, vc).reshape(b, lc, BRANCH_W).astype(BF16)
    return o_ctx, o_lat


def _bidir_prefix(run, ctx_dirs, lat_dirs, with_ctx_out):
    o_ctx, o_lat = None, None
    for d in range(2):
        rev = (lambda t: t) if d == 0 else (lambda t: jnp.flip(t, axis=2))
        oc, sc = run(d, [rev(t) for t in ctx_dirs[d]], None)
        ol, _ = run(d, [rev(t) for t in lat_dirs[d]], sc)
        o_lat = rev(ol) if o_lat is None else o_lat + rev(ol)
        if with_ctx_out:
            o_ctx = rev(oc) if o_ctx is None else o_ctx + rev(oc)
    return o_ctx, o_lat


def _blocks(t, chunk):
    b, h, l, d = t.shape
    return jnp.moveaxis(t.reshape(b, h, l // chunk, chunk, d), 2, 0)


def _unblocks(t):
    n, b, h, c, d = t.shape
    return jnp.moveaxis(t, 0, 2).reshape(b, h, n * c, d)


def _gated_chunk_scan(q, k, v, log_f, s0, chunk):
    b, h, _, dk = q.shape
    dv = v.shape[-1]
    if s0 is None:
        s0 = jnp.zeros((b, h, dk, dv), F32)
    qb, kb, vb, gb = _blocks(q, chunk), _blocks(k, chunk), _blocks(v, chunk), _blocks(log_f, chunk)
    cum = jnp.cumsum(gb, axis=3)
    last = cum[:, :, :, -1:, :]
    q_dec = qb * jnp.exp(cum)
    k_dec = kb * jnp.exp(-cum)
    k_end = kb * jnp.exp(last - cum)
    causal = jnp.tril(jnp.ones((chunk, chunk), bool))
    att = jnp.where(causal, jnp.einsum('nbhid,nbhjd->nbhij', q_dec, k_dec), 0.0)
    intra = jnp.einsum('nbhij,nbhje->nbhie', att, vb)
    kv = jnp.einsum('nbhjd,nbhje->nbhde', k_end, vb)
    decay_end = jnp.exp(last[:, :, :, 0, :])

    def step(s, xs):
        kv_n, d_n = xs
        return d_n[..., None] * s + kv_n, s

    s_fin, s_prev = lax.scan(step, s0, (kv, decay_end))
    cross = jnp.einsum('nbhid,nbhde->nbhie', q_dec, s_prev)
    return _unblocks(intra + cross), s_fin


def hgrn2_mixer(p_ctx, p_lat, lb, norm_w, with_ctx_out):
    def prep(p):
        q, f_fw, f_bw, i, g = jnp.split(p.astype(F32), 5, axis=-1)
        qh = _to_heads(jax.nn.silu(q))
        vh = _to_heads(i)
        dirs = []
        for f_logit in (f_fw, f_bw):
            fg = lb + (1.0 - lb) * jax.nn.sigmoid(f_logit)
            dirs.append((qh, _to_heads(1.0 - fg), vh, _to_heads(jnp.log(fg))))
        return dirs, g

    ctx_dirs, g_ctx = prep(p_ctx)
    lat_dirs, g_lat = prep(p_lat)
    run = lambda d, a, s0: _gated_chunk_scan(a[0], a[1], a[2], a[3], s0, HG_CHUNK)
    o_ctx, o_lat = _bidir_prefix(run, ctx_dirs, lat_dirs, with_ctx_out)

    def finish(o, g):
        return (_from_heads(_rms(o, norm_w)) * jax.nn.silu(g)).astype(BF16)

    return (finish(o_ctx, g_ctx) if with_ctx_out else None), finish(o_lat, g_lat)


def _retention_chunk_scan(q, k, v, log_gamma, s0, chunk):
    b, h, _, dk = q.shape
    dv = v.shape[-1]
    if s0 is None:
        s0 = jnp.zeros((b, h, dk, dv), F32)
    qb, kb, vb = _blocks(q, chunk), _blocks(k, chunk), _blocks(v, chunk)
    idx = jnp.arange(chunk, dtype=F32)
    diff = idx[:, None] - idx[None, :]
    lg = log_gamma[:, None, None]
    decay = jnp.where(diff >= 0, jnp.exp(jnp.maximum(diff, 0.0) * lg), 0.0)
    xi = jnp.exp((idx + 1.0)[None, :] * log_gamma[:, None])[..., None]
    zeta = jnp.exp((chunk - 1.0 - idx)[None, :] * log_gamma[:, None])[..., None]
    gamma_c = jnp.exp(chunk * log_gamma)[:, None, None]
    intra = jnp.einsum('nbhij,nbhje->nbhie', jnp.einsum('nbhid,nbhjd->nbhij', qb, kb) * decay, vb)
    kv = jnp.einsum('nbhjd,nbhje->nbhde', kb * zeta, vb)

    def step(s, kv_n):
        return gamma_c * s + kv_n, s

    s_fin, s_prev = lax.scan(step, s0, kv)
    cross = jnp.einsum('nbhid,nbhde->nbhie', qb, s_prev) * xi
    return _unblocks(intra + cross), s_fin


def _axial_rope(n_tokens):
    t = np.arange(n_tokens)
    quarter = HEAD_DIM // 4
    inv = ROPE_BASE ** (-np.arange(quarter) / quarter)
    ang = np.concatenate([(t // GRID_W)[:, None] * inv, (t % GRID_W)[:, None] * inv], axis=1)
    return jnp.asarray(np.cos(ang), F32), jnp.asarray(np.sin(ang), F32)


def _apply_rope(x, cos, sin):
    half = x.shape[-1] // 2
    x1, x2 = x[..., :half], x[..., half:]
    cos = cos[None, :, None, :]
    sin = sin[None, :, None, :]
    return jnp.concatenate([x1 * cos - x2 * sin, x2 * cos + x1 * sin], axis=-1)


def retention_mixer(p_ctx, p_lat, decay_logit, with_ctx_out):
    cos, sin = _axial_rope(p_lat.shape[1])

    def prep(p, rope):
        b, n, _ = p.shape
        q, k, v, g = jnp.split(p.astype(F32), 4, axis=-1)
        q = q.reshape(b, n, N_HEADS, HEAD_DIM)
        k = k.reshape(b, n, N_HEADS, HEAD_DIM)
        if rope:
            q, k = _apply_rope(q, cos, sin), _apply_rope(k, cos, sin)
        qh = jnp.transpose(q, (0, 2, 1, 3))
        kh = jnp.transpose(k, (0, 2, 1, 3)) * (HEAD_DIM ** -0.5)
        return (qh, kh, _to_heads(v)), g

    qkv_c, g_ctx = prep(p_ctx, False)
    qkv_l, g_lat = prep(p_lat, True)
    log_gamma = jax.nn.log_sigmoid(decay_logit)
    run = lambda d, a, s0: _retention_chunk_scan(a[0], a[1], a[2], log_gamma[d], s0, RET_CHUNK)
    o_ctx, o_lat = _bidir_prefix(run, [qkv_c, qkv_c], [qkv_l, qkv_l], with_ctx_out)

    def finish(o, g):
        return (_from_heads(_rms(o)) * jax.nn.silu(g)).astype(BF16)

    return (finish(o_ctx, g_ctx) if with_ctx_out else None), finish(o_lat, g_lat)


def kernel(x, c, ctx, c_ctx, ada_w, ada_b, norm_mix_w, norm_ffn_w, w_in, s5_lam_re, s5_lam_im, s5_log_step,
           s5_b_re, s5_b_im, s5_c_re, s5_c_im, s5_d, s5_glu_w, na_q_norm, na_k_norm, na_rpb, hg_lower_bounds,
           hg_norm_w, ret_decay_logit, w_branch, w_out, router_w, ex_w_gate, ex_w_up, ex_w_down):
    b = x.shape[0]
    lb_p = jax.nn.softmax(hg_lower_bounds, axis=0)
    lower_bounds = jnp.cumsum(lb_p, axis=0) - lb_p[0]
    cond_lat = jax.nn.silu(c)
    cond_ctx = jnp.broadcast_to(jax.nn.silu(c_ctx)[None], c.shape)
    hi = lax.Precision.HIGHEST
    xc = ctx
    for li in range(DEPTH):
        last = li == DEPTH - 1
        mod_l = [m[:, None] for m in jnp.split(jnp.dot(cond_lat, ada_w[li], precision=hi) + ada_b[li], 6, axis=-1)]
        mod_c = [m[:, None] for m in jnp.split(jnp.dot(cond_ctx, ada_w[li], precision=hi) + ada_b[li], 6, axis=-1)]
        sh1_l, sc1_l, g1_l, sh2_l, sc2_l, g2_l = mod_l
        sh1_c, sc1_c, g1_c, sh2_c, sc2_c, g2_c = mod_c
        w_in_b = w_in[li].astype(BF16)
        wb_b = w_branch[li].astype(BF16)
        wo_b = w_out[li].astype(BF16)
        rw_t = router_w[li].T
        nmw = norm_mix_w[li][None]
        nfw = norm_ffn_w[li][None]

        pl_ = in_proj(x, nmw, sh1_l, sc1_l, w_in_b, 512)
        pc_ = in_proj(xc, nmw, sh1_c, sc1_c, w_in_b, 256)
        wc = not last
        s5_c, s5_l = s5_mixer(pc_[0], pl_[0], s5_lam_re[li], s5_lam_im[li], s5_log_step[li], s5_b_re[li],
                              s5_b_im[li], s5_c_re[li], s5_c_im[li], s5_d[li], s5_glu_w[li], wc)
        na_c, na_l = neighbourhood_attention(pc_[1], pl_[1], na_q_norm[li], na_k_norm[li], na_rpb[li], wc)
        hg_c, hg_l = hgrn2_mixer(pc_[2], pl_[2], lower_bounds[li], hg_norm_w[li], wc)
        rt_c, rt_l = retention_mixer(pc_[3], pl_[3], ret_decay_logit[li], wc)

        x, h_l, aff_l = merge_and_route((s5_l, na_l, hg_l, rt_l), pl_[4], wb_b, wo_b, x, g1_l,
                                        nfw, sh2_l, sc2_l, rw_t, 512)
        x = x + g2_l * expert_choice_ffn(h_l, aff_l, ex_w_gate[li], ex_w_up[li], ex_w_down[li])
        if not last:
            xc, h_c, aff_c = merge_and_route((s5_c, na_c, hg_c, rt_c), pc_[4], wb_b, wo_b, xc, g1_c,
                                             nfw, sh2_c, sc2_c, rw_t, 256)
            xc = xc + g2_c * expert_choice_ffn(h_c, aff_c, ex_w_gate[li], ex_w_up[li], ex_w_down[li])
    return x
```

```python
import functools
import math

import jax
import jax.numpy as jnp
import numpy as np
from jax import lax
from jax.experimental import pallas as pl
from jax.experimental.pallas import tpu as pltpu

D_MODEL = 1024
DEPTH = 2
GRID_W = 64
N_BRANCH = 4
BRANCH_W = 256
HEAD_DIM = 64
N_HEADS = BRANCH_W // HEAD_DIM
S5_GROUP = 16
S5_GROUPS = BRANCH_W // S5_GROUP
S5_STATE = 64
NA_ROWS = 8
NA_COLS = 16
HG_CHUNK = 16
RET_CHUNK = 128
N_EXPERTS = 16
EC_CAPACITY = 2
D_EXPERT = 2816
ROPE_BASE = 10000.0
EPS = 1e-6
IN_SPLITS = (BRANCH_W, 3 * BRANCH_W, 5 * BRANCH_W, 4 * BRANCH_W, N_BRANCH * D_MODEL)
D_IN = sum(IN_SPLITS)

F32 = jnp.float32
BF16 = jnp.bfloat16

V7X_VMEM_BYTES = 64 * 1024 * 1024
VMEM_LIMIT = V7X_VMEM_BYTES - 8 * 1024 * 1024


def _params(*sem):
    return pltpu.CompilerParams(dimension_semantics=sem, vmem_limit_bytes=VMEM_LIMIT)


def _norm_mod(x, norm_w, shift, scale):
    y = x * lax.rsqrt(jnp.mean(x * x, axis=-1, keepdims=True) + EPS) * norm_w
    return y * (1.0 + scale) + shift


def _head_mean_sq(t, head_avg):
    sq = t * t
    hi = sq.astype(BF16)
    lo = (sq - hi.astype(F32)).astype(BF16)
    return (jnp.dot(hi, head_avg, preferred_element_type=F32)
            + jnp.dot(lo, head_avg, preferred_element_type=F32))


def _in_proj_kernel(x_ref, nw_ref, sh_ref, sc_ref, w_ref, wrot_ref, qkw_ref, havg_ref, cos_ref, sin_ref,
                    s5_ref, na_ref, hg_ref, rt_ref, gate_ref, *, rope):
    hb = _norm_mod(x_ref[0], nw_ref[...], sh_ref[0], sc_ref[0]).astype(BF16)
    cw = BRANCH_W
    proj = lambda col: jnp.dot(hb, w_ref[:, col * cw:(col + 1) * cw], preferred_element_type=F32)
    col = 0
    s5_ref[0] = proj(col).astype(BF16)
    col += 1
    for part in range(3):
        t = proj(col + part)
        if part < 2:
            t = t * lax.rsqrt(_head_mean_sq(t, havg_ref[...]) + EPS) * qkw_ref[part:part + 1, :]
        if part == 0:
            t = t * (HEAD_DIM ** -0.5)
        na_ref[0, :, part * cw:(part + 1) * cw] = t.astype(BF16)
    col += 3
    for part in range(5):
        hg_ref[0, :, part * cw:(part + 1) * cw] = proj(col + part).astype(BF16)
    col += 5
    for part in range(4):
        t = proj(col + part)
        if part < 2 and rope:
            swapped = jnp.dot(hb, wrot_ref[:, part * cw:(part + 1) * cw], preferred_element_type=F32)
            t = t * cos_ref[...] + swapped * sin_ref[...]
        if part == 1:
            t = t * (HEAD_DIM ** -0.5)
        rt_ref[0, :, part * cw:(part + 1) * cw] = t.astype(BF16)
    col += 4
    for part in range(N_BRANCH * D_MODEL // cw):
        gate_ref[0, :, part * cw:(part + 1) * cw] = proj(col + part).astype(BF16)


def rope_tables(n_tokens):
    quarter = HEAD_DIM // 4
    t = jnp.arange(n_tokens, dtype=jnp.int32)
    inv = jnp.asarray(ROPE_BASE ** (-np.arange(quarter) / quarter), F32)
    ang = jnp.concatenate([(t // GRID_W).astype(F32)[:, None] * inv, (t % GRID_W).astype(F32)[:, None] * inv], axis=1)
    cos, sin = jnp.cos(ang), jnp.sin(ang)
    cos_h = jnp.concatenate([cos, cos], axis=1)
    sin_h = jnp.concatenate([-sin, sin], axis=1)
    return jnp.tile(cos_h, (1, N_HEADS)), jnp.tile(sin_h, (1, N_HEADS))


def _swap_head_halves_cols(w):
    d, c = w.shape
    w = w.reshape(d, c // HEAD_DIM, 2, HEAD_DIM // 2)
    return w[:, :, ::-1, :].reshape(d, c)


def head_avg_matrix():
    h = np.arange(BRANCH_W) // HEAD_DIM
    return jnp.asarray((h[:, None] == h[None, :]) / HEAD_DIM, BF16)


def in_proj(x, norm_w, shift, scale, w_in_bf16, w_rot_bf16, qk_norm_w, rope, tm):
    b, n, d = x.shape
    tok = lambda bi, i: (bi, i, 0)
    per_b = lambda bi, i: (bi, 0, 0)
    const2 = lambda bi, i: (0, 0)
    cos, sin = rope_tables(n)
    return pl.pallas_call(
        functools.partial(_in_proj_kernel, rope=rope),
        grid=(b, n // tm),
        in_specs=[
            pl.BlockSpec((1, tm, d), tok),
            pl.BlockSpec((1, d), const2),
            pl.BlockSpec((1, 1, d), per_b),
            pl.BlockSpec((1, 1, d), per_b),
            pl.BlockSpec((d, D_IN), const2, pipeline_mode=pl.Buffered(1)),
            pl.BlockSpec((d, 2 * BRANCH_W), const2, pipeline_mode=pl.Buffered(1)),
            pl.BlockSpec((2, BRANCH_W), const2),
            pl.BlockSpec((BRANCH_W, BRANCH_W), const2),
            pl.BlockSpec((tm, BRANCH_W), lambda bi, i: (i, 0)),
            pl.BlockSpec((tm, BRANCH_W), lambda bi, i: (i, 0)),
        ],
        out_specs=[pl.BlockSpec((1, tm, w), tok) for w in IN_SPLITS],
        out_shape=[jax.ShapeDtypeStruct((b, n, w), BF16) for w in IN_SPLITS],
        compiler_params=_params("parallel", "parallel"),
        name="in_proj",
    )(x, norm_w, shift, scale, w_in_bf16, w_rot_bf16, qk_norm_w, head_avg_matrix(), cos, sin)


NA_QROWS = 8
NA_KROWS = 16
NA_TQ = NA_QROWS * GRID_W
NA_TK = NA_KROWS * GRID_W
NA_MASKED = -1e30


def na_bias(rpb, rows):
    nblk = rows // NA_QROWS
    pats = []
    qi = np.arange(NA_TQ)
    ki = np.arange(NA_TK)
    for g in (0, 1, nblk - 1):
        ks = int(np.clip(NA_QROWS * g - NA_ROWS // 2, 0, rows - NA_KROWS))
        qr = (NA_QROWS * g + qi // GRID_W)[:, None]
        qc = (qi % GRID_W)[:, None]
        kr = (ks + ki // GRID_W)[None, :]
        kc = (ki % GRID_W)[None, :]
        band = np.clip(qr - NA_ROWS // 2, 0, rows - NA_ROWS)
        cs = np.clip(qc - NA_COLS // 2, 0, GRID_W - NA_COLS)
        ok = (kr >= band) & (kr < band + NA_ROWS) & (kc >= cs) & (kc < cs + NA_COLS)
        dr = np.clip(kr - qr + NA_ROWS - 1, 0, 2 * NA_ROWS - 2)
        dc = np.clip(kc - qc + NA_COLS - 1, 0, 2 * NA_COLS - 2)
        pats.append(jnp.where(jnp.asarray(ok)[None], rpb[:, dr, dc], NA_MASKED))
    return jnp.stack(pats).astype(F32)


def _head_lane_mask(h, shape):
    lane = lax.broadcasted_iota(jnp.int32, shape, len(shape) - 1)
    return (lane >= h * HEAD_DIM) & (lane < (h + 1) * HEAD_DIM)


def _na_kernel(q_ref, k_ref, v_ref, kc_ref, vc_ref, bias_ref, o_ref, *, rows):
    g = pl.program_id(1)
    ks = jnp.clip(NA_QROWS * g - NA_ROWS // 2, 0, rows - NA_KROWS)
    start = pl.multiple_of(ks * GRID_W, GRID_W * (NA_ROWS // 2))
    q = q_ref[0]
    k_win = k_ref[0, pl.ds(start, NA_TK), :]
    v_win = v_ref[0, pl.ds(start, NA_TK), :]
    kc, vc = kc_ref[0], vc_ref[0]
    nt = (((1,), (1,)), ((), ()))
    out = jnp.zeros(q.shape, F32)
    for h in range(N_HEADS):
        hm = _head_lane_mask(h, q.shape)
        qh = jnp.where(hm, q, jnp.zeros_like(q))
        s_loc = lax.dot_general(qh, k_win, nt, preferred_element_type=F32) + bias_ref[0, h]
        s_ctx = lax.dot_general(qh, kc, nt, preferred_element_type=F32)
        m = jnp.maximum(jnp.max(s_loc, axis=-1, keepdims=True), jnp.max(s_ctx, axis=-1, keepdims=True))
        p_loc = jnp.exp(s_loc - m)
        p_ctx = jnp.exp(s_ctx - m)
        denom = jnp.sum(p_loc, axis=-1, keepdims=True) + jnp.sum(p_ctx, axis=-1, keepdims=True)
        oh = (jnp.dot(p_loc.astype(BF16), v_win, preferred_element_type=F32)
              + jnp.dot(p_ctx.astype(BF16), vc, preferred_element_type=F32)) / denom
        out = jnp.where(hm, oh, out)
    o_ref[0] = out.astype(o_ref.dtype)


def na_latent(na_lat, na_ctx, bias):
    b, l, _ = na_lat.shape
    lc = na_ctx.shape[1]
    rows = l // GRID_W
    nblk = rows // NA_QROWS
    pat = lambda bi, g: (jnp.where(g == 0, 0, jnp.where(g == nblk - 1, 2, 1)), 0, 0, 0)
    return pl.pallas_call(
        functools.partial(_na_kernel, rows=rows),
        grid=(b, nblk),
        in_specs=[
            pl.BlockSpec((1, NA_TQ, BRANCH_W), lambda bi, g: (bi, g, 0)),
            pl.BlockSpec((1, l, BRANCH_W), lambda bi, g: (bi, 0, 1)),
            pl.BlockSpec((1, l, BRANCH_W), lambda bi, g: (bi, 0, 2)),
            pl.BlockSpec((1, lc, BRANCH_W), lambda bi, g: (bi, 0, 1)),
            pl.BlockSpec((1, lc, BRANCH_W), lambda bi, g: (bi, 0, 2)),
            pl.BlockSpec((1, N_HEADS, NA_TQ, NA_TK), pat),
        ],
        out_specs=pl.BlockSpec((1, NA_TQ, BRANCH_W), lambda bi, g: (bi, g, 0)),
        out_shape=jax.ShapeDtypeStruct((b, l, BRANCH_W), BF16),
        compiler_params=_params("parallel", "arbitrary"),
        name="na_latent",
    )(na_lat, na_lat, na_lat, na_ctx, na_ctx, bias)


def _na_ctx_kernel(q_ref, k_ref, v_ref, o_ref):
    q, k, v = q_ref[0], k_ref[0], v_ref[0]
    nt = (((1,), (1,)), ((), ()))
    out = jnp.zeros(q.shape, F32)
    for h in range(N_HEADS):
        hm = _head_lane_mask(h, q.shape)
        s = lax.dot_general(jnp.where(hm, q, jnp.zeros_like(q)), k, nt, preferred_element_type=F32)
        p = jnp.exp(s - jnp.max(s, axis=-1, keepdims=True))
        oh = jnp.dot(p.astype(BF16), v, preferred_element_type=F32) / jnp.sum(p, axis=-1, keepdims=True)
        out = jnp.where(hm, oh, out)
    o_ref[0] = out.astype(o_ref.dtype)


def na_context(na_ctx):
    b, lc, _ = na_ctx.shape
    spec = lambda col: pl.BlockSpec((1, lc, BRANCH_W), lambda bi: (bi, 0, col))
    return pl.pallas_call(
        _na_ctx_kernel,
        grid=(b,),
        in_specs=[spec(0), spec(1), spec(2)],
        out_specs=pl.BlockSpec((1, lc, BRANCH_W), lambda bi: (bi, 0, 0)),
        out_shape=jax.ShapeDtypeStruct((b, lc, BRANCH_W), BF16),
        compiler_params=_params("parallel"),
        name="na_context",
    )(na_ctx, na_ctx, na_ctx)


def _merge_kernel(y5_ref, u5_ref, o1_ref, o2_ref, o3_ref, gate_ref, d5_ref, wglu_ref, wb_ref, wo_ref, x_ref, g1_ref,
                  nw_ref, sh_ref, sc_ref, rw_ref, xo_ref, h_ref, aff_ref):
    z = jax.nn.gelu(y5_ref[0] + d5_ref[...] * u5_ref[0].astype(F32))
    o_s5 = (z * jax.nn.sigmoid(jnp.dot(z.astype(BF16), wglu_ref[...], preferred_element_type=F32))).astype(BF16)
    m = None
    for br in range(N_BRANCH):
        o_br = o_s5 if br == 0 else (o1_ref, o2_ref, o3_ref)[br - 1][0]
        proj = jnp.dot(o_br, wb_ref[br], preferred_element_type=F32)
        gate = jax.nn.sigmoid(gate_ref[0, :, br * D_MODEL:(br + 1) * D_MODEL].astype(F32))
        m = gate * proj if m is None else m + gate * proj
    mix = jnp.dot(m.astype(BF16), wo_ref[...], preferred_element_type=F32)
    x_new = x_ref[0] + g1_ref[0] * mix
    xo_ref[0] = x_new
    h = _norm_mod(x_new, nw_ref[...], sh_ref[0], sc_ref[0])
    h_ref[0] = h.astype(BF16)
    logits = lax.dot_general(rw_ref[...], h, (((1,), (1,)), ((), ())),
                             precision=lax.Precision.HIGHEST, preferred_element_type=F32)
    logits = logits - jnp.max(logits, axis=0, keepdims=True)
    e = jnp.exp(logits)
    aff_ref[0] = e / jnp.sum(e, axis=0, keepdims=True)


def merge_and_route(y_s5, u_s5, outs, gates, s5_d, s5_glu_bf16, w_branch_bf16, w_out_bf16, x, g1,
                    norm_w, shift, scale, router_w_t, tm):
    b, n, d = x.shape
    tok = lambda bi, i: (bi, i, 0)
    per_b = lambda bi, i: (bi, 0, 0)
    const2 = lambda bi, i: (0, 0)
    const3 = lambda bi, i: (0, 0, 0)
    return pl.pallas_call(
        _merge_kernel,
        grid=(b, n // tm),
        in_specs=[pl.BlockSpec((1, tm, BRANCH_W), tok)] * (N_BRANCH + 1) + [
            pl.BlockSpec((1, tm, N_BRANCH * d), tok),
            pl.BlockSpec((1, BRANCH_W), const2),
            pl.BlockSpec((BRANCH_W, BRANCH_W), const2),
            pl.BlockSpec((N_BRANCH, BRANCH_W, d), const3),
            pl.BlockSpec((d, d), const2),
            pl.BlockSpec((1, tm, d), tok),
            pl.BlockSpec((1, 1, d), per_b),
            pl.BlockSpec((1, d), const2),
            pl.BlockSpec((1, 1, d), per_b),
            pl.BlockSpec((1, 1, d), per_b),
            pl.BlockSpec((N_EXPERTS, d), const2),
        ],
        out_specs=[
            pl.BlockSpec((1, tm, d), tok),
            pl.BlockSpec((1, tm, d), tok),
            pl.BlockSpec((1, N_EXPERTS, tm), lambda bi, i: (bi, 0, i)),
        ],
        out_shape=[
            jax.ShapeDtypeStruct((b, n, d), F32),
            jax.ShapeDtypeStruct((b, n, d), BF16),
            jax.ShapeDtypeStruct((b, N_EXPERTS, n), F32),
        ],
        compiler_params=_params("parallel", "parallel"),
        name="merge_route",
    )(y_s5, u_s5, *outs, gates, s5_d, s5_glu_bf16, w_branch_bf16, w_out_bf16, x, g1, norm_w, shift, scale,
      router_w_t)


def _expert_kernel(x_ref, g_ref, wg_ref, wu_ref, wd_ref, y_ref, acc_ref):
    f = pl.program_id(2)
    xb = x_ref[0]
    gate = jnp.dot(xb, wg_ref[0].astype(BF16), preferred_element_type=F32)
    up = jnp.dot(xb, wu_ref[0].astype(BF16), preferred_element_type=F32)
    act = (jax.nn.silu(gate) * up).astype(BF16)
    part = jnp.dot(act, wd_ref[0].astype(BF16), preferred_element_type=F32)

    @pl.when(f == 0)
    def _():
        acc_ref[...] = part

    @pl.when(f > 0)
    def _():
        acc_ref[...] += part

    @pl.when(f == pl.num_programs(2) - 1)
    def _():
        y_ref[0] = acc_ref[...] * g_ref[0]


def expert_ffn(xe, ge, w_gate, w_up, w_down, tm, tf):
    e, t, d = xe.shape
    f = w_gate.shape[-1]
    return pl.pallas_call(
        _expert_kernel,
        grid=(e, t // tm, f // tf),
        in_specs=[
            pl.BlockSpec((1, tm, d), lambda ei, i, fi: (ei, i, 0)),
            pl.BlockSpec((1, tm, 1), lambda ei, i, fi: (ei, i, 0)),
            pl.BlockSpec((1, d, tf), lambda ei, i, fi: (ei, 0, fi)),
            pl.BlockSpec((1, d, tf), lambda ei, i, fi: (ei, 0, fi)),
            pl.BlockSpec((1, tf, d), lambda ei, i, fi: (ei, fi, 0)),
        ],
        out_specs=pl.BlockSpec((1, tm, d), lambda ei, i, fi: (ei, i, 0)),
        out_shape=jax.ShapeDtypeStruct((e, t, d), F32),
        scratch_shapes=[pltpu.VMEM((tm, d), F32)],
        compiler_params=_params("parallel", "parallel", "arbitrary"),
        name="expert_ffn",
    )(xe, ge, w_gate, w_up, w_down)


def expert_choice_ffn(h_bf16, aff_t, w_gate, w_up, w_down):
    b, n, d = h_bf16.shape
    cap = EC_CAPACITY * n // N_EXPERTS
    g, idx = lax.top_k(aff_t, cap)
    idx_e = jnp.moveaxis(idx, 1, 0)
    g_e = jnp.moveaxis(g, 1, 0)
    xe = h_bf16[jnp.arange(b)[None, :, None], idx_e]
    t = b * cap
    tm = min(t, 2048)
    y = expert_ffn(xe.reshape(N_EXPERTS, t, d), g_e.reshape(N_EXPERTS, t, 1), w_gate, w_up, w_down, tm, 256)
    y = y.reshape(N_EXPERTS, b, cap, d)
    return jnp.zeros((b, n, d), F32).at[jnp.arange(b)[None, :, None], idx_e].add(y)


S5_BLK = 16
S5_ROW = S5_GROUPS * S5_BLK * S5_GROUP
S5_PAIRS = S5_GROUPS // 2
S5_LANES = S5_GROUPS * S5_STATE


def _pair_blockdiag(t):
    g, r, c = t.shape
    t = t.reshape(g // 2, 2, r, c)
    z = jnp.zeros_like(t[:, 0])
    top = jnp.concatenate([t[:, 0], z], axis=-1)
    bot = jnp.concatenate([z, t[:, 1]], axis=-1)
    return jnp.concatenate([top, bot], axis=-2)


def s5_operators(lam_re, lam_im, log_step, b_re, b_im, c_re, c_im):
    hp = lax.Precision.HIGHEST
    blk = S5_BLK
    dt = jnp.exp(log_step)[..., None]
    k = jnp.arange(blk + 1, dtype=F32)
    mag = jnp.exp((lam_re * dt)[..., None] * k)
    ang = (lam_im * dt)[..., None] * k
    pr, pi = mag * jnp.cos(ang), mag * jnp.sin(ang)
    ar, ai = pr[..., 1], pi[..., 1]
    den = lam_re * lam_re + lam_im * lam_im
    zr = ((ar - 1.0) * lam_re + ai * lam_im) / den
    zi = (ai * lam_re - (ar - 1.0) * lam_im) / den
    bb_re = zr[..., None] * b_re - zi[..., None] * b_im
    bb_im = zr[..., None] * b_im + zi[..., None] * b_re
    ca_re = c_re[..., None] * pr[:, :, None] - c_im[..., None] * pi[:, :, None]
    ca_im = c_re[..., None] * pi[:, :, None] + c_im[..., None] * pr[:, :, None]
    kern = (jnp.einsum('dgpnl,dgnq->dglpq', ca_re, bb_re, precision=hp)
            - jnp.einsum('dgpnl,dgnq->dglpq', ca_im, bb_im, precision=hp))
    j = np.arange(blk)[:, None]
    i = np.arange(blk)[None, :]
    ms, ws, rres, rims = [], [], [], []
    for d in range(2):
        lag = (i - j) if d == 0 else (j - i)
        valid = jnp.asarray(lag >= 0, F32)[None, :, None, :, None]
        kd = kern[d][:, np.clip(lag, 0, blk - 1)]
        m = jnp.transpose(kd, (0, 1, 4, 2, 3)) * valid
        ms.append(m.reshape(S5_GROUPS, blk * S5_GROUP, blk * S5_GROUP))
        pw = (blk - 1 - np.arange(blk)) if d == 0 else np.arange(blk)
        apr, api = pr[d][..., pw], pi[d][..., pw]
        w_re = apr[..., None] * bb_re[d][:, :, None] - api[..., None] * bb_im[d][:, :, None]
        w_im = apr[..., None] * bb_im[d][:, :, None] + api[..., None] * bb_re[d][:, :, None]
        to_w = lambda t: jnp.transpose(t, (0, 2, 3, 1)).reshape(S5_GROUPS, blk * S5_GROUP, S5_STATE)
        ws.append((_pair_blockdiag(to_w(w_re)), _pair_blockdiag(to_w(w_im))))
        ex = (np.arange(blk) + 1) if d == 0 else (blk - np.arange(blk))
        r_re = ca_re[d][..., ex]
        r_im = -ca_im[d][..., ex]
        to_r = lambda t: jnp.transpose(t, (0, 2, 3, 1)).reshape(S5_GROUPS, S5_STATE, blk * S5_GROUP)
        rres.append(_pair_blockdiag(to_r(r_re)))
        rims.append(_pair_blockdiag(to_r(r_im)))
    m_op = jnp.stack(ms).astype(BF16)
    w_re = jnp.stack([w[0] for w in ws]).astype(BF16)
    w_im = jnp.stack([w[1] for w in ws]).astype(BF16)
    r_re = jnp.stack(rres).astype(BF16)
    r_im = jnp.stack(rims).astype(BF16)
    a_blk = jnp.stack([pr[..., blk].reshape(2, 1, S5_LANES), pi[..., blk].reshape(2, 1, S5_LANES)], axis=1)
    return m_op, w_re, w_im, r_re, r_im, a_blk


def _s5_kernel(uc_ref, ul_ref, m_ref, wre_ref, wim_ref, rre_ref, rim_ref, a_ref, yc_ref, yl_ref,
               vre, vim, sre, sim):
    rc, rl = uc_ref.shape[1], ul_ref.shape[1]
    segs = ((uc_ref, yc_ref, 0, rc), (ul_ref, yl_ref, rc, rl))
    gw = S5_BLK * S5_GROUP
    for d in range(2):
        for u_ref, _, base, rows in segs:
            for h in range(S5_PAIRS):
                u_pair = u_ref[0, :, 2 * h * gw:2 * (h + 1) * gw]
                vre[base:base + rows, h * 128:(h + 1) * 128] = jnp.dot(
                    u_pair, wre_ref[d, h], preferred_element_type=F32)
                vim[base:base + rows, h * 128:(h + 1) * 128] = jnp.dot(
                    u_pair, wim_ref[d, h], preferred_element_type=F32)
        ar, ai = a_ref[d, 0], a_ref[d, 1]

        def run(base, rows, carry):
            def step(t, c):
                xr, xi = c
                idx = base + (t if d == 0 else rows - 1 - t)
                sre[pl.ds(idx, 1), :] = xr
                sim[pl.ds(idx, 1), :] = xi
                nr = ar * xr - ai * xi + vre[pl.ds(idx, 1), :]
                ni = ar * xi + ai * xr + vim[pl.ds(idx, 1), :]
                return nr, ni
            return lax.fori_loop(0, rows, step, carry)

        zero = jnp.zeros((1, S5_LANES), F32)
        carry = run(0, rc, (zero, zero))
        run(rc, rl, carry)
        for u_ref, y_ref, base, rows in segs:
            for h in range(S5_PAIRS):
                s_r = sre[base:base + rows, h * 128:(h + 1) * 128].astype(BF16)
                s_i = sim[base:base + rows, h * 128:(h + 1) * 128].astype(BF16)
                y = (jnp.dot(s_r, rre_ref[d, h], preferred_element_type=F32)
                     + jnp.dot(s_i, rim_ref[d, h], preferred_element_type=F32))
                for gl in range(2):
                    g = 2 * h + gl
                    yg = y[:, gl * gw:(gl + 1) * gw] + jnp.dot(
                        u_ref[0, :, g * gw:(g + 1) * gw], m_ref[d, g], preferred_element_type=F32)
                    if d == 0:
                        y_ref[0, :, g * gw:(g + 1) * gw] = yg
                    else:
                        y_ref[0, :, g * gw:(g + 1) * gw] += yg


def _to_s5_rows(u):
    b, n, _ = u.shape
    u = u.reshape(b, n // S5_BLK, S5_BLK, S5_GROUPS, S5_GROUP)
    return jnp.transpose(u, (0, 1, 3, 2, 4)).reshape(b, n // S5_BLK, S5_ROW)


def _from_s5_rows(y):
    b, r, _ = y.shape
    y = y.reshape(b, r, S5_GROUPS, S5_BLK, S5_GROUP)
    return jnp.transpose(y, (0, 1, 3, 2, 4)).reshape(b, r * S5_BLK, BRANCH_W)


def s5_scan_readout(u_ctx, u_lat, ops):
    m_op, w_re, w_im, r_re, r_im, a_blk = ops
    b = u_lat.shape[0]
    uc, ul = _to_s5_rows(u_ctx), _to_s5_rows(u_lat)
    rc, rl = uc.shape[1], ul.shape[1]
    per_b = lambda bi: (bi, 0, 0)
    c4 = lambda bi: (0, 0, 0, 0)
    yc, yl = pl.pallas_call(
        _s5_kernel,
        grid=(b,),
        in_specs=[
            pl.BlockSpec((1, rc, S5_ROW), per_b),
            pl.BlockSpec((1, rl, S5_ROW), per_b),
            pl.BlockSpec(m_op.shape, c4),
            pl.BlockSpec(w_re.shape, c4),
            pl.BlockSpec(w_im.shape, c4),
            pl.BlockSpec(r_re.shape, c4),
            pl.BlockSpec(r_im.shape, c4),
            pl.BlockSpec(a_blk.shape, c4),
        ],
        out_specs=[pl.BlockSpec((1, rc, S5_ROW), per_b), pl.BlockSpec((1, rl, S5_ROW), per_b)],
        out_shape=[jax.ShapeDtypeStruct((b, rc, S5_ROW), F32), jax.ShapeDtypeStruct((b, rl, S5_ROW), F32)],
        scratch_shapes=[pltpu.VMEM((rc + rl, S5_LANES), F32) for _ in range(4)],
        compiler_params=_params("parallel"),
        name="s5_scan",
    )(uc, ul, m_op, w_re, w_im, r_re, r_im, a_blk)
    return _from_s5_rows(yc), _from_s5_rows(yl)


LA_TILE = 128


def _stack_heads(q):
    return jnp.concatenate(
        [jnp.where(_head_lane_mask(h, q.shape), q, jnp.zeros_like(q)) for h in range(N_HEADS)], axis=0)


def _unstack_heads(o_stack, t):
    out = jnp.zeros((t, o_stack.shape[1]), o_stack.dtype)
    for h in range(N_HEADS):
        blk = o_stack[h * t:(h + 1) * t]
        out = jnp.where(_head_lane_mask(h, blk.shape), blk, out)
    return out


def _same_head_block(shape):
    r = lax.broadcasted_iota(jnp.int32, shape, 0) // HEAD_DIM
    c = lax.broadcasted_iota(jnp.int32, shape, 1) // HEAD_DIM
    return r == c


_NT = (((1,), (1,)), ((), ()))


def _state_update(st_ref, decay_lane, v, k_scaled):
    vt = jnp.transpose(v.astype(F32)).astype(BF16)
    kv = jnp.dot(vt, k_scaled.astype(BF16), preferred_element_type=F32)
    st_ref[...] = decay_lane * st_ref[...] + jnp.where(_same_head_block(kv.shape), kv, 0.0)


def _for_tiles(n_tiles, reverse, body):
    def step(i, carry):
        body(n_tiles - 1 - i if reverse else i)
        return carry
    lax.fori_loop(0, n_tiles, step, 0)


def _ret_kernel(qc_ref, kc_ref, vc_ref, gc_ref, ql_ref, kl_ref, vl_ref, gl_ref, dmask_ref, xi_ref, zeta_ref,
                gam_ref, havg_ref, oc_ref, ol_ref, st_ref, accc_ref, accl_ref):
    t = LA_TILE
    segs = ((qc_ref, kc_ref, vc_ref, accc_ref), (ql_ref, kl_ref, vl_ref, accl_ref))
    for d in range(2):
        st_ref[...] = jnp.zeros_like(st_ref)
        for q_ref, k_ref, v_ref, acc_ref in segs:
            def tile(i, q_ref=q_ref, k_ref=k_ref, v_ref=v_ref, acc_ref=acc_ref):
                rows = pl.ds(pl.multiple_of(i * t, t), t)
                q, k, v = q_ref[0, rows, :], k_ref[0, rows, :], v_ref[0, rows, :]
                att = lax.dot_general(_stack_heads(q), k, _NT, preferred_element_type=F32) * dmask_ref[d]
                intra = _unstack_heads(jnp.dot(att.astype(BF16), v, preferred_element_type=F32), t)
                cross = lax.dot_general(q, st_ref[...].astype(BF16), _NT, preferred_element_type=F32)
                o = intra + cross * xi_ref[d]
                if d == 0:
                    acc_ref[rows, :] = o
                else:
                    acc_ref[rows, :] += o
                _state_update(st_ref, gam_ref[d], v, k.astype(F32) * zeta_ref[d])
            _for_tiles(q_ref.shape[1] // t, d == 1, tile)
    for acc_ref, g_ref, o_ref in ((accc_ref, gc_ref, oc_ref), (accl_ref, gl_ref, ol_ref)):
        o = acc_ref[...]
        o = o * lax.rsqrt(_head_mean_sq(o, havg_ref[...]) + EPS)
        o_ref[0] = (o * jax.nn.silu(g_ref[0].astype(F32))).astype(o_ref.dtype)


def retention_tables(decay_logit):
    t = LA_TILE
    lg = jax.nn.log_sigmoid(decay_logit)
    idx = jnp.arange(t, dtype=F32)
    diff = idx[:, None] - idx[None, :]
    lgm = lg[:, :, None, None]
    fwd = jnp.where(diff >= 0, jnp.exp(jnp.maximum(diff, 0.0) * lgm[0]), 0.0)
    bwd = jnp.where(diff <= 0, jnp.exp(jnp.maximum(-diff, 0.0) * lgm[1]), 0.0)
    dmask = jnp.stack([fwd.reshape(N_HEADS * t, t), bwd.reshape(N_HEADS * t, t)])
    lane = lambda a: jnp.repeat(a, HEAD_DIM, axis=-1)
    lg_l = lane(lg)[:, None, :]
    steps_q = jnp.stack([idx + 1.0, t - idx])[:, :, None]
    steps_k = jnp.stack([t - 1.0 - idx, idx])[:, :, None]
    xi = jnp.exp(steps_q * lg_l)
    zeta = jnp.exp(steps_k * lg_l)
    gam = jnp.exp(t * lg_l)
    return dmask, xi, zeta, gam


def retention_pallas(rt_ctx, rt_lat, decay_logit):
    b, l, _ = rt_lat.shape
    lc = rt_ctx.shape[1]
    dmask, xi, zeta, gam = retention_tables(decay_logit)
    col = lambda n, c: pl.BlockSpec((1, n, BRANCH_W), lambda bi: (bi, 0, c))
    const = lambda a: pl.BlockSpec(a.shape, lambda bi: (0,) * a.ndim)
    havg = head_avg_matrix()
    return pl.pallas_call(
        _ret_kernel,
        grid=(b,),
        in_specs=[col(lc, c) for c in range(4)] + [col(l, c) for c in range(4)]
        + [const(dmask), const(xi), const(zeta), const(gam), const(havg)],
        out_specs=[col(lc, 0), col(l, 0)],
        out_shape=[jax.ShapeDtypeStruct((b, lc, BRANCH_W), BF16), jax.ShapeDtypeStruct((b, l, BRANCH_W), BF16)],
        scratch_shapes=[pltpu.VMEM((BRANCH_W, BRANCH_W), F32), pltpu.VMEM((lc, BRANCH_W), F32),
                        pltpu.VMEM((l, BRANCH_W), F32)],
        compiler_params=_params("parallel"),
        name="retention",
    )(rt_ctx, rt_ctx, rt_ctx, rt_ctx, rt_lat, rt_lat, rt_lat, rt_lat, dmask, xi, zeta, gam, havg)


HG_LEVELS = (32, 64, 128)


def _hg_level_masks(d):
    t = LA_TILE
    i = lax.broadcasted_iota(jnp.int32, (N_HEADS * t, t), 0) % t
    j = lax.broadcasted_iota(jnp.int32, (N_HEADS * t, t), 1)
    base = (i // HG_CHUNK == j // HG_CHUNK) & ((j <= i) if d == 0 else (j >= i))
    masks = [base]
    for blk in HG_LEVELS:
        qi_late = (i % blk) >= blk // 2
        kj_late = (j % blk) >= blk // 2
        cross = (qi_late & ~kj_late) if d == 0 else (~qi_late & kj_late)
        masks.append((i // blk == j // blk) & cross)
    return masks


def _hg_tile(d, q, k, v, logf, st_ref, tri):
    t = LA_TILE
    w = q.shape[1]
    g = jnp.dot(tri, logf, precision=lax.Precision.HIGHEST, preferred_element_type=F32)
    nb = t // HG_CHUNK
    g3 = g.reshape(nb, HG_CHUNK, w)
    if d == 0:
        edge = g3[:, HG_CHUNK - 1:HG_CHUNK, :]
        prev = jnp.concatenate([jnp.zeros((1, 1, w), F32), edge[:-1]], axis=0)
    else:
        edge = g3[:, 0:1, :]
        prev = jnp.concatenate([edge[1:], jnp.zeros((1, 1, w), F32)], axis=0)
    cum = (g3 - prev).reshape(t, w)
    masks = _hg_level_masks(d)
    qs = [q * jnp.exp(cum)]
    ks = [k * jnp.exp(-cum)]
    for blk in HG_LEVELS:
        gb = g.reshape(t // blk, blk, w)
        row = blk // 2 - 1 if d == 0 else blk // 2
        mid = jnp.broadcast_to(gb[:, row:row + 1, :], gb.shape).reshape(t, w)
        qs.append(q * jnp.exp(jnp.minimum(g - mid, 0.0)))
        ks.append(k * jnp.exp(jnp.minimum(mid - g, 0.0)))
    att = None
    for qq, kk, m in zip(qs, ks, masks):
        a = lax.dot_general(_stack_heads(qq.astype(BF16)), kk.astype(BF16), _NT, preferred_element_type=F32)
        a = jnp.where(m, a, 0.0)
        att = a if att is None else att + a
    intra = _unstack_heads(jnp.dot(att.astype(BF16), v, preferred_element_type=F32), t)
    cross = lax.dot_general((q * jnp.exp(g)).astype(BF16), st_ref[...].astype(BF16), _NT,
                            preferred_element_type=F32)
    total = g[t - 1:t, :] if d == 0 else g[0:1, :]
    return intra + cross, jnp.exp(total), k * jnp.exp(total - g)


def _hg_kernel(pc_ref, pl_ref, lb_ref, nw_ref, havg_ref, oc_ref, ol_ref, st_ref, accc_ref, accl_ref):
    t = LA_TILE
    w = BRANCH_W
    lb = lb_ref[...]
    r = lax.broadcasted_iota(jnp.int32, (t, t), 0)
    c = lax.broadcasted_iota(jnp.int32, (t, t), 1)
    for d in range(2):
        tri = jnp.where((c <= r) if d == 0 else (c >= r), 1.0, 0.0).astype(F32)
        st_ref[...] = jnp.zeros_like(st_ref)
        for p_ref, acc_ref in ((pc_ref, accc_ref), (pl_ref, accl_ref)):
            def tile(i, p_ref=p_ref, acc_ref=acc_ref):
                rows = pl.ds(pl.multiple_of(i * t, t), t)
                q = jax.nn.silu(p_ref[0, rows, 0:w].astype(F32))
                f_logit = p_ref[0, rows, (1 + d) * w:(2 + d) * w].astype(F32)
                v = p_ref[0, rows, 3 * w:4 * w]
                fg = lb + (1.0 - lb) * jax.nn.sigmoid(f_logit)
                o, decay, k_end = _hg_tile(d, q, 1.0 - fg, v, jnp.log(fg), st_ref, tri)
                if d == 0:
                    acc_ref[rows, :] = o
                else:
                    acc_ref[rows, :] += o
                _state_update(st_ref, decay, v, k_end)
            _for_tiles(p_ref.shape[1] // t, d == 1, tile)
    for acc_ref, p_ref, o_ref in ((accc_ref, pc_ref, oc_ref), (accl_ref, pl_ref, ol_ref)):
        o = acc_ref[...]
        o = o * lax.rsqrt(_head_mean_sq(o, havg_ref[...]) + EPS) * nw_ref[...]
        o_ref[0] = (o * jax.nn.silu(p_ref[0, :, 4 * w:5 * w].astype(F32))).astype(o_ref.dtype)


def hgrn2_pallas(hg_ctx, hg_lat, lower_bound, norm_w):
    b, l, width = hg_lat.shape
    lc = hg_ctx.shape[1]
    full = lambda n: pl.BlockSpec((1, n, width), lambda bi: (bi, 0, 0))
    out = lambda n: pl.BlockSpec((1, n, BRANCH_W), lambda bi: (bi, 0, 0))
    vec = pl.BlockSpec((1, BRANCH_W), lambda bi: (0, 0))
    havg = head_avg_matrix()
    return pl.pallas_call(
        _hg_kernel,
        grid=(b,),
        in_specs=[full(lc), full(l), vec, vec, pl.BlockSpec(havg.shape, lambda bi: (0, 0))],
        out_specs=[out(lc), out(l)],
        out_shape=[jax.ShapeDtypeStruct((b, lc, BRANCH_W), BF16), jax.ShapeDtypeStruct((b, l, BRANCH_W), BF16)],
        scratch_shapes=[pltpu.VMEM((BRANCH_W, BRANCH_W), F32), pltpu.VMEM((lc, BRANCH_W), F32),
                        pltpu.VMEM((l, BRANCH_W), F32)],
        compiler_params=_params("parallel"),
        name="hgrn2",
    )(hg_ctx, hg_lat, lower_bound[None], jnp.tile(norm_w, N_HEADS)[None], havg)


def _rms(x, w=None):
    y = x * lax.rsqrt(jnp.mean(jnp.square(x), axis=-1, keepdims=True) + EPS)
    return y if w is None else y * w


def _to_heads(t):
    b, l, _ = t.shape
    return jnp.transpose(t.reshape(b, l, N_HEADS, -1), (0, 2, 1, 3))


def _from_heads(t):
    b, h, l, d = t.shape
    return jnp.transpose(t, (0, 2, 1, 3)).reshape(b, l, h * d)


def _s5_scan(u, lam_re, lam_im, log_step, b_re, b_im, s0_re, s0_im):
    dt = jnp.exp(log_step)[:, None]
    mag = jnp.exp(lam_re * dt)
    ar = mag * jnp.cos(lam_im * dt)
    ai = mag * jnp.sin(lam_im * dt)
    den = lam_re * lam_re + lam_im * lam_im
    zr = ((ar - 1.0) * lam_re + ai * lam_im) / den
    zi = (ai * lam_re - (ar - 1.0) * lam_im) / den
    bb_re = zr[..., None] * b_re - zi[..., None] * b_im
    bb_im = zr[..., None] * b_im + zi[..., None] * b_re
    bu_re = jnp.einsum('blgp,gnp->blgn', u, bb_re)
    bu_im = jnp.einsum('blgp,gnp->blgn', u, bb_im)
    bu_re = bu_re.at[:, 0].add(ar * s0_re - ai * s0_im)
    bu_im = bu_im.at[:, 0].add(ar * s0_im + ai * s0_re)
    a_re = jnp.broadcast_to(ar, bu_re.shape)
    a_im = jnp.broadcast_to(ai, bu_im.shape)

    def combine(e1, e2):
        a1r, a1i, b1r, b1i = e1
        a2r, a2i, b2r, b2i = e2
        return (a1r * a2r - a1i * a2i, a1r * a2i + a1i * a2r,
                a2r * b1r - a2i * b1i + b2r, a2r * b1i + a2i * b1r + b2i)

    _, _, x_re, x_im = lax.associative_scan(combine, (a_re, a_im, bu_re, bu_im), axis=1)
    return x_re, x_im


def _s5_readout(x_re, x_im, c_re, c_im):
    return jnp.einsum('blgn,gpn->blgp', x_re, c_re) - jnp.einsum('blgn,gpn->blgp', x_im, c_im)


def s5_mixer(u_ctx, u_lat, lam_re, lam_im, log_step, b_re, b_im, c_re, c_im, d_skip, w_glu, with_ctx_out):
    uc, ul = u_ctx.astype(F32), u_lat.astype(F32)
    gc = uc.reshape(uc.shape[0], uc.shape[1], S5_GROUPS, S5_GROUP)
    gl = ul.reshape(ul.shape[0], ul.shape[1], S5_GROUPS, S5_GROUP)
    zero = jnp.zeros((ul.shape[0], S5_GROUPS, S5_STATE), F32)
    y_ctx, y_lat = 0.0, 0.0
    for d in range(2):
        rev = (lambda t: t) if d == 0 else (lambda t: jnp.flip(t, axis=1))
        prm = (lam_re[d], lam_im[d], log_step[d], b_re[d], b_im[d])
        xc_re, xc_im = _s5_scan(rev(gc), *prm, zero, zero)
        xl_re, xl_im = _s5_scan(rev(gl), *prm, xc_re[:, -1], xc_im[:, -1])
        y_lat = y_lat + rev(_s5_readout(xl_re, xl_im, c_re[d], c_im[d]))
        if with_ctx_out:
            y_ctx = y_ctx + rev(_s5_readout(xc_re, xc_im, c_re[d], c_im[d]))

    def finish(y, u):
        y = y.reshape(u.shape) + d_skip * u
        z = jax.nn.gelu(y)
        return (z * jax.nn.sigmoid(z @ w_glu)).astype(BF16)

    return (finish(y_ctx, uc) if with_ctx_out else None), finish(y_lat, ul)


def neighbourhood_attention(qkv_ctx, qkv_lat, q_norm_w, k_norm_w, rpb, with_ctx_out):
    qkv_ctx, qkv_lat = qkv_ctx.astype(F32), qkv_lat.astype(F32)
    b, l, _ = qkv_lat.shape
    lc = qkv_ctx.shape[1]
    rows = l // GRID_W
    wr = min(NA_ROWS, rows)
    scale = HEAD_DIM ** -0.5

    def split(t, n):
        q, k, v = jnp.split(t, 3, axis=-1)
        q = _rms(q.reshape(b, n, N_HEADS, HEAD_DIM), q_norm_w)
        k = _rms(k.reshape(b, n, N_HEADS, HEAD_DIM), k_norm_w)
        return q, k, v.reshape(b, n, N_HEADS, HEAD_DIM)

    qc, kc, vc = split(qkv_ctx, lc)
    ql, kl, vl = split(qkv_lat, l)
    qg = ql.reshape(b, rows, GRID_W, N_HEADS, HEAD_DIM)
    kg = kl.reshape(b, rows, GRID_W, N_HEADS, HEAD_DIM)
    vg = vl.reshape(b, rows, GRID_W, N_HEADS, HEAD_DIM)
    r = np.arange(rows)
    row_idx = np.clip(r - wr // 2, 0, rows - wr)[:, None] + np.arange(wr)[None, :]
    kband = kg[:, row_idx]
    vband = vg[:, row_idx]
    col = np.arange(GRID_W)
    col_start = np.clip(col - NA_COLS // 2, 0, GRID_W - NA_COLS)
    col_ok = (col[None, :] >= col_start[:, None]) & (col[None, :] < col_start[:, None] + NA_COLS)
    dr_idx = (row_idx - r[:, None] + NA_ROWS - 1).reshape(rows, 1, wr, 1)
    dc_idx = np.clip(col[None, :] - col[:, None] + NA_COLS - 1, 0, 2 * NA_COLS - 2).reshape(1, GRID_W, 1, GRID_W)
    bias = rpb[:, dr_idx, dc_idx]
    s_loc = jnp.einsum('brihd,brajhd->bhriaj', qg, kband) * scale + bias[None]
    s_loc = jnp.where(jnp.asarray(col_ok).reshape(1, 1, 1, GRID_W, 1, GRID_W), s_loc, -jnp.inf)
    s_ctx = jnp.einsum('brihd,bchd->bhric', qg, kc) * scale
    n_loc = wr * GRID_W
    s_all = jnp.concatenate([s_loc.reshape(b, N_HEADS, rows, GRID_W, n_loc), s_ctx], axis=-1)
    p = jax.nn.softmax(s_all, axis=-1)
    p_loc = p[..., :n_loc].reshape(b, N_HEADS, rows, GRID_W, wr, GRID_W)
    p_ctx = p[..., n_loc:]
    o_lat = jnp.einsum('bhriaj,brajhd->brihd', p_loc, vband) + jnp.einsum('bhric,bchd->brihd', p_ctx, vc)
    o_lat = o_lat.reshape(b, l, BRANCH_W).astype(BF16)
    o_ctx = None
    if with_ctx_out:
        sc = jnp.einsum('bqhd,bkhd->bhqk', qc, kc) * scale
        pc = jax.nn.softmax(sc, axis=-1)
        o_ctx = jnp.einsum('bhqk,bkhd->bqhd', pc, vc).reshape(b, lc, BRANCH_W).astype(BF16)
    return o_ctx, o_lat


def _bidir_prefix(run, ctx_dirs, lat_dirs, with_ctx_out):
    o_ctx, o_lat = None, None
    for d in range(2):
        rev = (lambda t: t) if d == 0 else (lambda t: jnp.flip(t, axis=2))
        oc, sc = run(d, [rev(t) for t in ctx_dirs[d]], None)
        ol, _ = run(d, [rev(t) for t in lat_dirs[d]], sc)
        o_lat = rev(ol) if o_lat is None else o_lat + rev(ol)
        if with_ctx_out:
            o_ctx = rev(oc) if o_ctx is None else o_ctx + rev(oc)
    return o_ctx, o_lat


def _blocks(t, chunk):
    b, h, l, d = t.shape
    return jnp.moveaxis(t.reshape(b, h, l // chunk, chunk, d), 2, 0)


def _unblocks(t):
    n, b, h, c, d = t.shape
    return jnp.moveaxis(t, 0, 2).reshape(b, h, n * c, d)


def _gated_chunk_scan(q, k, v, log_f, s0, chunk):
    b, h, _, dk = q.shape
    dv = v.shape[-1]
    if s0 is None:
        s0 = jnp.zeros((b, h, dk, dv), F32)
    qb, kb, vb, gb = _blocks(q, chunk), _blocks(k, chunk), _blocks(v, chunk), _blocks(log_f, chunk)
    cum = jnp.cumsum(gb, axis=3)
    last = cum[:, :, :, -1:, :]
    q_dec = qb * jnp.exp(cum)
    k_dec = kb * jnp.exp(-cum)
    k_end = kb * jnp.exp(last - cum)
    causal = jnp.tril(jnp.ones((chunk, chunk), bool))
    att = jnp.where(causal, jnp.einsum('nbhid,nbhjd->nbhij', q_dec, k_dec), 0.0)
    intra = jnp.einsum('nbhij,nbhje->nbhie', att, vb)
    kv = jnp.einsum('nbhjd,nbhje->nbhde', k_end, vb)
    decay_end = jnp.exp(last[:, :, :, 0, :])

    def step(s, xs):
        kv_n, d_n = xs
        return d_n[..., None] * s + kv_n, s

    s_fin, s_prev = lax.scan(step, s0, (kv, decay_end))
    cross = jnp.einsum('nbhid,nbhde->nbhie', q_dec, s_prev)
    return _unblocks(intra + cross), s_fin


def hgrn2_mixer(p_ctx, p_lat, lb, norm_w, with_ctx_out):
    def prep(p):
        q, f_fw, f_bw, i, g = jnp.split(p.astype(F32), 5, axis=-1)
        qh = _to_heads(jax.nn.silu(q))
        vh = _to_heads(i)
        dirs = []
        for f_logit in (f_fw, f_bw):
            fg = lb + (1.0 - lb) * jax.nn.sigmoid(f_logit)
            dirs.append((qh, _to_heads(1.0 - fg), vh, _to_heads(jnp.log(fg))))
        return dirs, g

    ctx_dirs, g_ctx = prep(p_ctx)
    lat_dirs, g_lat = prep(p_lat)
    run = lambda d, a, s0: _gated_chunk_scan(a[0], a[1], a[2], a[3], s0, HG_CHUNK)
    o_ctx, o_lat = _bidir_prefix(run, ctx_dirs, lat_dirs, with_ctx_out)

    def finish(o, g):
        return (_from_heads(_rms(o, norm_w)) * jax.nn.silu(g)).astype(BF16)

    return (finish(o_ctx, g_ctx) if with_ctx_out else None), finish(o_lat, g_lat)


def _retention_chunk_scan(q, k, v, log_gamma, s0, chunk):
    b, h, _, dk = q.shape
    dv = v.shape[-1]
    if s0 is None:
        s0 = jnp.zeros((b, h, dk, dv), F32)
    qb, kb, vb = _blocks(q, chunk), _blocks(k, chunk), _blocks(v, chunk)
    idx = jnp.arange(chunk, dtype=F32)
    diff = idx[:, None] - idx[None, :]
    lg = log_gamma[:, None, None]
    decay = jnp.where(diff >= 0, jnp.exp(jnp.maximum(diff, 0.0) * lg), 0.0)
    xi = jnp.exp((idx + 1.0)[None, :] * log_gamma[:, None])[..., None]
    zeta = jnp.exp((chunk - 1.0 - idx)[None, :] * log_gamma[:, None])[..., None]
    gamma_c = jnp.exp(chunk * log_gamma)[:, None, None]
    intra = jnp.einsum('nbhij,nbhje->nbhie', jnp.einsum('nbhid,nbhjd->nbhij', qb, kb) * decay, vb)
    kv = jnp.einsum('nbhjd,nbhje->nbhde', kb * zeta, vb)

    def step(s, kv_n):
        return gamma_c * s + kv_n, s

    s_fin, s_prev = lax.scan(step, s0, kv)
    cross = jnp.einsum('nbhid,nbhde->nbhie', qb, s_prev) * xi
    return _unblocks(intra + cross), s_fin


def _axial_rope(n_tokens):
    t = np.arange(n_tokens)
    quarter = HEAD_DIM // 4
    inv = ROPE_BASE ** (-np.arange(quarter) / quarter)
    ang = np.concatenate([(t // GRID_W)[:, None] * inv, (t % GRID_W)[:, None] * inv], axis=1)
    return jnp.asarray(np.cos(ang), F32), jnp.asarray(np.sin(ang), F32)


def _apply_rope(x, cos, sin):
    half = x.shape[-1] // 2
    x1, x2 = x[..., :half], x[..., half:]
    cos = cos[None, :, None, :]
    sin = sin[None, :, None, :]
    return jnp.concatenate([x1 * cos - x2 * sin, x2 * cos + x1 * sin], axis=-1)


def retention_mixer(p_ctx, p_lat, decay_logit, with_ctx_out):
    cos, sin = _axial_rope(p_lat.shape[1])

    def prep(p, rope):
        b, n, _ = p.shape
        q, k, v, g = jnp.split(p.astype(F32), 4, axis=-1)
        q = q.reshape(b, n, N_HEADS, HEAD_DIM)
        k = k.reshape(b, n, N_HEADS, HEAD_DIM)
        if rope:
            q, k = _apply_rope(q, cos, sin), _apply_rope(k, cos, sin)
        qh = jnp.transpose(q, (0, 2, 1, 3))
        kh = jnp.transpose(k, (0, 2, 1, 3)) * (HEAD_DIM ** -0.5)
        return (qh, kh, _to_heads(v)), g

    qkv_c, g_ctx = prep(p_ctx, False)
    qkv_l, g_lat = prep(p_lat, True)
    log_gamma = jax.nn.log_sigmoid(decay_logit)
    run = lambda d, a, s0: _retention_chunk_scan(a[0], a[1], a[2], log_gamma[d], s0, RET_CHUNK)
    o_ctx, o_lat = _bidir_prefix(run, [qkv_c, qkv_c], [qkv_l, qkv_l], with_ctx_out)

    def finish(o, g):
        return (_from_heads(_rms(o)) * jax.nn.silu(g)).astype(BF16)

    return (finish(o_ctx, g_ctx) if with_ctx_out else None), finish(o_lat, g_lat)


def kernel(x, c, ctx, c_ctx, ada_w, ada_b, norm_mix_w, norm_ffn_w, w_in, s5_lam_re, s5_lam_im, s5_log_step,
           s5_b_re, s5_b_im, s5_c_re, s5_c_im, s5_d, s5_glu_w, na_q_norm, na_k_norm, na_rpb, hg_lower_bounds,
           hg_norm_w, ret_decay_logit, w_branch, w_out, router_w, ex_w_gate, ex_w_up, ex_w_down):
    b = x.shape[0]
    lb_p = jax.nn.softmax(hg_lower_bounds, axis=0)
    lower_bounds = jnp.cumsum(lb_p, axis=0) - lb_p[0]
    cond_lat = jax.nn.silu(c)
    cond_ctx = jnp.broadcast_to(jax.nn.silu(c_ctx)[None], c.shape)
    hi = lax.Precision.HIGHEST
    xc = ctx
    for li in range(DEPTH):
        last = li == DEPTH - 1
        mod_l = [m[:, None] for m in jnp.split(jnp.dot(cond_lat, ada_w[li], precision=hi) + ada_b[li], 6, axis=-1)]
        mod_c = [m[:, None] for m in jnp.split(jnp.dot(cond_ctx, ada_w[li], precision=hi) + ada_b[li], 6, axis=-1)]
        sh1_l, sc1_l, g1_l, sh2_l, sc2_l, g2_l = mod_l
        sh1_c, sc1_c, g1_c, sh2_c, sc2_c, g2_c = mod_c
        w_in_b = w_in[li].astype(BF16)
        wb_b = w_branch[li].astype(BF16)
        wo_b = w_out[li].astype(BF16)
        rw_t = router_w[li].T
        nmw = norm_mix_w[li][None]
        nfw = norm_ffn_w[li][None]

        rt_off = sum(IN_SPLITS[:3])
        w_rot_b = jnp.concatenate(
            [_swap_head_halves_cols(w_in[li][:, rt_off + i * BRANCH_W:rt_off + (i + 1) * BRANCH_W]) for i in range(2)],
            axis=1).astype(BF16)
        qk_w = jnp.stack([jnp.tile(na_q_norm[li], N_HEADS), jnp.tile(na_k_norm[li], N_HEADS)])
        s5_ops = s5_operators(s5_lam_re[li], s5_lam_im[li], s5_log_step[li], s5_b_re[li], s5_b_im[li],
                              s5_c_re[li], s5_c_im[li])
        s5_dl = s5_d[li][None]
        glu_b = s5_glu_w[li].astype(BF16)

        pl_ = in_proj(x, nmw, sh1_l, sc1_l, w_in_b, w_rot_b, qk_w, True, 512)
        pc_ = in_proj(xc, nmw, sh1_c, sc1_c, w_in_b, w_rot_b, qk_w, False, 256)
        y5_c, y5_l = s5_scan_readout(pc_[0], pl_[0], s5_ops)
        na_l = na_latent(pl_[1], pc_[1], na_bias(na_rpb[li], x.shape[1] // GRID_W))
        hg_c, hg_l = hgrn2_pallas(pc_[2], pl_[2], lower_bounds[li], hg_norm_w[li])
        rt_c, rt_l = retention_pallas(pc_[3], pl_[3], ret_decay_logit[li])

        x, h_l, aff_l = merge_and_route(y5_l, pl_[0], (na_l, hg_l, rt_l), pl_[4], s5_dl, glu_b, wb_b, wo_b, x, g1_l,
                                        nfw, sh2_l, sc2_l, rw_t, 512)
        x = x + g2_l * expert_choice_ffn(h_l, aff_l, ex_w_gate[li], ex_w_up[li], ex_w_down[li])
        if not last:
            na_c = na_context(pc_[1])
            xc, h_c, aff_c = merge_and_route(y5_c, pc_[0], (na_c, hg_c, rt_c), pc_[4], s5_dl, glu_b, wb_b, wo_b, xc,
                                             g1_c, nfw, sh2_c, sc2_c, rw_t, 256)
            xc = xc + g2_c * expert_choice_ffn(h_c, aff_c, ex_w_gate[li], ex_w_up[li], ex_w_down[li])
    return x
```

```python
import functools
import math

import jax
import jax.numpy as jnp
import numpy as np
from jax import lax
from jax.experimental import pallas as pl
from jax.experimental.pallas import tpu as pltpu

D_MODEL = 1024
DEPTH = 2
GRID_W = 64
N_BRANCH = 4
BRANCH_W = 256
HEAD_DIM = 64
N_HEADS = BRANCH_W // HEAD_DIM
S5_GROUP = 16
S5_GROUPS = BRANCH_W // S5_GROUP
S5_STATE = 64
NA_ROWS = 8
NA_COLS = 16
HG_CHUNK = 16
RET_CHUNK = 128
N_EXPERTS = 16
EC_CAPACITY = 2
D_EXPERT = 2816
ROPE_BASE = 10000.0
EPS = 1e-6
IN_SPLITS = (BRANCH_W, 3 * BRANCH_W, 5 * BRANCH_W, 4 * BRANCH_W, N_BRANCH * D_MODEL)
D_IN = sum(IN_SPLITS)

F32 = jnp.float32
BF16 = jnp.bfloat16

V7X_VMEM_BYTES = 64 * 1024 * 1024
VMEM_LIMIT = V7X_VMEM_BYTES - 8 * 1024 * 1024


def _params(*sem):
    return pltpu.CompilerParams(dimension_semantics=sem, vmem_limit_bytes=VMEM_LIMIT)


def _norm_mod(x, norm_w, shift, scale):
    y = x * lax.rsqrt(jnp.mean(x * x, axis=-1, keepdims=True) + EPS) * norm_w
    return y * (1.0 + scale) + shift


def _ada_kernel(c_ref, w_ref, b_ref, o_ref):
    cond = jax.nn.silu(c_ref[...])
    o_ref[0] = jnp.dot(cond, w_ref[0], precision=lax.Precision.HIGHEST, preferred_element_type=F32) + b_ref[0]


def ada_modulation(cond_in, ada_w, ada_b, tn):
    r, d = cond_in.shape
    depth, _, n = ada_w.shape
    return pl.pallas_call(
        _ada_kernel,
        grid=(depth, n // tn),
        in_specs=[
            pl.BlockSpec((r, d), lambda l, j: (0, 0)),
            pl.BlockSpec((1, d, tn), lambda l, j: (l, 0, j)),
            pl.BlockSpec((1, 1, tn), lambda l, j: (l, 0, j)),
        ],
        out_specs=pl.BlockSpec((1, r, tn), lambda l, j: (l, 0, j)),
        out_shape=jax.ShapeDtypeStruct((depth, r, n), F32),
        compiler_params=_params("parallel", "parallel"),
        name="ada_modulation",
    )(cond_in, ada_w, ada_b[:, None, :])


def _head_mean_sq(t, head_avg):
    sq = t * t
    hi = sq.astype(BF16)
    lo = (sq - hi.astype(F32)).astype(BF16)
    return (jnp.dot(hi, head_avg, preferred_element_type=F32)
            + jnp.dot(lo, head_avg, preferred_element_type=F32))


def _in_proj_kernel(x_ref, nw_ref, sh_ref, sc_ref, w_ref, wrot_ref, qkw_ref, havg_ref, cos_ref, sin_ref,
                    s5_ref, na_ref, hg_ref, rt_ref, gate_ref, *, rope):
    hb = _norm_mod(x_ref[0], nw_ref[...], sh_ref[0], sc_ref[0]).astype(BF16)
    cw = BRANCH_W
    proj = lambda col: jnp.dot(hb, w_ref[:, col * cw:(col + 1) * cw], preferred_element_type=F32)
    col = 0
    s5_ref[0] = proj(col).astype(BF16)
    col += 1
    for part in range(3):
        t = proj(col + part)
        if part < 2:
            t = t * lax.rsqrt(_head_mean_sq(t, havg_ref[...]) + EPS) * qkw_ref[part:part + 1, :]
        if part == 0:
            t = t * (HEAD_DIM ** -0.5)
        na_ref[0, :, part * cw:(part + 1) * cw] = t.astype(BF16)
    col += 3
    for part in range(5):
        hg_ref[0, :, part * cw:(part + 1) * cw] = proj(col + part).astype(BF16)
    col += 5
    for part in range(4):
        t = proj(col + part)
        if part < 2 and rope:
            swapped = jnp.dot(hb, wrot_ref[:, part * cw:(part + 1) * cw], preferred_element_type=F32)
            t = t * cos_ref[...] + swapped * sin_ref[...]
        if part == 1:
            t = t * (HEAD_DIM ** -0.5)
        rt_ref[0, :, part * cw:(part + 1) * cw] = t.astype(BF16)
    col += 4
    for part in range(N_BRANCH * D_MODEL // cw):
        gate_ref[0, :, part * cw:(part + 1) * cw] = proj(col + part).astype(BF16)


def rope_tables(n_tokens):
    quarter = HEAD_DIM // 4
    t = jnp.arange(n_tokens, dtype=jnp.int32)
    inv = jnp.asarray(ROPE_BASE ** (-np.arange(quarter) / quarter), F32)
    ang = jnp.concatenate([(t // GRID_W).astype(F32)[:, None] * inv, (t % GRID_W).astype(F32)[:, None] * inv], axis=1)
    cos, sin = jnp.cos(ang), jnp.sin(ang)
    cos_h = jnp.concatenate([cos, cos], axis=1)
    sin_h = jnp.concatenate([-sin, sin], axis=1)
    return jnp.tile(cos_h, (1, N_HEADS)), jnp.tile(sin_h, (1, N_HEADS))


def _swap_head_halves_cols(w):
    d, c = w.shape
    w = w.reshape(d, c // HEAD_DIM, 2, HEAD_DIM // 2)
    return w[:, :, ::-1, :].reshape(d, c)


def head_avg_matrix():
    h = np.arange(BRANCH_W) // HEAD_DIM
    return jnp.asarray((h[:, None] == h[None, :]) / HEAD_DIM, BF16)


def in_proj(x, norm_w, shift, scale, w_in_bf16, w_rot_bf16, qk_norm_w, rope, tm):
    b, n, d = x.shape
    tok = lambda bi, i: (bi, i, 0)
    per_b = lambda bi, i: (bi, 0, 0)
    const2 = lambda bi, i: (0, 0)
    cos, sin = rope_tables(n)
    return pl.pallas_call(
        functools.partial(_in_proj_kernel, rope=rope),
        grid=(b, n // tm),
        in_specs=[
            pl.BlockSpec((1, tm, d), tok),
            pl.BlockSpec((1, d), const2),
            pl.BlockSpec((1, 1, d), per_b),
            pl.BlockSpec((1, 1, d), per_b),
            pl.BlockSpec((d, D_IN), const2, pipeline_mode=pl.Buffered(1)),
            pl.BlockSpec((d, 2 * BRANCH_W), const2, pipeline_mode=pl.Buffered(1)),
            pl.BlockSpec((2, BRANCH_W), const2),
            pl.BlockSpec((BRANCH_W, BRANCH_W), const2),
            pl.BlockSpec((tm, BRANCH_W), lambda bi, i: (i, 0)),
            pl.BlockSpec((tm, BRANCH_W), lambda bi, i: (i, 0)),
        ],
        out_specs=[pl.BlockSpec((1, tm, w), tok) for w in IN_SPLITS],
        out_shape=[jax.ShapeDtypeStruct((b, n, w), BF16) for w in IN_SPLITS],
        compiler_params=_params("parallel", "parallel"),
        name="in_proj",
    )(x, norm_w, shift, scale, w_in_bf16, w_rot_bf16, qk_norm_w, head_avg_matrix(), cos, sin)


NA_QROWS = 8
NA_KROWS = 16
NA_TQ = NA_QROWS * GRID_W
NA_TK = NA_KROWS * GRID_W
NA_MASKED = -1e30


def na_bias(rpb, rows):
    hp = lax.Precision.HIGHEST
    nblk = rows // NA_QROWS
    n_dr, n_dc = 2 * NA_ROWS - 1, 2 * NA_COLS - 1
    qc = np.arange(GRID_W)[:, None]
    kc = np.arange(GRID_W)[None, :]
    cs = np.clip(qc - NA_COLS // 2, 0, GRID_W - NA_COLS)
    col_ok = (kc >= cs) & (kc < cs + NA_COLS)
    dc = np.clip(kc - qc + NA_COLS - 1, 0, n_dc - 1).reshape(-1)
    sel_dc = jnp.asarray(dc[None, :] == np.arange(n_dc)[:, None], F32)
    by_col = jnp.einsum('hrc,cx->hrx', rpb.astype(F32), sel_dc, precision=hp)
    pats = []
    for g in (0, 1, nblk - 1):
        ks = int(np.clip(NA_QROWS * g - NA_ROWS // 2, 0, rows - NA_KROWS))
        qr = (NA_QROWS * g + np.arange(NA_QROWS))[:, None]
        kr = (ks + np.arange(NA_KROWS))[None, :]
        band = np.clip(qr - NA_ROWS // 2, 0, rows - NA_ROWS)
        row_ok = (kr >= band) & (kr < band + NA_ROWS)
        dr = np.clip(kr - qr + NA_ROWS - 1, 0, n_dr - 1).reshape(-1)
        sel_dr = jnp.asarray(dr[:, None] == np.arange(n_dr)[None, :], F32)
        t = jnp.einsum('yr,hrx->hyx', sel_dr, by_col, precision=hp)
        t = t.reshape(N_HEADS, NA_QROWS, NA_KROWS, GRID_W, GRID_W)
        t = jnp.transpose(t, (0, 1, 3, 2, 4)).reshape(N_HEADS, NA_TQ, NA_TK)
        ok = (row_ok[:, None, :, None] & col_ok[None, :, None, :]).reshape(NA_TQ, NA_TK)
        pats.append(jnp.where(jnp.asarray(ok)[None], t, NA_MASKED))
    return jnp.stack(pats)


def _head_lane_mask(h, shape):
    lane = lax.broadcasted_iota(jnp.int32, shape, len(shape) - 1)
    return (lane >= h * HEAD_DIM) & (lane < (h + 1) * HEAD_DIM)


def _na_kernel(q_ref, k_ref, v_ref, kc_ref, vc_ref, bias_ref, o_ref, *, rows):
    g = pl.program_id(1)
    ks = jnp.clip(NA_QROWS * g - NA_ROWS // 2, 0, rows - NA_KROWS)
    start = pl.multiple_of(ks * GRID_W, GRID_W * (NA_ROWS // 2))
    q = q_ref[0]
    k_win = k_ref[0, pl.ds(start, NA_TK), :]
    v_win = v_ref[0, pl.ds(start, NA_TK), :]
    kc, vc = kc_ref[0], vc_ref[0]
    nt = (((1,), (1,)), ((), ()))
    out = jnp.zeros(q.shape, F32)
    for h in range(N_HEADS):
        hm = _head_lane_mask(h, q.shape)
        qh = jnp.where(hm, q, jnp.zeros_like(q))
        s_loc = lax.dot_general(qh, k_win, nt, preferred_element_type=F32) + bias_ref[0, h]
        s_ctx = lax.dot_general(qh, kc, nt, preferred_element_type=F32)
        m = jnp.maximum(jnp.max(s_loc, axis=-1, keepdims=True), jnp.max(s_ctx, axis=-1, keepdims=True))
        p_loc = jnp.exp(s_loc - m)
        p_ctx = jnp.exp(s_ctx - m)
        denom = jnp.sum(p_loc, axis=-1, keepdims=True) + jnp.sum(p_ctx, axis=-1, keepdims=True)
        oh = (jnp.dot(p_loc.astype(BF16), v_win, preferred_element_type=F32)
              + jnp.dot(p_ctx.astype(BF16), vc, preferred_element_type=F32)) / denom
        out = jnp.where(hm, oh, out)
    o_ref[0] = out.astype(o_ref.dtype)


def na_latent(na_lat, na_ctx, bias):
    b, l, _ = na_lat.shape
    lc = na_ctx.shape[1]
    rows = l // GRID_W
    nblk = rows // NA_QROWS
    pat = lambda bi, g: (jnp.where(g == 0, 0, jnp.where(g == nblk - 1, 2, 1)), 0, 0, 0)
    return pl.pallas_call(
        functools.partial(_na_kernel, rows=rows),
        grid=(b, nblk),
        in_specs=[
            pl.BlockSpec((1, NA_TQ, BRANCH_W), lambda bi, g: (bi, g, 0)),
            pl.BlockSpec((1, l, BRANCH_W), lambda bi, g: (bi, 0, 1)),
            pl.BlockSpec((1, l, BRANCH_W), lambda bi, g: (bi, 0, 2)),
            pl.BlockSpec((1, lc, BRANCH_W), lambda bi, g: (bi, 0, 1)),
            pl.BlockSpec((1, lc, BRANCH_W), lambda bi, g: (bi, 0, 2)),
            pl.BlockSpec((1, N_HEADS, NA_TQ, NA_TK), pat),
        ],
        out_specs=pl.BlockSpec((1, NA_TQ, BRANCH_W), lambda bi, g: (bi, g, 0)),
        out_shape=jax.ShapeDtypeStruct((b, l, BRANCH_W), BF16),
        compiler_params=_params("parallel", "arbitrary"),
        name="na_latent",
    )(na_lat, na_lat, na_lat, na_ctx, na_ctx, bias)


def _na_ctx_kernel(q_ref, k_ref, v_ref, o_ref):
    q, k, v = q_ref[0], k_ref[0], v_ref[0]
    nt = (((1,), (1,)), ((), ()))
    out = jnp.zeros(q.shape, F32)
    for h in range(N_HEADS):
        hm = _head_lane_mask(h, q.shape)
        s = lax.dot_general(jnp.where(hm, q, jnp.zeros_like(q)), k, nt, preferred_element_type=F32)
        p = jnp.exp(s - jnp.max(s, axis=-1, keepdims=True))
        oh = jnp.dot(p.astype(BF16), v, preferred_element_type=F32) / jnp.sum(p, axis=-1, keepdims=True)
        out = jnp.where(hm, oh, out)
    o_ref[0] = out.astype(o_ref.dtype)


def na_context(na_ctx):
    b, lc, _ = na_ctx.shape
    spec = lambda col: pl.BlockSpec((1, lc, BRANCH_W), lambda bi: (bi, 0, col))
    return pl.pallas_call(
        _na_ctx_kernel,
        grid=(b,),
        in_specs=[spec(0), spec(1), spec(2)],
        out_specs=pl.BlockSpec((1, lc, BRANCH_W), lambda bi: (bi, 0, 0)),
        out_shape=jax.ShapeDtypeStruct((b, lc, BRANCH_W), BF16),
        compiler_params=_params("parallel"),
        name="na_context",
    )(na_ctx, na_ctx, na_ctx)


def _merge_kernel(y5_ref, u5_ref, o1_ref, o2_ref, o3_ref, gate_ref, d5_ref, wglu_ref, wb_ref, wo_ref, x_ref, g1_ref,
                  nw_ref, sh_ref, sc_ref, rw_ref, xo_ref, h_ref, aff_ref):
    z = jax.nn.gelu(y5_ref[0] + d5_ref[...] * u5_ref[0].astype(F32))
    o_s5 = (z * jax.nn.sigmoid(jnp.dot(z.astype(BF16), wglu_ref[...], preferred_element_type=F32))).astype(BF16)
    m = None
    for br in range(N_BRANCH):
        o_br = o_s5 if br == 0 else (o1_ref, o2_ref, o3_ref)[br - 1][0]
        proj = jnp.dot(o_br, wb_ref[br], preferred_element_type=F32)
        gate = jax.nn.sigmoid(gate_ref[0, :, br * D_MODEL:(br + 1) * D_MODEL].astype(F32))
        m = gate * proj if m is None else m + gate * proj
    mix = jnp.dot(m.astype(BF16), wo_ref[...], preferred_element_type=F32)
    x_new = x_ref[0] + g1_ref[0] * mix
    xo_ref[0] = x_new
    h = _norm_mod(x_new, nw_ref[...], sh_ref[0], sc_ref[0])
    h_ref[0] = h.astype(BF16)
    logits = lax.dot_general(rw_ref[...], h, (((1,), (1,)), ((), ())),
                             precision=lax.Precision.HIGHEST, preferred_element_type=F32)
    logits = logits - jnp.max(logits, axis=0, keepdims=True)
    e = jnp.exp(logits)
    aff_ref[0] = e / jnp.sum(e, axis=0, keepdims=True)


def merge_and_route(y_s5, u_s5, outs, gates, s5_d, s5_glu_bf16, w_branch_bf16, w_out_bf16, x, g1,
                    norm_w, shift, scale, router_w_t, tm):
    b, n, d = x.shape
    tok = lambda bi, i: (bi, i, 0)
    per_b = lambda bi, i: (bi, 0, 0)
    const2 = lambda bi, i: (0, 0)
    const3 = lambda bi, i: (0, 0, 0)
    return pl.pallas_call(
        _merge_kernel,
        grid=(b, n // tm),
        in_specs=[pl.BlockSpec((1, tm, BRANCH_W), tok)] * (N_BRANCH + 1) + [
            pl.BlockSpec((1, tm, N_BRANCH * d), tok),
            pl.BlockSpec((1, BRANCH_W), const2),
            pl.BlockSpec((BRANCH_W, BRANCH_W), const2),
            pl.BlockSpec((N_BRANCH, BRANCH_W, d), const3),
            pl.BlockSpec((d, d), const2),
            pl.BlockSpec((1, tm, d), tok),
            pl.BlockSpec((1, 1, d), per_b),
            pl.BlockSpec((1, d), const2),
            pl.BlockSpec((1, 1, d), per_b),
            pl.BlockSpec((1, 1, d), per_b),
            pl.BlockSpec((N_EXPERTS, d), const2),
        ],
        out_specs=[
            pl.BlockSpec((1, tm, d), tok),
            pl.BlockSpec((1, tm, d), tok),
            pl.BlockSpec((1, N_EXPERTS, tm), lambda bi, i: (bi, 0, i)),
        ],
        out_shape=[
            jax.ShapeDtypeStruct((b, n, d), F32),
            jax.ShapeDtypeStruct((b, n, d), BF16),
            jax.ShapeDtypeStruct((b, N_EXPERTS, n), F32),
        ],
        compiler_params=_params("parallel", "parallel"),
        name="merge_route",
    )(y_s5, u_s5, *outs, gates, s5_d, s5_glu_bf16, w_branch_bf16, w_out_bf16, x, g1, norm_w, shift, scale,
      router_w_t)


MXU_N = 256


def _expert_kernel(x_ref, g_ref, g2_ref, wg_ref, wu_ref, wd_ref, y_ref, act_ref):
    xb = x_ref[0, 0]
    f = wg_ref.shape[2]
    for c0 in range(0, f, MXU_N):
        c1 = min(c0 + MXU_N, f)
        gate = jnp.dot(xb, wg_ref[0, :, c0:c1], preferred_element_type=F32)
        up = jnp.dot(xb, wu_ref[0, :, c0:c1], preferred_element_type=F32)
        act_ref[:, c0:c1] = (jax.nn.silu(gate) * up).astype(BF16)
    y = jnp.dot(act_ref[...], wd_ref[0], preferred_element_type=F32)
    y_ref[0, 0] = (y * g_ref[0, 0] * g2_ref[0]).astype(y_ref.dtype)


def expert_ffn(xe, ge, g2, w_gate, w_up, w_down):
    e, b, cap, d = xe.shape
    f = w_gate.shape[-1]
    return pl.pallas_call(
        _expert_kernel,
        grid=(e, b),
        in_specs=[
            pl.BlockSpec((1, 1, cap, d), lambda ei, bi: (ei, bi, 0, 0)),
            pl.BlockSpec((1, 1, cap, 1), lambda ei, bi: (ei, bi, 0, 0)),
            pl.BlockSpec((1, 1, d), lambda ei, bi: (bi, 0, 0)),
            pl.BlockSpec((1, d, f), lambda ei, bi: (ei, 0, 0)),
            pl.BlockSpec((1, d, f), lambda ei, bi: (ei, 0, 0)),
            pl.BlockSpec((1, f, d), lambda ei, bi: (ei, 0, 0)),
        ],
        out_specs=pl.BlockSpec((1, 1, cap, d), lambda ei, bi: (ei, bi, 0, 0)),
        out_shape=jax.ShapeDtypeStruct((e, b, cap, d), BF16),
        scratch_shapes=[pltpu.VMEM((cap, f), BF16)],
        compiler_params=_params("parallel", "arbitrary"),
        name="expert_ffn",
    )(xe, ge, g2, w_gate, w_up, w_down)


COMBINE_UNROLL = 8


def _combine_kernel(idx_ref, y_ref, x_hbm, o_ref, yf_ref, sem):
    bi = pl.program_id(0)
    e = pl.program_id(1)
    cap = y_ref.shape[2]

    @pl.when(e == 0)
    def _():
        cp = pltpu.make_async_copy(x_hbm.at[bi], o_ref.at[0], sem)
        cp.start()
        cp.wait()

    yf_ref[...] = y_ref[0, 0].astype(F32)

    def body(i, carry):
        row = idx_ref[0, 0, e * cap + i]
        o_ref[0, pl.ds(row, 1), :] += yf_ref[pl.ds(i, 1), :]
        return carry

    lax.fori_loop(0, cap, body, 0, unroll=COMBINE_UNROLL)


def moe_combine(x, y, idx):
    b, n, d = x.shape
    e, _, cap, _ = y.shape
    return pl.pallas_call(
        _combine_kernel,
        grid=(b, e),
        in_specs=[
            pl.BlockSpec((1, 1, e * cap), lambda bi, ei: (bi, 0, 0), memory_space=pltpu.SMEM),
            pl.BlockSpec((1, 1, cap, d), lambda bi, ei: (ei, bi, 0, 0)),
            pl.BlockSpec(memory_space=pl.ANY),
        ],
        out_specs=pl.BlockSpec((1, n, d), lambda bi, ei: (bi, 0, 0)),
        out_shape=jax.ShapeDtypeStruct((b, n, d), F32),
        scratch_shapes=[pltpu.VMEM((cap, d), F32), pltpu.SemaphoreType.DMA(())],
        compiler_params=_params("parallel", "arbitrary"),
        name="moe_combine",
    )(idx, y, x)


def expert_choice_ffn(x, g2, h_bf16, aff_t, w_gate, w_up, w_down):
    b, n, d = h_bf16.shape
    cap = EC_CAPACITY * n // N_EXPERTS
    g, idx = lax.top_k(aff_t, cap)
    idx_e = jnp.moveaxis(idx, 1, 0)
    g_e = jnp.moveaxis(g, 1, 0)[..., None]
    xe = h_bf16[jnp.arange(b)[None, :, None], idx_e]
    y = expert_ffn(xe, g_e, g2, w_gate, w_up, w_down)
    return moe_combine(x, y, idx.reshape(b, 1, N_EXPERTS * cap).astype(jnp.int32))


S5_BLK = 16
S5_ROW = S5_GROUPS * S5_BLK * S5_GROUP
S5_PAIRS = S5_GROUPS // 2
S5_LANES = S5_GROUPS * S5_STATE


def _pair_blockdiag(t):
    g, r, c = t.shape
    t = t.reshape(g // 2, 2, r, c)
    z = jnp.zeros_like(t[:, 0])
    top = jnp.concatenate([t[:, 0], z], axis=-1)
    bot = jnp.concatenate([z, t[:, 1]], axis=-1)
    return jnp.concatenate([top, bot], axis=-2)


def s5_operators(lam_re, lam_im, log_step, b_re, b_im, c_re, c_im):
    hp = lax.Precision.HIGHEST
    blk = S5_BLK
    dt = jnp.exp(log_step)[..., None]
    k = jnp.arange(blk + 1, dtype=F32)
    mag = jnp.exp((lam_re * dt)[..., None] * k)
    ang = (lam_im * dt)[..., None] * k
    pr, pi = mag * jnp.cos(ang), mag * jnp.sin(ang)
    ar, ai = pr[..., 1], pi[..., 1]
    den = lam_re * lam_re + lam_im * lam_im
    zr = ((ar - 1.0) * lam_re + ai * lam_im) / den
    zi = (ai * lam_re - (ar - 1.0) * lam_im) / den
    bb_re = zr[..., None] * b_re - zi[..., None] * b_im
    bb_im = zr[..., None] * b_im + zi[..., None] * b_re
    ca_re = c_re[..., None] * pr[:, :, None] - c_im[..., None] * pi[:, :, None]
    ca_im = c_re[..., None] * pi[:, :, None] + c_im[..., None] * pr[:, :, None]
    kern = (jnp.einsum('dgpnl,dgnq->dglpq', ca_re, bb_re, precision=hp)
            - jnp.einsum('dgpnl,dgnq->dglpq', ca_im, bb_im, precision=hp))
    j = np.arange(blk)[:, None]
    i = np.arange(blk)[None, :]
    ms, ws, rres, rims = [], [], [], []
    for d in range(2):
        lag = (i - j) if d == 0 else (j - i)
        valid = jnp.asarray(lag >= 0, F32)[None, :, None, :, None]
        kd = kern[d][:, np.clip(lag, 0, blk - 1)]
        m = jnp.transpose(kd, (0, 1, 4, 2, 3)) * valid
        ms.append(m.reshape(S5_GROUPS, blk * S5_GROUP, blk * S5_GROUP))
        pw = (blk - 1 - np.arange(blk)) if d == 0 else np.arange(blk)
        apr, api = pr[d][..., pw], pi[d][..., pw]
        w_re = apr[..., None] * bb_re[d][:, :, None] - api[..., None] * bb_im[d][:, :, None]
        w_im = apr[..., None] * bb_im[d][:, :, None] + api[..., None] * bb_re[d][:, :, None]
        to_w = lambda t: jnp.transpose(t, (0, 2, 3, 1)).reshape(S5_GROUPS, blk * S5_GROUP, S5_STATE)
        ws.append((_pair_blockdiag(to_w(w_re)), _pair_blockdiag(to_w(w_im))))
        ex = (np.arange(blk) + 1) if d == 0 else (blk - np.arange(blk))
        r_re = ca_re[d][..., ex]
        r_im = -ca_im[d][..., ex]
        to_r = lambda t: jnp.transpose(t, (0, 2, 3, 1)).reshape(S5_GROUPS, S5_STATE, blk * S5_GROUP)
        rres.append(_pair_blockdiag(to_r(r_re)))
        rims.append(_pair_blockdiag(to_r(r_im)))
    m_op = jnp.stack(ms).astype(BF16)
    w_re = jnp.stack([w[0] for w in ws]).astype(BF16)
    w_im = jnp.stack([w[1] for w in ws]).astype(BF16)
    r_re = jnp.stack(rres).astype(BF16)
    r_im = jnp.stack(rims).astype(BF16)
    a_blk = jnp.stack([pr[..., blk].reshape(2, 1, S5_LANES), pi[..., blk].reshape(2, 1, S5_LANES)], axis=1)
    return m_op, w_re, w_im, r_re, r_im, a_blk


def _s5_kernel(uc_ref, ul_ref, m_ref, wre_ref, wim_ref, rre_ref, rim_ref, a_ref, yc_ref, yl_ref,
               vre, vim, sre, sim):
    rc, rl = uc_ref.shape[1], ul_ref.shape[1]
    segs = ((uc_ref, yc_ref, 0, rc), (ul_ref, yl_ref, rc, rl))
    gw = S5_BLK * S5_GROUP
    for d in range(2):
        for u_ref, _, base, rows in segs:
            for h in range(S5_PAIRS):
                u_pair = u_ref[0, :, 2 * h * gw:2 * (h + 1) * gw]
                vre[base:base + rows, h * 128:(h + 1) * 128] = jnp.dot(
                    u_pair, wre_ref[d, h], preferred_element_type=F32)
                vim[base:base + rows, h * 128:(h + 1) * 128] = jnp.dot(
                    u_pair, wim_ref[d, h], preferred_element_type=F32)
        ar, ai = a_ref[d, 0], a_ref[d, 1]

        def run(base, rows, carry):
            def step(t, c):
                xr, xi = c
                idx = base + (t if d == 0 else rows - 1 - t)
                sre[pl.ds(idx, 1), :] = xr
                sim[pl.ds(idx, 1), :] = xi
                nr = ar * xr - ai * xi + vre[pl.ds(idx, 1), :]
                ni = ar * xi + ai * xr + vim[pl.ds(idx, 1), :]
                return nr, ni
            return lax.fori_loop(0, rows, step, carry)

        zero = jnp.zeros((1, S5_LANES), F32)
        carry = run(0, rc, (zero, zero))
        run(rc, rl, carry)
        for u_ref, y_ref, base, rows in segs:
            for h in range(S5_PAIRS):
                s_r = sre[base:base + rows, h * 128:(h + 1) * 128].astype(BF16)
                s_i = sim[base:base + rows, h * 128:(h + 1) * 128].astype(BF16)
                y = (jnp.dot(s_r, rre_ref[d, h], preferred_element_type=F32)
                     + jnp.dot(s_i, rim_ref[d, h], preferred_element_type=F32))
                for gl in range(2):
                    g = 2 * h + gl
                    yg = y[:, gl * gw:(gl + 1) * gw] + jnp.dot(
                        u_ref[0, :, g * gw:(g + 1) * gw], m_ref[d, g], preferred_element_type=F32)
                    if d == 0:
                        y_ref[0, :, g * gw:(g + 1) * gw] = yg
                    else:
                        y_ref[0, :, g * gw:(g + 1) * gw] += yg


def _to_s5_rows(u):
    b, n, _ = u.shape
    u = u.reshape(b, n // S5_BLK, S5_BLK, S5_GROUPS, S5_GROUP)
    return jnp.transpose(u, (0, 1, 3, 2, 4)).reshape(b, n // S5_BLK, S5_ROW)


def _from_s5_rows(y):
    b, r, _ = y.shape
    y = y.reshape(b, r, S5_GROUPS, S5_BLK, S5_GROUP)
    return jnp.transpose(y, (0, 1, 3, 2, 4)).reshape(b, r * S5_BLK, BRANCH_W)


def s5_scan_readout(u_ctx, u_lat, ops):
    m_op, w_re, w_im, r_re, r_im, a_blk = ops
    b = u_lat.shape[0]
    uc, ul = _to_s5_rows(u_ctx), _to_s5_rows(u_lat)
    rc, rl = uc.shape[1], ul.shape[1]
    per_b = lambda bi: (bi, 0, 0)
    c4 = lambda bi: (0, 0, 0, 0)
    yc, yl = pl.pallas_call(
        _s5_kernel,
        grid=(b,),
        in_specs=[
            pl.BlockSpec((1, rc, S5_ROW), per_b),
            pl.BlockSpec((1, rl, S5_ROW), per_b),
            pl.BlockSpec(m_op.shape, c4),
            pl.BlockSpec(w_re.shape, c4),
            pl.BlockSpec(w_im.shape, c4),
            pl.BlockSpec(r_re.shape, c4),
            pl.BlockSpec(r_im.shape, c4),
            pl.BlockSpec(a_blk.shape, c4),
        ],
        out_specs=[pl.BlockSpec((1, rc, S5_ROW), per_b), pl.BlockSpec((1, rl, S5_ROW), per_b)],
        out_shape=[jax.ShapeDtypeStruct((b, rc, S5_ROW), F32), jax.ShapeDtypeStruct((b, rl, S5_ROW), F32)],
        scratch_shapes=[pltpu.VMEM((rc + rl, S5_LANES), F32) for _ in range(4)],
        compiler_params=_params("parallel"),
        name="s5_scan",
    )(uc, ul, m_op, w_re, w_im, r_re, r_im, a_blk)
    return _from_s5_rows(yc), _from_s5_rows(yl)


LA_TILE = 128


def _stack_heads(q):
    return jnp.concatenate(
        [jnp.where(_head_lane_mask(h, q.shape), q, jnp.zeros_like(q)) for h in range(N_HEADS)], axis=0)


def _unstack_heads(o_stack, t):
    out = jnp.zeros((t, o_stack.shape[1]), o_stack.dtype)
    for h in range(N_HEADS):
        blk = o_stack[h * t:(h + 1) * t]
        out = jnp.where(_head_lane_mask(h, blk.shape), blk, out)
    return out


def _same_head_block(shape):
    r = lax.broadcasted_iota(jnp.int32, shape, 0) // HEAD_DIM
    c = lax.broadcasted_iota(jnp.int32, shape, 1) // HEAD_DIM
    return r == c


_NT = (((1,), (1,)), ((), ()))


def _state_update(st_ref, decay_lane, v, k_scaled):
    vt = jnp.transpose(v.astype(F32)).astype(BF16)
    kv = jnp.dot(vt, k_scaled.astype(BF16), preferred_element_type=F32)
    st_ref[...] = decay_lane * st_ref[...] + jnp.where(_same_head_block(kv.shape), kv, 0.0)


def _for_tiles(n_tiles, reverse, body):
    def step(i, carry):
        body(n_tiles - 1 - i if reverse else i)
        return carry
    lax.fori_loop(0, n_tiles, step, 0)


def _ret_kernel(qc_ref, kc_ref, vc_ref, gc_ref, ql_ref, kl_ref, vl_ref, gl_ref, dmask_ref, xi_ref, zeta_ref,
                gam_ref, havg_ref, oc_ref, ol_ref, st_ref, accc_ref, accl_ref):
    t = LA_TILE
    segs = ((qc_ref, kc_ref, vc_ref, accc_ref), (ql_ref, kl_ref, vl_ref, accl_ref))
    for d in range(2):
        st_ref[...] = jnp.zeros_like(st_ref)
        for q_ref, k_ref, v_ref, acc_ref in segs:
            def tile(i, q_ref=q_ref, k_ref=k_ref, v_ref=v_ref, acc_ref=acc_ref):
                rows = pl.ds(pl.multiple_of(i * t, t), t)
                q, k, v = q_ref[0, rows, :], k_ref[0, rows, :], v_ref[0, rows, :]
                att = lax.dot_general(_stack_heads(q), k, _NT, preferred_element_type=F32) * dmask_ref[d]
                intra = _unstack_heads(jnp.dot(att.astype(BF16), v, preferred_element_type=F32), t)
                cross = lax.dot_general(q, st_ref[...].astype(BF16), _NT, preferred_element_type=F32)
                o = intra + cross * xi_ref[d]
                if d == 0:
                    acc_ref[rows, :] = o
                else:
                    acc_ref[rows, :] += o
                _state_update(st_ref, gam_ref[d], v, k.astype(F32) * zeta_ref[d])
            _for_tiles(q_ref.shape[1] // t, d == 1, tile)
    for acc_ref, g_ref, o_ref in ((accc_ref, gc_ref, oc_ref), (accl_ref, gl_ref, ol_ref)):
        o = acc_ref[...]
        o = o * lax.rsqrt(_head_mean_sq(o, havg_ref[...]) + EPS)
        o_ref[0] = (o * jax.nn.silu(g_ref[0].astype(F32))).astype(o_ref.dtype)


def retention_tables(decay_logit):
    t = LA_TILE
    lg = jax.nn.log_sigmoid(decay_logit)
    idx = jnp.arange(t, dtype=F32)
    diff = idx[:, None] - idx[None, :]
    lgm = lg[:, :, None, None]
    fwd = jnp.where(diff >= 0, jnp.exp(jnp.maximum(diff, 0.0) * lgm[0]), 0.0)
    bwd = jnp.where(diff <= 0, jnp.exp(jnp.maximum(-diff, 0.0) * lgm[1]), 0.0)
    dmask = jnp.stack([fwd.reshape(N_HEADS * t, t), bwd.reshape(N_HEADS * t, t)])
    lane = lambda a: jnp.repeat(a, HEAD_DIM, axis=-1)
    lg_l = lane(lg)[:, None, :]
    steps_q = jnp.stack([idx + 1.0, t - idx])[:, :, None]
    steps_k = jnp.stack([t - 1.0 - idx, idx])[:, :, None]
    xi = jnp.exp(steps_q * lg_l)
    zeta = jnp.exp(steps_k * lg_l)
    gam = jnp.exp(t * lg_l)
    return dmask, xi, zeta, gam


def retention_pallas(rt_ctx, rt_lat, decay_logit):
    b, l, _ = rt_lat.shape
    lc = rt_ctx.shape[1]
    dmask, xi, zeta, gam = retention_tables(decay_logit)
    col = lambda n, c: pl.BlockSpec((1, n, BRANCH_W), lambda bi: (bi, 0, c))
    const = lambda a: pl.BlockSpec(a.shape, lambda bi: (0,) * a.ndim)
    havg = head_avg_matrix()
    return pl.pallas_call(
        _ret_kernel,
        grid=(b,),
        in_specs=[col(lc, c) for c in range(4)] + [col(l, c) for c in range(4)]
        + [const(dmask), const(xi), const(zeta), const(gam), const(havg)],
        out_specs=[col(lc, 0), col(l, 0)],
        out_shape=[jax.ShapeDtypeStruct((b, lc, BRANCH_W), BF16), jax.ShapeDtypeStruct((b, l, BRANCH_W), BF16)],
        scratch_shapes=[pltpu.VMEM((BRANCH_W, BRANCH_W), F32), pltpu.VMEM((lc, BRANCH_W), F32),
                        pltpu.VMEM((l, BRANCH_W), F32)],
        compiler_params=_params("parallel"),
        name="retention",
    )(rt_ctx, rt_ctx, rt_ctx, rt_ctx, rt_lat, rt_lat, rt_lat, rt_lat, dmask, xi, zeta, gam, havg)


HG_LEVELS = (32, 64, 128)


def _hg_level_masks(d):
    t = LA_TILE
    i = lax.broadcasted_iota(jnp.int32, (N_HEADS * t, t), 0) % t
    j = lax.broadcasted_iota(jnp.int32, (N_HEADS * t, t), 1)
    base = (i // HG_CHUNK == j // HG_CHUNK) & ((j <= i) if d == 0 else (j >= i))
    masks = [base]
    for blk in HG_LEVELS:
        qi_late = (i % blk) >= blk // 2
        kj_late = (j % blk) >= blk // 2
        cross = (qi_late & ~kj_late) if d == 0 else (~qi_late & kj_late)
        masks.append((i // blk == j // blk) & cross)
    return masks


def _hg_tile(d, q, k, v, logf, st_ref, tri):
    t = LA_TILE
    w = q.shape[1]
    g = jnp.dot(tri, logf, precision=lax.Precision.HIGHEST, preferred_element_type=F32)
    nb = t // HG_CHUNK
    g3 = g.reshape(nb, HG_CHUNK, w)
    if d == 0:
        edge = g3[:, HG_CHUNK - 1:HG_CHUNK, :]
        prev = jnp.concatenate([jnp.zeros((1, 1, w), F32), edge[:-1]], axis=0)
    else:
        edge = g3[:, 0:1, :]
        prev = jnp.concatenate([edge[1:], jnp.zeros((1, 1, w), F32)], axis=0)
    cum = (g3 - prev).reshape(t, w)
    masks = _hg_level_masks(d)
    qs = [q * jnp.exp(cum)]
    ks = [k * jnp.exp(-cum)]
    for blk in HG_LEVELS:
        gb = g.reshape(t // blk, blk, w)
        row = blk // 2 - 1 if d == 0 else blk // 2
        mid = jnp.broadcast_to(gb[:, row:row + 1, :], gb.shape).reshape(t, w)
        qs.append(q * jnp.exp(jnp.minimum(g - mid, 0.0)))
        ks.append(k * jnp.exp(jnp.minimum(mid - g, 0.0)))
    att = None
    for qq, kk, m in zip(qs, ks, masks):
        a = lax.dot_general(_stack_heads(qq.astype(BF16)), kk.astype(BF16), _NT, preferred_element_type=F32)
        a = jnp.where(m, a, 0.0)
        att = a if att is None else att + a
    intra = _unstack_heads(jnp.dot(att.astype(BF16), v, preferred_element_type=F32), t)
    cross = lax.dot_general((q * jnp.exp(g)).astype(BF16), st_ref[...].astype(BF16), _NT,
                            preferred_element_type=F32)
    total = g[t - 1:t, :] if d == 0 else g[0:1, :]
    return intra + cross, jnp.exp(total), k * jnp.exp(total - g)


def _hg_kernel(pc_ref, pl_ref, lb_ref, nw_ref, havg_ref, oc_ref, ol_ref, st_ref, accc_ref, accl_ref):
    t = LA_TILE
    w = BRANCH_W
    lb = lb_ref[...]
    r = lax.broadcasted_iota(jnp.int32, (t, t), 0)
    c = lax.broadcasted_iota(jnp.int32, (t, t), 1)
    for d in range(2):
        tri = jnp.where((c <= r) if d == 0 else (c >= r), 1.0, 0.0).astype(F32)
        st_ref[...] = jnp.zeros_like(st_ref)
        for p_ref, acc_ref in ((pc_ref, accc_ref), (pl_ref, accl_ref)):
            def tile(i, p_ref=p_ref, acc_ref=acc_ref):
                rows = pl.ds(pl.multiple_of(i * t, t), t)
                q = jax.nn.silu(p_ref[0, rows, 0:w].astype(F32))
                f_logit = p_ref[0, rows, (1 + d) * w:(2 + d) * w].astype(F32)
                v = p_ref[0, rows, 3 * w:4 * w]
                fg = lb + (1.0 - lb) * jax.nn.sigmoid(f_logit)
                o, decay, k_end = _hg_tile(d, q, 1.0 - fg, v, jnp.log(fg), st_ref, tri)
                if d == 0:
                    acc_ref[rows, :] = o
                else:
                    acc_ref[rows, :] += o
                _state_update(st_ref, decay, v, k_end)
            _for_tiles(p_ref.shape[1] // t, d == 1, tile)
    for acc_ref, p_ref, o_ref in ((accc_ref, pc_ref, oc_ref), (accl_ref, pl_ref, ol_ref)):
        o = acc_ref[...]
        o = o * lax.rsqrt(_head_mean_sq(o, havg_ref[...]) + EPS) * nw_ref[...]
        o_ref[0] = (o * jax.nn.silu(p_ref[0, :, 4 * w:5 * w].astype(F32))).astype(o_ref.dtype)


def hgrn2_pallas(hg_ctx, hg_lat, lower_bound, norm_w):
    b, l, width = hg_lat.shape
    lc = hg_ctx.shape[1]
    full = lambda n: pl.BlockSpec((1, n, width), lambda bi: (bi, 0, 0))
    out = lambda n: pl.BlockSpec((1, n, BRANCH_W), lambda bi: (bi, 0, 0))
    vec = pl.BlockSpec((1, BRANCH_W), lambda bi: (0, 0))
    havg = head_avg_matrix()
    return pl.pallas_call(
        _hg_kernel,
        grid=(b,),
        in_specs=[full(lc), full(l), vec, vec, pl.BlockSpec(havg.shape, lambda bi: (0, 0))],
        out_specs=[out(lc), out(l)],
        out_shape=[jax.ShapeDtypeStruct((b, lc, BRANCH_W), BF16), jax.ShapeDtypeStruct((b, l, BRANCH_W), BF16)],
        scratch_shapes=[pltpu.VMEM((BRANCH_W, BRANCH_W), F32), pltpu.VMEM((lc, BRANCH_W), F32),
                        pltpu.VMEM((l, BRANCH_W), F32)],
        compiler_params=_params("parallel"),
        name="hgrn2",
    )(hg_ctx, hg_lat, lower_bound[None], jnp.tile(norm_w, N_HEADS)[None], havg)


def kernel(x, c, ctx, c_ctx, ada_w, ada_b, norm_mix_w, norm_ffn_w, w_in, s5_lam_re, s5_lam_im, s5_log_step,
           s5_b_re, s5_b_im, s5_c_re, s5_c_im, s5_d, s5_glu_w, na_q_norm, na_k_norm, na_rpb, hg_lower_bounds,
           hg_norm_w, ret_decay_logit, w_branch, w_out, router_w, ex_w_gate, ex_w_up, ex_w_down):
    b = x.shape[0]
    lb_p = jax.nn.softmax(hg_lower_bounds, axis=0)
    lower_bounds = jnp.cumsum(lb_p, axis=0) - lb_p[0]
    cond_rows = jnp.concatenate([c, jnp.broadcast_to(c_ctx[None], c.shape)], axis=0)
    mods = ada_modulation(cond_rows, ada_w, ada_b, D_MODEL)
    xc = ctx
    for li in range(DEPTH):
        last = li == DEPTH - 1
        mod_l = [m[:, None] for m in jnp.split(mods[li, :b], 6, axis=-1)]
        mod_c = [m[:, None] for m in jnp.split(mods[li, b:], 6, axis=-1)]
        sh1_l, sc1_l, g1_l, sh2_l, sc2_l, g2_l = mod_l
        sh1_c, sc1_c, g1_c, sh2_c, sc2_c, g2_c = mod_c
        w_in_b = w_in[li].astype(BF16)
        wb_b = w_branch[li].astype(BF16)
        wo_b = w_out[li].astype(BF16)
        rw_t = router_w[li].T
        nmw = norm_mix_w[li][None]
        nfw = norm_ffn_w[li][None]

        rt_off = sum(IN_SPLITS[:3])
        w_rot_b = jnp.concatenate(
            [_swap_head_halves_cols(w_in[li][:, rt_off + i * BRANCH_W:rt_off + (i + 1) * BRANCH_W]) for i in range(2)],
            axis=1).astype(BF16)
        qk_w = jnp.stack([jnp.tile(na_q_norm[li], N_HEADS), jnp.tile(na_k_norm[li], N_HEADS)])
        s5_ops = s5_operators(s5_lam_re[li], s5_lam_im[li], s5_log_step[li], s5_b_re[li], s5_b_im[li],
                              s5_c_re[li], s5_c_im[li])
        s5_dl = s5_d[li][None]
        glu_b = s5_glu_w[li].astype(BF16)

        pl_ = in_proj(x, nmw, sh1_l, sc1_l, w_in_b, w_rot_b, qk_w, True, 512)
        pc_ = in_proj(xc, nmw, sh1_c, sc1_c, w_in_b, w_rot_b, qk_w, False, 256)
        y5_c, y5_l = s5_scan_readout(pc_[0], pl_[0], s5_ops)
        na_l = na_latent(pl_[1], pc_[1], na_bias(na_rpb[li], x.shape[1] // GRID_W))
        hg_c, hg_l = hgrn2_pallas(pc_[2], pl_[2], lower_bounds[li], hg_norm_w[li])
        rt_c, rt_l = retention_pallas(pc_[3], pl_[3], ret_decay_logit[li])

        x, h_l, aff_l = merge_and_route(y5_l, pl_[0], (na_l, hg_l, rt_l), pl_[4], s5_dl, glu_b, wb_b, wo_b, x, g1_l,
                                        nfw, sh2_l, sc2_l, rw_t, 512)
        ex_w = (ex_w_gate[li].astype(BF16), ex_w_up[li].astype(BF16), ex_w_down[li].astype(BF16))
        x = expert_choice_ffn(x, g2_l, h_l, aff_l, *ex_w)
        if not last:
            na_c = na_context(pc_[1])
            xc, h_c, aff_c = merge_and_route(y5_c, pc_[0], (na_c, hg_c, rt_c), pc_[4], s5_dl, glu_b, wb_b, wo_b, xc,
                                             g1_c, nfw, sh2_c, sc2_c, rw_t, 256)
            xc = expert_choice_ffn(xc, g2_c, h_c, aff_c, *ex_w)
    return x
```

```python
import functools
import math

import jax
import jax.numpy as jnp
import numpy as np
from jax import lax
from jax.experimental import pallas as pl
from jax.experimental.pallas import tpu as pltpu

D_MODEL = 1024
DEPTH = 2
GRID_W = 64
N_BRANCH = 4
BRANCH_W = 256
HEAD_DIM = 64
N_HEADS = BRANCH_W // HEAD_DIM
S5_GROUP = 16
S5_GROUPS = BRANCH_W // S5_GROUP
S5_STATE = 64
NA_ROWS = 8
NA_COLS = 16
HG_CHUNK = 16
RET_CHUNK = 128
N_EXPERTS = 16
EC_CAPACITY = 2
D_EXPERT = 2816
ROPE_BASE = 10000.0
EPS = 1e-6
IN_SPLITS = (BRANCH_W, 3 * BRANCH_W, 5 * BRANCH_W, 4 * BRANCH_W, N_BRANCH * D_MODEL)
D_IN = sum(IN_SPLITS)

F32 = jnp.float32
BF16 = jnp.bfloat16

V7X_VMEM_BYTES = 64 * 1024 * 1024
VMEM_LIMIT = V7X_VMEM_BYTES - 8 * 1024 * 1024
EXPERT_VMEM_LIMIT = V7X_VMEM_BYTES - 4 * 1024 * 1024


def _params(*sem):
    return pltpu.CompilerParams(dimension_semantics=sem, vmem_limit_bytes=VMEM_LIMIT)


def _norm_mod(x, norm_w, shift, scale):
    y = x * lax.rsqrt(jnp.mean(x * x, axis=-1, keepdims=True) + EPS) * norm_w
    return y * (1.0 + scale) + shift


def _ada_kernel(c_ref, w_ref, b_ref, o_ref):
    cond = jax.nn.silu(c_ref[...])
    o_ref[0] = jnp.dot(cond, w_ref[0], precision=lax.Precision.HIGHEST, preferred_element_type=F32) + b_ref[0]


def ada_modulation(cond_in, ada_w, ada_b, tn):
    r, d = cond_in.shape
    depth, _, n = ada_w.shape
    return pl.pallas_call(
        _ada_kernel,
        grid=(depth, n // tn),
        in_specs=[
            pl.BlockSpec((r, d), lambda l, j: (0, 0)),
            pl.BlockSpec((1, d, tn), lambda l, j: (l, 0, j)),
            pl.BlockSpec((1, 1, tn), lambda l, j: (l, 0, j)),
        ],
        out_specs=pl.BlockSpec((1, r, tn), lambda l, j: (l, 0, j)),
        out_shape=jax.ShapeDtypeStruct((depth, r, n), F32),
        compiler_params=_params("parallel", "parallel"),
        name="ada_modulation",
    )(cond_in, ada_w, ada_b[:, None, :])


def _head_mean_sq(t, head_avg):
    sq = t * t
    hi = sq.astype(BF16)
    lo = (sq - hi.astype(F32)).astype(BF16)
    return (jnp.dot(hi, head_avg, preferred_element_type=F32)
            + jnp.dot(lo, head_avg, preferred_element_type=F32))


def _in_proj_kernel(x_ref, nw_ref, sh_ref, sc_ref, w_ref, wrot_ref, qkw_ref, havg_ref, cos_ref, sin_ref,
                    s5_ref, na_ref, hg_ref, rt_ref, gate_ref, *, rope):
    hb = _norm_mod(x_ref[0], nw_ref[...], sh_ref[0], sc_ref[0]).astype(BF16)
    cw = BRANCH_W
    proj = lambda col: jnp.dot(hb, w_ref[:, col * cw:(col + 1) * cw], preferred_element_type=F32)
    col = 0
    s5_ref[0] = proj(col).astype(BF16)
    col += 1
    for part in range(3):
        t = proj(col + part)
        if part < 2:
            t = t * lax.rsqrt(_head_mean_sq(t, havg_ref[...]) + EPS) * qkw_ref[part:part + 1, :]
        if part == 0:
            t = t * (HEAD_DIM ** -0.5)
        na_ref[0, :, part * cw:(part + 1) * cw] = t.astype(BF16)
    col += 3
    for part in range(5):
        hg_ref[0, :, part * cw:(part + 1) * cw] = proj(col + part).astype(BF16)
    col += 5
    for part in range(4):
        t = proj(col + part)
        if part < 2 and rope:
            swapped = jnp.dot(hb, wrot_ref[:, part * cw:(part + 1) * cw], preferred_element_type=F32)
            t = t * cos_ref[...] + swapped * sin_ref[...]
        if part == 1:
            t = t * (HEAD_DIM ** -0.5)
        rt_ref[0, :, part * cw:(part + 1) * cw] = t.astype(BF16)
    col += 4
    for part in range(N_BRANCH * D_MODEL // cw):
        gate_ref[0, :, part * cw:(part + 1) * cw] = proj(col + part).astype(BF16)


def rope_tables(n_tokens):
    quarter = HEAD_DIM // 4
    t = jnp.arange(n_tokens, dtype=jnp.int32)
    inv = jnp.asarray(ROPE_BASE ** (-np.arange(quarter) / quarter), F32)
    ang = jnp.concatenate([(t // GRID_W).astype(F32)[:, None] * inv, (t % GRID_W).astype(F32)[:, None] * inv], axis=1)
    cos, sin = jnp.cos(ang), jnp.sin(ang)
    cos_h = jnp.concatenate([cos, cos], axis=1)
    sin_h = jnp.concatenate([-sin, sin], axis=1)
    return jnp.tile(cos_h, (1, N_HEADS)), jnp.tile(sin_h, (1, N_HEADS))


def _swap_head_halves_cols(w):
    d, c = w.shape
    w = w.reshape(d, c // HEAD_DIM, 2, HEAD_DIM // 2)
    return w[:, :, ::-1, :].reshape(d, c)


def head_avg_matrix():
    h = np.arange(BRANCH_W) // HEAD_DIM
    return jnp.asarray((h[:, None] == h[None, :]) / HEAD_DIM, BF16)


def in_proj(x, norm_w, shift, scale, w_in_bf16, w_rot_bf16, qk_norm_w, rope, tm):
    b, n, d = x.shape
    tok = lambda bi, i: (bi, i, 0)
    per_b = lambda bi, i: (bi, 0, 0)
    const2 = lambda bi, i: (0, 0)
    cos, sin = rope_tables(n)
    return pl.pallas_call(
        functools.partial(_in_proj_kernel, rope=rope),
        grid=(b, n // tm),
        in_specs=[
            pl.BlockSpec((1, tm, d), tok),
            pl.BlockSpec((1, d), const2),
            pl.BlockSpec((1, 1, d), per_b),
            pl.BlockSpec((1, 1, d), per_b),
            pl.BlockSpec((d, D_IN), const2, pipeline_mode=pl.Buffered(1)),
            pl.BlockSpec((d, 2 * BRANCH_W), const2, pipeline_mode=pl.Buffered(1)),
            pl.BlockSpec((2, BRANCH_W), const2),
            pl.BlockSpec((BRANCH_W, BRANCH_W), const2),
            pl.BlockSpec((tm, BRANCH_W), lambda bi, i: (i, 0)),
            pl.BlockSpec((tm, BRANCH_W), lambda bi, i: (i, 0)),
        ],
        out_specs=[pl.BlockSpec((1, tm, w), tok) for w in IN_SPLITS],
        out_shape=[jax.ShapeDtypeStruct((b, n, w), BF16) for w in IN_SPLITS],
        compiler_params=_params("parallel", "parallel"),
        name="in_proj",
    )(x, norm_w, shift, scale, w_in_bf16, w_rot_bf16, qk_norm_w, head_avg_matrix(), cos, sin)


NA_QROWS = 8
NA_KROWS = 16
NA_TQ = NA_QROWS * GRID_W
NA_TK = NA_KROWS * GRID_W
NA_MASKED = -1e30


def na_bias(rpb, rows):
    hp = lax.Precision.HIGHEST
    nblk = rows // NA_QROWS
    n_dr, n_dc = 2 * NA_ROWS - 1, 2 * NA_COLS - 1
    qc = np.arange(GRID_W)[:, None]
    kc = np.arange(GRID_W)[None, :]
    cs = np.clip(qc - NA_COLS // 2, 0, GRID_W - NA_COLS)
    col_ok = (kc >= cs) & (kc < cs + NA_COLS)
    dc = np.clip(kc - qc + NA_COLS - 1, 0, n_dc - 1).reshape(-1)
    sel_dc = jnp.asarray(dc[None, :] == np.arange(n_dc)[:, None], F32)
    by_col = jnp.einsum('hrc,cx->hrx', rpb.astype(F32), sel_dc, precision=hp)
    pats = []
    for g in (0, 1, nblk - 1):
        ks = int(np.clip(NA_QROWS * g - NA_ROWS // 2, 0, rows - NA_KROWS))
        qr = (NA_QROWS * g + np.arange(NA_QROWS))[:, None]
        kr = (ks + np.arange(NA_KROWS))[None, :]
        band = np.clip(qr - NA_ROWS // 2, 0, rows - NA_ROWS)
        row_ok = (kr >= band) & (kr < band + NA_ROWS)
        dr = np.clip(kr - qr + NA_ROWS - 1, 0, n_dr - 1).reshape(-1)
        sel_dr = jnp.asarray(dr[:, None] == np.arange(n_dr)[None, :], F32)
        t = jnp.einsum('yr,hrx->hyx', sel_dr, by_col, precision=hp)
        t = t.reshape(N_HEADS, NA_QROWS, NA_KROWS, GRID_W, GRID_W)
        t = jnp.transpose(t, (0, 1, 3, 2, 4)).reshape(N_HEADS, NA_TQ, NA_TK)
        ok = (row_ok[:, None, :, None] & col_ok[None, :, None, :]).reshape(NA_TQ, NA_TK)
        pats.append(jnp.where(jnp.asarray(ok)[None], t, NA_MASKED))
    return jnp.stack(pats)


def _head_lane_mask(h, shape):
    lane = lax.broadcasted_iota(jnp.int32, shape, len(shape) - 1)
    return (lane >= h * HEAD_DIM) & (lane < (h + 1) * HEAD_DIM)


def _na_kernel(q_ref, k_ref, v_ref, kc_ref, vc_ref, bias_ref, o_ref, *, rows):
    g = pl.program_id(1)
    ks = jnp.clip(NA_QROWS * g - NA_ROWS // 2, 0, rows - NA_KROWS)
    start = pl.multiple_of(ks * GRID_W, GRID_W * (NA_ROWS // 2))
    q = q_ref[0]
    k_win = k_ref[0, pl.ds(start, NA_TK), :]
    v_win = v_ref[0, pl.ds(start, NA_TK), :]
    kc, vc = kc_ref[0], vc_ref[0]
    nt = (((1,), (1,)), ((), ()))
    out = jnp.zeros(q.shape, F32)
    for h in range(N_HEADS):
        hm = _head_lane_mask(h, q.shape)
        qh = jnp.where(hm, q, jnp.zeros_like(q))
        s_loc = lax.dot_general(qh, k_win, nt, preferred_element_type=F32) + bias_ref[0, h]
        s_ctx = lax.dot_general(qh, kc, nt, preferred_element_type=F32)
        m = jnp.maximum(jnp.max(s_loc, axis=-1, keepdims=True), jnp.max(s_ctx, axis=-1, keepdims=True))
        p_loc = jnp.exp(s_loc - m)
        p_ctx = jnp.exp(s_ctx - m)
        denom = jnp.sum(p_loc, axis=-1, keepdims=True) + jnp.sum(p_ctx, axis=-1, keepdims=True)
        oh = (jnp.dot(p_loc.astype(BF16), v_win, preferred_element_type=F32)
              + jnp.dot(p_ctx.astype(BF16), vc, preferred_element_type=F32)) / denom
        out = jnp.where(hm, oh, out)
    o_ref[0] = out.astype(o_ref.dtype)


def na_latent(na_lat, na_ctx, bias):
    b, l, _ = na_lat.shape
    lc = na_ctx.shape[1]
    rows = l // GRID_W
    nblk = rows // NA_QROWS
    pat = lambda bi, g: (jnp.where(g == 0, 0, jnp.where(g == nblk - 1, 2, 1)), 0, 0, 0)
    return pl.pallas_call(
        functools.partial(_na_kernel, rows=rows),
        grid=(b, nblk),
        in_specs=[
            pl.BlockSpec((1, NA_TQ, BRANCH_W), lambda bi, g: (bi, g, 0)),
            pl.BlockSpec((1, l, BRANCH_W), lambda bi, g: (bi, 0, 1)),
            pl.BlockSpec((1, l, BRANCH_W), lambda bi, g: (bi, 0, 2)),
            pl.BlockSpec((1, lc, BRANCH_W), lambda bi, g: (bi, 0, 1)),
            pl.BlockSpec((1, lc, BRANCH_W), lambda bi, g: (bi, 0, 2)),
            pl.BlockSpec((1, N_HEADS, NA_TQ, NA_TK), pat),
        ],
        out_specs=pl.BlockSpec((1, NA_TQ, BRANCH_W), lambda bi, g: (bi, g, 0)),
        out_shape=jax.ShapeDtypeStruct((b, l, BRANCH_W), BF16),
        compiler_params=_params("parallel", "arbitrary"),
        name="na_latent",
    )(na_lat, na_lat, na_lat, na_ctx, na_ctx, bias)


def _na_ctx_kernel(q_ref, k_ref, v_ref, o_ref):
    q, k, v = q_ref[0], k_ref[0], v_ref[0]
    nt = (((1,), (1,)), ((), ()))
    out = jnp.zeros(q.shape, F32)
    for h in range(N_HEADS):
        hm = _head_lane_mask(h, q.shape)
        s = lax.dot_general(jnp.where(hm, q, jnp.zeros_like(q)), k, nt, preferred_element_type=F32)
        p = jnp.exp(s - jnp.max(s, axis=-1, keepdims=True))
        oh = jnp.dot(p.astype(BF16), v, preferred_element_type=F32) / jnp.sum(p, axis=-1, keepdims=True)
        out = jnp.where(hm, oh, out)
    o_ref[0] = out.astype(o_ref.dtype)


def na_context(na_ctx):
    b, lc, _ = na_ctx.shape
    spec = lambda col: pl.BlockSpec((1, lc, BRANCH_W), lambda bi: (bi, 0, col))
    return pl.pallas_call(
        _na_ctx_kernel,
        grid=(b,),
        in_specs=[spec(0), spec(1), spec(2)],
        out_specs=pl.BlockSpec((1, lc, BRANCH_W), lambda bi: (bi, 0, 0)),
        out_shape=jax.ShapeDtypeStruct((b, lc, BRANCH_W), BF16),
        compiler_params=_params("parallel"),
        name="na_context",
    )(na_ctx, na_ctx, na_ctx)


def _merge_kernel(y5_ref, u5_ref, o1_ref, o2_ref, o3_ref, gate_ref, d5_ref, wglu_ref, wb_ref, wo_ref, x_ref, g1_ref,
                  nw_ref, sh_ref, sc_ref, rw_ref, xo_ref, h_ref, aff_ref):
    z = jax.nn.gelu(y5_ref[0] + d5_ref[...] * u5_ref[0].astype(F32))
    o_s5 = (z * jax.nn.sigmoid(jnp.dot(z.astype(BF16), wglu_ref[...], preferred_element_type=F32))).astype(BF16)
    m = None
    for br in range(N_BRANCH):
        o_br = o_s5 if br == 0 else (o1_ref, o2_ref, o3_ref)[br - 1][0]
        proj = jnp.dot(o_br, wb_ref[br], preferred_element_type=F32)
        gate = jax.nn.sigmoid(gate_ref[0, :, br * D_MODEL:(br + 1) * D_MODEL].astype(F32))
        m = gate * proj if m is None else m + gate * proj
    mix = jnp.dot(m.astype(BF16), wo_ref[...], preferred_element_type=F32)
    x_new = x_ref[0] + g1_ref[0] * mix
    xo_ref[0] = x_new
    h = _norm_mod(x_new, nw_ref[...], sh_ref[0], sc_ref[0])
    h_ref[0] = h.astype(BF16)
    logits = lax.dot_general(rw_ref[...], h, (((1,), (1,)), ((), ())),
                             precision=lax.Precision.HIGHEST, preferred_element_type=F32)
    logits = logits - jnp.max(logits, axis=0, keepdims=True)
    e = jnp.exp(logits)
    aff_ref[0] = e / jnp.sum(e, axis=0, keepdims=True)


def merge_and_route(y_s5, u_s5, outs, gates, s5_d, s5_glu_bf16, w_branch_bf16, w_out_bf16, x, g1,
                    norm_w, shift, scale, router_w_t, tm):
    b, n, d = x.shape
    tok = lambda bi, i: (bi, i, 0)
    per_b = lambda bi, i: (bi, 0, 0)
    const2 = lambda bi, i: (0, 0)
    const3 = lambda bi, i: (0, 0, 0)
    return pl.pallas_call(
        _merge_kernel,
        grid=(b, n // tm),
        in_specs=[pl.BlockSpec((1, tm, BRANCH_W), tok)] * (N_BRANCH + 1) + [
            pl.BlockSpec((1, tm, N_BRANCH * d), tok),
            pl.BlockSpec((1, BRANCH_W), const2),
            pl.BlockSpec((BRANCH_W, BRANCH_W), const2),
            pl.BlockSpec((N_BRANCH, BRANCH_W, d), const3),
            pl.BlockSpec((d, d), const2),
            pl.BlockSpec((1, tm, d), tok),
            pl.BlockSpec((1, 1, d), per_b),
            pl.BlockSpec((1, d), const2),
            pl.BlockSpec((1, 1, d), per_b),
            pl.BlockSpec((1, 1, d), per_b),
            pl.BlockSpec((N_EXPERTS, d), const2),
        ],
        out_specs=[
            pl.BlockSpec((1, tm, d), tok),
            pl.BlockSpec((1, tm, d), tok),
            pl.BlockSpec((1, N_EXPERTS, tm), lambda bi, i: (bi, 0, i)),
        ],
        out_shape=[
            jax.ShapeDtypeStruct((b, n, d), F32),
            jax.ShapeDtypeStruct((b, n, d), BF16),
            jax.ShapeDtypeStruct((b, N_EXPERTS, n), F32),
        ],
        compiler_params=_params("parallel", "parallel"),
        name="merge_route",
    )(y_s5, u_s5, *outs, gates, s5_d, s5_glu_bf16, w_branch_bf16, w_out_bf16, x, g1, norm_w, shift, scale,
      router_w_t)


MXU_N = 256


EXPERT_F_SPLIT = 2
EXPERT_ROWS = 1024


def _expert_kernel(x_ref, g_ref, g2_ref, wg_ref, wu_ref, wd_ref, y_ref, act_ref, acc_ref):
    fi = pl.program_id(2)
    nb, cap, d = x_ref.shape[1:]
    xb = x_ref[0].reshape(nb * cap, d)
    fh = wg_ref.shape[3]
    for c0 in range(0, fh, MXU_N):
        c1 = min(c0 + MXU_N, fh)
        gate = jnp.dot(xb, wg_ref[0, 0, :, c0:c1].astype(BF16), preferred_element_type=F32)
        up = jnp.dot(xb, wu_ref[0, 0, :, c0:c1].astype(BF16), preferred_element_type=F32)
        act_ref[:, c0:c1] = (jax.nn.silu(gate) * up).astype(BF16)
    part = jnp.dot(act_ref[...], wd_ref[0, 0].astype(BF16), preferred_element_type=F32)

    @pl.when(fi == 0)
    def _():
        acc_ref[...] = part

    @pl.when(fi == pl.num_programs(2) - 1)
    def _():
        y = (acc_ref[...] + part).reshape(nb, cap, d)
        y_ref[0] = (y * g_ref[0] * g2_ref[...]).astype(y_ref.dtype)


def expert_ffn(xe, ge, g2, w_gate, w_up, w_down, layer):
    e, b, cap, d = xe.shape
    f = w_gate.shape[-1]
    nb = max(1, min(b, EXPERT_ROWS // cap))
    fh = f // EXPERT_F_SPLIT
    tok = lambda ei, i, fi: (ei, i, 0, 0)
    return pl.pallas_call(
        _expert_kernel,
        grid=(e, b // nb, EXPERT_F_SPLIT),
        in_specs=[
            pl.BlockSpec((1, nb, cap, d), tok, pipeline_mode=pl.Buffered(1)),
            pl.BlockSpec((1, nb, cap, 1), tok, pipeline_mode=pl.Buffered(1)),
            pl.BlockSpec((nb, 1, d), lambda ei, i, fi: (i, 0, 0)),
            pl.BlockSpec((1, 1, d, fh), lambda ei, i, fi: (layer, ei, 0, fi)),
            pl.BlockSpec((1, 1, d, fh), lambda ei, i, fi: (layer, ei, 0, fi)),
            pl.BlockSpec((1, 1, fh, d), lambda ei, i, fi: (layer, ei, fi, 0)),
        ],
        out_specs=pl.BlockSpec((1, nb, cap, d), tok),
        out_shape=jax.ShapeDtypeStruct((e, b, cap, d), BF16),
        scratch_shapes=[pltpu.VMEM((nb * cap, fh), BF16), pltpu.VMEM((nb * cap, d), F32)],
        compiler_params=pltpu.CompilerParams(dimension_semantics=("parallel", "arbitrary", "arbitrary"),
                                             vmem_limit_bytes=EXPERT_VMEM_LIMIT),
        name="expert_ffn",
    )(xe, ge, g2, w_gate, w_up, w_down)


COMBINE_GROUP = 8


def _combine_kernel(idx_ref, y_ref, x_hbm, o_ref, yf_ref, sem):
    bi = pl.program_id(0)
    e = pl.program_id(1)
    cap = y_ref.shape[2]

    @pl.when(e == 0)
    def _():
        cp = pltpu.make_async_copy(x_hbm.at[bi], o_ref.at[0], sem)
        cp.start()
        cp.wait()

    yf_ref[...] = y_ref[0, 0].astype(F32)

    def body(c, carry):
        base = pl.multiple_of(c * COMBINE_GROUP, COMBINE_GROUP)
        ys = yf_ref[pl.ds(base, COMBINE_GROUP), :]
        rows = [idx_ref[0, 0, e * cap + base + k] for k in range(COMBINE_GROUP)]
        new = [o_ref[0, pl.ds(rows[k], 1), :] + ys[k:k + 1, :] for k in range(COMBINE_GROUP)]
        for k in range(COMBINE_GROUP):
            o_ref[0, pl.ds(rows[k], 1), :] = new[k]
        return carry

    lax.fori_loop(0, cap // COMBINE_GROUP, body, 0)


def moe_combine(x, y, idx):
    b, n, d = x.shape
    e, _, cap, _ = y.shape
    return pl.pallas_call(
        _combine_kernel,
        grid=(b, e),
        in_specs=[
            pl.BlockSpec((1, 1, e * cap), lambda bi, ei: (bi, 0, 0), memory_space=pltpu.SMEM),
            pl.BlockSpec((1, 1, cap, d), lambda bi, ei: (ei, bi, 0, 0)),
            pl.BlockSpec(memory_space=pl.ANY),
        ],
        out_specs=pl.BlockSpec((1, n, d), lambda bi, ei: (bi, 0, 0)),
        out_shape=jax.ShapeDtypeStruct((b, n, d), F32),
        scratch_shapes=[pltpu.VMEM((cap, d), F32), pltpu.SemaphoreType.DMA(())],
        compiler_params=_params("parallel", "arbitrary"),
        name="moe_combine",
    )(idx, y, x)


def expert_choice_ffn(x, g2, h_bf16, aff_t, w_gate, w_up, w_down, layer):
    b, n, d = h_bf16.shape
    cap = EC_CAPACITY * n // N_EXPERTS
    g, idx = lax.top_k(aff_t, cap)
    idx_e = jnp.moveaxis(idx, 1, 0)
    g_e = jnp.moveaxis(g, 1, 0)[..., None]
    xe = h_bf16[jnp.arange(b)[None, :, None], idx_e]
    y = expert_ffn(xe, g_e, g2, w_gate, w_up, w_down, layer)
    return moe_combine(x, y, idx.reshape(b, 1, N_EXPERTS * cap).astype(jnp.int32))


S5_BLK = 16
S5_ROW = S5_GROUPS * S5_BLK * S5_GROUP
S5_PAIRS = S5_GROUPS // 2
S5_LANES = S5_GROUPS * S5_STATE


def _pair_blockdiag(t):
    g, r, c = t.shape
    t = t.reshape(g // 2, 2, r, c)
    z = jnp.zeros_like(t[:, 0])
    top = jnp.concatenate([t[:, 0], z], axis=-1)
    bot = jnp.concatenate([z, t[:, 1]], axis=-1)
    return jnp.concatenate([top, bot], axis=-2)


def s5_operators(lam_re, lam_im, log_step, b_re, b_im, c_re, c_im):
    hp = lax.Precision.HIGHEST
    blk = S5_BLK
    dt = jnp.exp(log_step)[..., None]
    k = jnp.arange(blk + 1, dtype=F32)
    mag = jnp.exp((lam_re * dt)[..., None] * k)
    ang = (lam_im * dt)[..., None] * k
    pr, pi = mag * jnp.cos(ang), mag * jnp.sin(ang)
    ar, ai = pr[..., 1], pi[..., 1]
    den = lam_re * lam_re + lam_im * lam_im
    zr = ((ar - 1.0) * lam_re + ai * lam_im) / den
    zi = (ai * lam_re - (ar - 1.0) * lam_im) / den
    bb_re = zr[..., None] * b_re - zi[..., None] * b_im
    bb_im = zr[..., None] * b_im + zi[..., None] * b_re
    ca_re = c_re[..., None] * pr[:, :, None] - c_im[..., None] * pi[:, :, None]
    ca_im = c_re[..., None] * pi[:, :, None] + c_im[..., None] * pr[:, :, None]
    kern = (jnp.einsum('dgpnl,dgnq->dglpq', ca_re, bb_re, precision=hp)
            - jnp.einsum('dgpnl,dgnq->dglpq', ca_im, bb_im, precision=hp))
    j = np.arange(blk)[:, None]
    i = np.arange(blk)[None, :]
    ms, ws, rres, rims = [], [], [], []
    for d in range(2):
        lag = (i - j) if d == 0 else (j - i)
        valid = jnp.asarray(lag >= 0, F32)[None, :, None, :, None]
        kd = kern[d][:, np.clip(lag, 0, blk - 1)]
        m = jnp.transpose(kd, (0, 1, 4, 2, 3)) * valid
        ms.append(m.reshape(S5_GROUPS, blk * S5_GROUP, blk * S5_GROUP))
        pw = (blk - 1 - np.arange(blk)) if d == 0 else np.arange(blk)
        apr, api = pr[d][..., pw], pi[d][..., pw]
        w_re = apr[..., None] * bb_re[d][:, :, None] - api[..., None] * bb_im[d][:, :, None]
        w_im = apr[..., None] * bb_im[d][:, :, None] + api[..., None] * bb_re[d][:, :, None]
        to_w = lambda t: jnp.transpose(t, (0, 2, 3, 1)).reshape(S5_GROUPS, blk * S5_GROUP, S5_STATE)
        ws.append((_pair_blockdiag(to_w(w_re)), _pair_blockdiag(to_w(w_im))))
        ex = (np.arange(blk) + 1) if d == 0 else (blk - np.arange(blk))
        r_re = ca_re[d][..., ex]
        r_im = -ca_im[d][..., ex]
        to_r = lambda t: jnp.transpose(t, (0, 2, 3, 1)).reshape(S5_GROUPS, S5_STATE, blk * S5_GROUP)
        rres.append(_pair_blockdiag(to_r(r_re)))
        rims.append(_pair_blockdiag(to_r(r_im)))
    m_op = jnp.stack(ms).astype(BF16)
    w_re = jnp.stack([w[0] for w in ws]).astype(BF16)
    w_im = jnp.stack([w[1] for w in ws]).astype(BF16)
    r_re = jnp.stack(rres).astype(BF16)
    r_im = jnp.stack(rims).astype(BF16)
    a_blk = jnp.stack([pr[..., blk].reshape(2, 1, S5_LANES), pi[..., blk].reshape(2, 1, S5_LANES)], axis=1)
    return m_op, w_re, w_im, r_re, r_im, a_blk


def _s5_kernel(uc_ref, ul_ref, m_ref, wre_ref, wim_ref, rre_ref, rim_ref, a_ref, yc_ref, yl_ref,
               vre, vim, sre, sim):
    rc, rl = uc_ref.shape[1], ul_ref.shape[1]
    segs = ((uc_ref, yc_ref, 0, rc), (ul_ref, yl_ref, rc, rl))
    gw = S5_BLK * S5_GROUP
    for d in range(2):
        for u_ref, _, base, rows in segs:
            for h in range(S5_PAIRS):
                u_pair = u_ref[0, :, 2 * h * gw:2 * (h + 1) * gw]
                vre[base:base + rows, h * 128:(h + 1) * 128] = jnp.dot(
                    u_pair, wre_ref[d, h], preferred_element_type=F32)
                vim[base:base + rows, h * 128:(h + 1) * 128] = jnp.dot(
                    u_pair, wim_ref[d, h], preferred_element_type=F32)
        ar, ai = a_ref[d, 0], a_ref[d, 1]

        def run(base, rows, carry):
            def step(t, c):
                xr, xi = c
                idx = base + (t if d == 0 else rows - 1 - t)
                sre[pl.ds(idx, 1), :] = xr
                sim[pl.ds(idx, 1), :] = xi
                nr = ar * xr - ai * xi + vre[pl.ds(idx, 1), :]
                ni = ar * xi + ai * xr + vim[pl.ds(idx, 1), :]
                return nr, ni
            return lax.fori_loop(0, rows, step, carry)

        zero = jnp.zeros((1, S5_LANES), F32)
        carry = run(0, rc, (zero, zero))
        run(rc, rl, carry)
        for u_ref, y_ref, base, rows in segs:
            for h in range(S5_PAIRS):
                s_r = sre[base:base + rows, h * 128:(h + 1) * 128].astype(BF16)
                s_i = sim[base:base + rows, h * 128:(h + 1) * 128].astype(BF16)
                y = (jnp.dot(s_r, rre_ref[d, h], preferred_element_type=F32)
                     + jnp.dot(s_i, rim_ref[d, h], preferred_element_type=F32))
                for gl in range(2):
                    g = 2 * h + gl
                    yg = y[:, gl * gw:(gl + 1) * gw] + jnp.dot(
                        u_ref[0, :, g * gw:(g + 1) * gw], m_ref[d, g], preferred_element_type=F32)
                    if d == 0:
                        y_ref[0, :, g * gw:(g + 1) * gw] = yg
                    else:
                        y_ref[0, :, g * gw:(g + 1) * gw] += yg


def _to_s5_rows(u):
    b, n, _ = u.shape
    u = u.reshape(b, n // S5_BLK, S5_BLK, S5_GROUPS, S5_GROUP)
    return jnp.transpose(u, (0, 1, 3, 2, 4)).reshape(b, n // S5_BLK, S5_ROW)


def _from_s5_rows(y):
    b, r, _ = y.shape
    y = y.reshape(b, r, S5_GROUPS, S5_BLK, S5_GROUP)
    return jnp.transpose(y, (0, 1, 3, 2, 4)).reshape(b, r * S5_BLK, BRANCH_W)


def s5_scan_readout(u_ctx, u_lat, ops):
    m_op, w_re, w_im, r_re, r_im, a_blk = ops
    b = u_lat.shape[0]
    uc, ul = _to_s5_rows(u_ctx), _to_s5_rows(u_lat)
    rc, rl = uc.shape[1], ul.shape[1]
    per_b = lambda bi: (bi, 0, 0)
    c4 = lambda bi: (0, 0, 0, 0)
    yc, yl = pl.pallas_call(
        _s5_kernel,
        grid=(b,),
        in_specs=[
            pl.BlockSpec((1, rc, S5_ROW), per_b),
            pl.BlockSpec((1, rl, S5_ROW), per_b),
            pl.BlockSpec(m_op.shape, c4),
            pl.BlockSpec(w_re.shape, c4),
            pl.BlockSpec(w_im.shape, c4),
            pl.BlockSpec(r_re.shape, c4),
            pl.BlockSpec(r_im.shape, c4),
            pl.BlockSpec(a_blk.shape, c4),
        ],
        out_specs=[pl.BlockSpec((1, rc, S5_ROW), per_b), pl.BlockSpec((1, rl, S5_ROW), per_b)],
        out_shape=[jax.ShapeDtypeStruct((b, rc, S5_ROW), F32), jax.ShapeDtypeStruct((b, rl, S5_ROW), F32)],
        scratch_shapes=[pltpu.VMEM((rc + rl, S5_LANES), F32) for _ in range(4)],
        compiler_params=_params("parallel"),
        name="s5_scan",
    )(uc, ul, m_op, w_re, w_im, r_re, r_im, a_blk)
    return _from_s5_rows(yc), _from_s5_rows(yl)


LA_TILE = 128


def _stack_heads(q):
    return jnp.concatenate(
        [jnp.where(_head_lane_mask(h, q.shape), q, jnp.zeros_like(q)) for h in range(N_HEADS)], axis=0)


def _unstack_heads(o_stack, t):
    out = jnp.zeros((t, o_stack.shape[1]), o_stack.dtype)
    for h in range(N_HEADS):
        blk = o_stack[h * t:(h + 1) * t]
        out = jnp.where(_head_lane_mask(h, blk.shape), blk, out)
    return out


def _same_head_block(shape):
    r = lax.broadcasted_iota(jnp.int32, shape, 0) // HEAD_DIM
    c = lax.broadcasted_iota(jnp.int32, shape, 1) // HEAD_DIM
    return r == c


_NT = (((1,), (1,)), ((), ()))


def _state_update(st_ref, decay_lane, v, k_scaled):
    vt = jnp.transpose(v.astype(F32)).astype(BF16)
    kv = jnp.dot(vt, k_scaled.astype(BF16), preferred_element_type=F32)
    st_ref[...] = decay_lane * st_ref[...] + jnp.where(_same_head_block(kv.shape), kv, 0.0)


def _for_tiles(n_tiles, reverse, body):
    def step(i, carry):
        body(n_tiles - 1 - i if reverse else i)
        return carry
    lax.fori_loop(0, n_tiles, step, 0)


def _ret_kernel(qc_ref, kc_ref, vc_ref, gc_ref, ql_ref, kl_ref, vl_ref, gl_ref, dmask_ref, xi_ref, zeta_ref,
                gam_ref, havg_ref, oc_ref, ol_ref, st_ref, accc_ref, accl_ref):
    t = LA_TILE
    segs = ((qc_ref, kc_ref, vc_ref, accc_ref), (ql_ref, kl_ref, vl_ref, accl_ref))
    for d in range(2):
        st_ref[...] = jnp.zeros_like(st_ref)
        for q_ref, k_ref, v_ref, acc_ref in segs:
            def tile(i, q_ref=q_ref, k_ref=k_ref, v_ref=v_ref, acc_ref=acc_ref):
                rows = pl.ds(pl.multiple_of(i * t, t), t)
                q, k, v = q_ref[0, rows, :], k_ref[0, rows, :], v_ref[0, rows, :]
                att = lax.dot_general(_stack_heads(q), k, _NT, preferred_element_type=F32) * dmask_ref[d]
                intra = _unstack_heads(jnp.dot(att.astype(BF16), v, preferred_element_type=F32), t)
                cross = lax.dot_general(q, st_ref[...].astype(BF16), _NT, preferred_element_type=F32)
                o = intra + cross * xi_ref[d]
                if d == 0:
                    acc_ref[rows, :] = o
                else:
                    acc_ref[rows, :] += o
                _state_update(st_ref, gam_ref[d], v, k.astype(F32) * zeta_ref[d])
            _for_tiles(q_ref.shape[1] // t, d == 1, tile)
    for acc_ref, g_ref, o_ref in ((accc_ref, gc_ref, oc_ref), (accl_ref, gl_ref, ol_ref)):
        o = acc_ref[...]
        o = o * lax.rsqrt(_head_mean_sq(o, havg_ref[...]) + EPS)
        o_ref[0] = (o * jax.nn.silu(g_ref[0].astype(F32))).astype(o_ref.dtype)


def retention_tables(decay_logit):
    t = LA_TILE
    lg = jax.nn.log_sigmoid(decay_logit)
    idx = jnp.arange(t, dtype=F32)
    diff = idx[:, None] - idx[None, :]
    lgm = lg[:, :, None, None]
    fwd = jnp.where(diff >= 0, jnp.exp(jnp.maximum(diff, 0.0) * lgm[0]), 0.0)
    bwd = jnp.where(diff <= 0, jnp.exp(jnp.maximum(-diff, 0.0) * lgm[1]), 0.0)
    dmask = jnp.stack([fwd.reshape(N_HEADS * t, t), bwd.reshape(N_HEADS * t, t)])
    lane = lambda a: jnp.repeat(a, HEAD_DIM, axis=-1)
    lg_l = lane(lg)[:, None, :]
    steps_q = jnp.stack([idx + 1.0, t - idx])[:, :, None]
    steps_k = jnp.stack([t - 1.0 - idx, idx])[:, :, None]
    xi = jnp.exp(steps_q * lg_l)
    zeta = jnp.exp(steps_k * lg_l)
    gam = jnp.exp(t * lg_l)
    return dmask, xi, zeta, gam


def retention_pallas(rt_ctx, rt_lat, decay_logit):
    b, l, _ = rt_lat.shape
    lc = rt_ctx.shape[1]
    dmask, xi, zeta, gam = retention_tables(decay_logit)
    col = lambda n, c: pl.BlockSpec((1, n, BRANCH_W), lambda bi: (bi, 0, c))
    const = lambda a: pl.BlockSpec(a.shape, lambda bi: (0,) * a.ndim)
    havg = head_avg_matrix()
    return pl.pallas_call(
        _ret_kernel,
        grid=(b,),
        in_specs=[col(lc, c) for c in range(4)] + [col(l, c) for c in range(4)]
        + [const(dmask), const(xi), const(zeta), const(gam), const(havg)],
        out_specs=[col(lc, 0), col(l, 0)],
        out_shape=[jax.ShapeDtypeStruct((b, lc, BRANCH_W), BF16), jax.ShapeDtypeStruct((b, l, BRANCH_W), BF16)],
        scratch_shapes=[pltpu.VMEM((BRANCH_W, BRANCH_W), F32), pltpu.VMEM((lc, BRANCH_W), F32),
                        pltpu.VMEM((l, BRANCH_W), F32)],
        compiler_params=_params("parallel"),
        name="retention",
    )(rt_ctx, rt_ctx, rt_ctx, rt_ctx, rt_lat, rt_lat, rt_lat, rt_lat, dmask, xi, zeta, gam, havg)


HG_LEVELS = (32, 64, 128)


def _hg_level_masks(d):
    t = LA_TILE
    i = lax.broadcasted_iota(jnp.int32, (N_HEADS * t, t), 0) % t
    j = lax.broadcasted_iota(jnp.int32, (N_HEADS * t, t), 1)
    base = (i // HG_CHUNK == j // HG_CHUNK) & ((j <= i) if d == 0 else (j >= i))
    masks = [base]
    for blk in HG_LEVELS:
        qi_late = (i % blk) >= blk // 2
        kj_late = (j % blk) >= blk // 2
        cross = (qi_late & ~kj_late) if d == 0 else (~qi_late & kj_late)
        masks.append((i // blk == j // blk) & cross)
    return masks


def _hg_tile(d, q, k, v, logf, st_ref, tri):
    t = LA_TILE
    w = q.shape[1]
    g = jnp.dot(tri, logf, precision=lax.Precision.HIGHEST, preferred_element_type=F32)
    nb = t // HG_CHUNK
    g3 = g.reshape(nb, HG_CHUNK, w)
    if d == 0:
        edge = g3[:, HG_CHUNK - 1:HG_CHUNK, :]
        prev = jnp.concatenate([jnp.zeros((1, 1, w), F32), edge[:-1]], axis=0)
    else:
        edge = g3[:, 0:1, :]
        prev = jnp.concatenate([edge[1:], jnp.zeros((1, 1, w), F32)], axis=0)
    cum = (g3 - prev).reshape(t, w)
    masks = _hg_level_masks(d)
    qs = [q * jnp.exp(cum)]
    ks = [k * jnp.exp(-cum)]
    for blk in HG_LEVELS:
        gb = g.reshape(t // blk, blk, w)
        row = blk // 2 - 1 if d == 0 else blk // 2
        mid = jnp.broadcast_to(gb[:, row:row + 1, :], gb.shape).reshape(t, w)
        qs.append(q * jnp.exp(jnp.minimum(g - mid, 0.0)))
        ks.append(k * jnp.exp(jnp.minimum(mid - g, 0.0)))
    att = None
    for qq, kk, m in zip(qs, ks, masks):
        a = lax.dot_general(_stack_heads(qq.astype(BF16)), kk.astype(BF16), _NT, preferred_element_type=F32)
        a = jnp.where(m, a, 0.0)
        att = a if att is None else att + a
    intra = _unstack_heads(jnp.dot(att.astype(BF16), v, preferred_element_type=F32), t)
    cross = lax.dot_general((q * jnp.exp(g)).astype(BF16), st_ref[...].astype(BF16), _NT,
                            preferred_element_type=F32)
    total = g[t - 1:t, :] if d == 0 else g[0:1, :]
    return intra + cross, jnp.exp(total), k * jnp.exp(total - g)


def _hg_kernel(pc_ref, pl_ref, lb_ref, nw_ref, havg_ref, oc_ref, ol_ref, st_ref, accc_ref, accl_ref):
    t = LA_TILE
    w = BRANCH_W
    lb = lb_ref[...]
    r = lax.broadcasted_iota(jnp.int32, (t, t), 0)
    c = lax.broadcasted_iota(jnp.int32, (t, t), 1)
    for d in range(2):
        tri = jnp.where((c <= r) if d == 0 else (c >= r), 1.0, 0.0).astype(F32)
        st_ref[...] = jnp.zeros_like(st_ref)
        for p_ref, acc_ref in ((pc_ref, accc_ref), (pl_ref, accl_ref)):
            def tile(i, p_ref=p_ref, acc_ref=acc_ref):
                rows = pl.ds(pl.multiple_of(i * t, t), t)
                q = jax.nn.silu(p_ref[0, rows, 0:w].astype(F32))
                f_logit = p_ref[0, rows, (1 + d) * w:(2 + d) * w].astype(F32)
                v = p_ref[0, rows, 3 * w:4 * w]
                fg = lb + (1.0 - lb) * jax.nn.sigmoid(f_logit)
                o, decay, k_end = _hg_tile(d, q, 1.0 - fg, v, jnp.log(fg), st_ref, tri)
                if d == 0:
                    acc_ref[rows, :] = o
                else:
                    acc_ref[rows, :] += o
                _state_update(st_ref, decay, v, k_end)
            _for_tiles(p_ref.shape[1] // t, d == 1, tile)
    for acc_ref, p_ref, o_ref in ((accc_ref, pc_ref, oc_ref), (accl_ref, pl_ref, ol_ref)):
        o = acc_ref[...]
        o = o * lax.rsqrt(_head_mean_sq(o, havg_ref[...]) + EPS) * nw_ref[...]
        o_ref[0] = (o * jax.nn.silu(p_ref[0, :, 4 * w:5 * w].astype(F32))).astype(o_ref.dtype)


def hgrn2_pallas(hg_ctx, hg_lat, lower_bound, norm_w):
    b, l, width = hg_lat.shape
    lc = hg_ctx.shape[1]
    full = lambda n: pl.BlockSpec((1, n, width), lambda bi: (bi, 0, 0))
    out = lambda n: pl.BlockSpec((1, n, BRANCH_W), lambda bi: (bi, 0, 0))
    vec = pl.BlockSpec((1, BRANCH_W), lambda bi: (0, 0))
    havg = head_avg_matrix()
    return pl.pallas_call(
        _hg_kernel,
        grid=(b,),
        in_specs=[full(lc), full(l), vec, vec, pl.BlockSpec(havg.shape, lambda bi: (0, 0))],
        out_specs=[out(lc), out(l)],
        out_shape=[jax.ShapeDtypeStruct((b, lc, BRANCH_W), BF16), jax.ShapeDtypeStruct((b, l, BRANCH_W), BF16)],
        scratch_shapes=[pltpu.VMEM((BRANCH_W, BRANCH_W), F32), pltpu.VMEM((lc, BRANCH_W), F32),
                        pltpu.VMEM((l, BRANCH_W), F32)],
        compiler_params=_params("parallel"),
        name="hgrn2",
    )(hg_ctx, hg_lat, lower_bound[None], jnp.tile(norm_w, N_HEADS)[None], havg)


def kernel(x, c, ctx, c_ctx, ada_w, ada_b, norm_mix_w, norm_ffn_w, w_in, s5_lam_re, s5_lam_im, s5_log_step,
           s5_b_re, s5_b_im, s5_c_re, s5_c_im, s5_d, s5_glu_w, na_q_norm, na_k_norm, na_rpb, hg_lower_bounds,
           hg_norm_w, ret_decay_logit, w_branch, w_out, router_w, ex_w_gate, ex_w_up, ex_w_down):
    b = x.shape[0]
    lb_p = jax.nn.softmax(hg_lower_bounds, axis=0)
    lower_bounds = jnp.cumsum(lb_p, axis=0) - lb_p[0]
    cond_rows = jnp.concatenate([c, jnp.broadcast_to(c_ctx[None], c.shape)], axis=0)
    mods = ada_modulation(cond_rows, ada_w, ada_b, D_MODEL)
    xc = ctx
    for li in range(DEPTH):
        last = li == DEPTH - 1
        mod_l = [m[:, None] for m in jnp.split(mods[li, :b], 6, axis=-1)]
        mod_c = [m[:, None] for m in jnp.split(mods[li, b:], 6, axis=-1)]
        sh1_l, sc1_l, g1_l, sh2_l, sc2_l, g2_l = mod_l
        sh1_c, sc1_c, g1_c, sh2_c, sc2_c, g2_c = mod_c
        w_in_b = w_in[li].astype(BF16)
        wb_b = w_branch[li].astype(BF16)
        wo_b = w_out[li].astype(BF16)
        rw_t = router_w[li].T
        nmw = norm_mix_w[li][None]
        nfw = norm_ffn_w[li][None]

        rt_off = sum(IN_SPLITS[:3])
        w_rot_b = jnp.concatenate(
            [_swap_head_halves_cols(w_in[li][:, rt_off + i * BRANCH_W:rt_off + (i + 1) * BRANCH_W]) for i in range(2)],
            axis=1).astype(BF16)
        qk_w = jnp.stack([jnp.tile(na_q_norm[li], N_HEADS), jnp.tile(na_k_norm[li], N_HEADS)])
        s5_ops = s5_operators(s5_lam_re[li], s5_lam_im[li], s5_log_step[li], s5_b_re[li], s5_b_im[li],
                              s5_c_re[li], s5_c_im[li])
        s5_dl = s5_d[li][None]
        glu_b = s5_glu_w[li].astype(BF16)

        pl_ = in_proj(x, nmw, sh1_l, sc1_l, w_in_b, w_rot_b, qk_w, True, 512)
        pc_ = in_proj(xc, nmw, sh1_c, sc1_c, w_in_b, w_rot_b, qk_w, False, 256)
        y5_c, y5_l = s5_scan_readout(pc_[0], pl_[0], s5_ops)
        na_l = na_latent(pl_[1], pc_[1], na_bias(na_rpb[li], x.shape[1] // GRID_W))
        hg_c, hg_l = hgrn2_pallas(pc_[2], pl_[2], lower_bounds[li], hg_norm_w[li])
        rt_c, rt_l = retention_pallas(pc_[3], pl_[3], ret_decay_logit[li])

        x, h_l, aff_l = merge_and_route(y5_l, pl_[0], (na_l, hg_l, rt_l), pl_[4], s5_dl, glu_b, wb_b, wo_b, x, g1_l,
                                        nfw, sh2_l, sc2_l, rw_t, 512)
        ex_w = (ex_w_gate, ex_w_up, ex_w_down, li)
        x = expert_choice_ffn(x, g2_l, h_l, aff_l, *ex_w)
        if not last:
            na_c = na_context(pc_[1])
            xc, h_c, aff_c = merge_and_route(y5_c, pc_[0], (na_c, hg_c, rt_c), pc_[4], s5_dl, glu_b, wb_b, wo_b, xc,
                                             g1_c, nfw, sh2_c, sc2_c, rw_t, 256)
            xc = expert_choice_ffn(xc, g2_c, h_c, aff_c, *ex_w)
    return x
```

```python
import functools
import math

import jax
import jax.numpy as jnp
import numpy as np
from jax import lax
from jax.experimental import pallas as pl
from jax.experimental.pallas import tpu as pltpu

D_MODEL = 1024
DEPTH = 2
GRID_W = 64
N_BRANCH = 4
BRANCH_W = 256
HEAD_DIM = 64
N_HEADS = BRANCH_W // HEAD_DIM
S5_GROUP = 16
S5_GROUPS = BRANCH_W // S5_GROUP
S5_STATE = 64
NA_ROWS = 8
NA_COLS = 16
HG_CHUNK = 16
RET_CHUNK = 128
N_EXPERTS = 16
EC_CAPACITY = 2
D_EXPERT = 2816
ROPE_BASE = 10000.0
EPS = 1e-6
IN_SPLITS = (BRANCH_W, 3 * BRANCH_W, 5 * BRANCH_W, 4 * BRANCH_W, N_BRANCH * D_MODEL)
D_IN = sum(IN_SPLITS)

F32 = jnp.float32
BF16 = jnp.bfloat16

V7X_VMEM_BYTES = 64 * 1024 * 1024
VMEM_LIMIT = V7X_VMEM_BYTES - 8 * 1024 * 1024


def _params(*sem):
    return pltpu.CompilerParams(dimension_semantics=sem, vmem_limit_bytes=VMEM_LIMIT)


def _norm_mod(x, norm_w, shift, scale):
    y = x * lax.rsqrt(jnp.mean(x * x, axis=-1, keepdims=True) + EPS) * norm_w
    return y * (1.0 + scale) + shift


def _ada_kernel(c_ref, w_ref, b_ref, o_ref):
    cond = jax.nn.silu(c_ref[...])
    o_ref[0] = jnp.dot(cond, w_ref[0], precision=lax.Precision.HIGHEST, preferred_element_type=F32) + b_ref[0]


def ada_modulation(cond_in, ada_w, ada_b, tn):
    r, d = cond_in.shape
    depth, _, n = ada_w.shape
    return pl.pallas_call(
        _ada_kernel,
        grid=(depth, n // tn),
        in_specs=[
            pl.BlockSpec((r, d), lambda l, j: (0, 0)),
            pl.BlockSpec((1, d, tn), lambda l, j: (l, 0, j)),
            pl.BlockSpec((1, 1, tn), lambda l, j: (l, 0, j)),
        ],
        out_specs=pl.BlockSpec((1, r, tn), lambda l, j: (l, 0, j)),
        out_shape=jax.ShapeDtypeStruct((depth, r, n), F32),
        compiler_params=_params("parallel", "parallel"),
        name="ada_modulation",
    )(cond_in, ada_w, ada_b[:, None, :])


def _head_mean_sq(t, head_avg):
    sq = t * t
    hi = sq.astype(BF16)
    lo = (sq - hi.astype(F32)).astype(BF16)
    return (jnp.dot(hi, head_avg, preferred_element_type=F32)
            + jnp.dot(lo, head_avg, preferred_element_type=F32))


def _in_proj_kernel(x_ref, nw_ref, sh_ref, sc_ref, w_ref, wrot_ref, qkw_ref, havg_ref, cos_ref, sin_ref,
                    s5_ref, na_ref, hg_ref, rt_ref, gate_ref, *, rope):
    hb = _norm_mod(x_ref[0], nw_ref[...], sh_ref[0], sc_ref[0]).astype(BF16)
    cw = BRANCH_W
    proj = lambda col: jnp.dot(hb, w_ref[:, col * cw:(col + 1) * cw], preferred_element_type=F32)
    col = 0
    s5_ref[0] = proj(col).astype(BF16)
    col += 1
    for part in range(3):
        t = proj(col + part)
        if part < 2:
            t = t * lax.rsqrt(_head_mean_sq(t, havg_ref[...]) + EPS) * qkw_ref[part:part + 1, :]
        if part == 0:
            t = t * (HEAD_DIM ** -0.5)
        na_ref[0, :, part * cw:(part + 1) * cw] = t.astype(BF16)
    col += 3
    for part in range(5):
        hg_ref[0, :, part * cw:(part + 1) * cw] = proj(col + part).astype(BF16)
    col += 5
    for part in range(4):
        t = proj(col + part)
        if part < 2 and rope:
            swapped = jnp.dot(hb, wrot_ref[:, part * cw:(part + 1) * cw], preferred_element_type=F32)
            t = t * cos_ref[...] + swapped * sin_ref[...]
        if part == 1:
            t = t * (HEAD_DIM ** -0.5)
        rt_ref[0, :, part * cw:(part + 1) * cw] = t.astype(BF16)
    col += 4
    for part in range(N_BRANCH * D_MODEL // cw):
        gate_ref[0, :, part * cw:(part + 1) * cw] = proj(col + part).astype(BF16)


def rope_tables(n_tokens):
    quarter = HEAD_DIM // 4
    t = jnp.arange(n_tokens, dtype=jnp.int32)
    inv = jnp.asarray(ROPE_BASE ** (-np.arange(quarter) / quarter), F32)
    ang = jnp.concatenate([(t // GRID_W).astype(F32)[:, None] * inv, (t % GRID_W).astype(F32)[:, None] * inv], axis=1)
    cos, sin = jnp.cos(ang), jnp.sin(ang)
    cos_h = jnp.concatenate([cos, cos], axis=1)
    sin_h = jnp.concatenate([-sin, sin], axis=1)
    return jnp.tile(cos_h, (1, N_HEADS)), jnp.tile(sin_h, (1, N_HEADS))


def _swap_head_halves_cols(w):
    d, c = w.shape
    w = w.reshape(d, c // HEAD_DIM, 2, HEAD_DIM // 2)
    return w[:, :, ::-1, :].reshape(d, c)


def head_avg_matrix():
    h = np.arange(BRANCH_W) // HEAD_DIM
    return jnp.asarray((h[:, None] == h[None, :]) / HEAD_DIM, BF16)


def in_proj(x, norm_w, shift, scale, w_in_bf16, w_rot_bf16, qk_norm_w, rope, tm):
    b, n, d = x.shape
    tok = lambda bi, i: (bi, i, 0)
    per_b = lambda bi, i: (bi, 0, 0)
    const2 = lambda bi, i: (0, 0)
    cos, sin = rope_tables(n)
    return pl.pallas_call(
        functools.partial(_in_proj_kernel, rope=rope),
        grid=(b, n // tm),
        in_specs=[
            pl.BlockSpec((1, tm, d), tok),
            pl.BlockSpec((1, d), const2),
            pl.BlockSpec((1, 1, d), per_b),
            pl.BlockSpec((1, 1, d), per_b),
            pl.BlockSpec((d, D_IN), const2, pipeline_mode=pl.Buffered(1)),
            pl.BlockSpec((d, 2 * BRANCH_W), const2, pipeline_mode=pl.Buffered(1)),
            pl.BlockSpec((2, BRANCH_W), const2),
            pl.BlockSpec((BRANCH_W, BRANCH_W), const2),
            pl.BlockSpec((tm, BRANCH_W), lambda bi, i: (i, 0)),
            pl.BlockSpec((tm, BRANCH_W), lambda bi, i: (i, 0)),
        ],
        out_specs=[pl.BlockSpec((1, tm, w), tok) for w in IN_SPLITS],
        out_shape=[jax.ShapeDtypeStruct((b, n, w), BF16) for w in IN_SPLITS],
        compiler_params=_params("parallel", "parallel"),
        name="in_proj",
    )(x, norm_w, shift, scale, w_in_bf16, w_rot_bf16, qk_norm_w, head_avg_matrix(), cos, sin)


NA_QROWS = 8
NA_KROWS = 16
NA_TQ = NA_QROWS * GRID_W
NA_TK = NA_KROWS * GRID_W
NA_MASKED = -1e30


def na_bias(rpb, rows):
    hp = lax.Precision.HIGHEST
    nblk = rows // NA_QROWS
    n_dr, n_dc = 2 * NA_ROWS - 1, 2 * NA_COLS - 1
    qc = np.arange(GRID_W)[:, None]
    kc = np.arange(GRID_W)[None, :]
    cs = np.clip(qc - NA_COLS // 2, 0, GRID_W - NA_COLS)
    col_ok = (kc >= cs) & (kc < cs + NA_COLS)
    dc = np.clip(kc - qc + NA_COLS - 1, 0, n_dc - 1).reshape(-1)
    sel_dc = jnp.asarray(dc[None, :] == np.arange(n_dc)[:, None], F32)
    by_col = jnp.einsum('hrc,cx->hrx', rpb.astype(F32), sel_dc, precision=hp)
    pats = []
    for g in (0, 1, nblk - 1):
        ks = int(np.clip(NA_QROWS * g - NA_ROWS // 2, 0, rows - NA_KROWS))
        qr = (NA_QROWS * g + np.arange(NA_QROWS))[:, None]
        kr = (ks + np.arange(NA_KROWS))[None, :]
        band = np.clip(qr - NA_ROWS // 2, 0, rows - NA_ROWS)
        row_ok = (kr >= band) & (kr < band + NA_ROWS)
        dr = np.clip(kr - qr + NA_ROWS - 1, 0, n_dr - 1).reshape(-1)
        sel_dr = jnp.asarray(dr[:, None] == np.arange(n_dr)[None, :], F32)
        t = jnp.einsum('yr,hrx->hyx', sel_dr, by_col, precision=hp)
        t = t.reshape(N_HEADS, NA_QROWS, NA_KROWS, GRID_W, GRID_W)
        t = jnp.transpose(t, (0, 1, 3, 2, 4)).reshape(N_HEADS, NA_TQ, NA_TK)
        ok = (row_ok[:, None, :, None] & col_ok[None, :, None, :]).reshape(NA_TQ, NA_TK)
        pats.append(jnp.where(jnp.asarray(ok)[None], t, NA_MASKED))
    return jnp.stack(pats)


def _head_lane_mask(h, shape):
    lane = lax.broadcasted_iota(jnp.int32, shape, len(shape) - 1)
    return (lane >= h * HEAD_DIM) & (lane < (h + 1) * HEAD_DIM)


def _na_kernel(q_ref, k_ref, v_ref, kc_ref, vc_ref, bias_ref, o_ref, *, rows):
    g = pl.program_id(1)
    ks = jnp.clip(NA_QROWS * g - NA_ROWS // 2, 0, rows - NA_KROWS)
    start = pl.multiple_of(ks * GRID_W, GRID_W * (NA_ROWS // 2))
    q = q_ref[0]
    k_win = k_ref[0, pl.ds(start, NA_TK), :]
    v_win = v_ref[0, pl.ds(start, NA_TK), :]
    kc, vc = kc_ref[0], vc_ref[0]
    nt = (((1,), (1,)), ((), ()))
    out = jnp.zeros(q.shape, F32)
    for h in range(N_HEADS):
        hm = _head_lane_mask(h, q.shape)
        qh = jnp.where(hm, q, jnp.zeros_like(q))
        s_loc = lax.dot_general(qh, k_win, nt, preferred_element_type=F32) + bias_ref[0, h]
        s_ctx = lax.dot_general(qh, kc, nt, preferred_element_type=F32)
        m = jnp.maximum(jnp.max(s_loc, axis=-1, keepdims=True), jnp.max(s_ctx, axis=-1, keepdims=True))
        p_loc = jnp.exp(s_loc - m)
        p_ctx = jnp.exp(s_ctx - m)
        denom = jnp.sum(p_loc, axis=-1, keepdims=True) + jnp.sum(p_ctx, axis=-1, keepdims=True)
        oh = (jnp.dot(p_loc.astype(BF16), v_win, preferred_element_type=F32)
              + jnp.dot(p_ctx.astype(BF16), vc, preferred_element_type=F32)) / denom
        out = jnp.where(hm, oh, out)
    o_ref[0] = out.astype(o_ref.dtype)


def na_latent(na_lat, na_ctx, bias):
    b, l, _ = na_lat.shape
    lc = na_ctx.shape[1]
    rows = l // GRID_W
    nblk = rows // NA_QROWS
    pat = lambda bi, g: (jnp.where(g == 0, 0, jnp.where(g == nblk - 1, 2, 1)), 0, 0, 0)
    return pl.pallas_call(
        functools.partial(_na_kernel, rows=rows),
        grid=(b, nblk),
        in_specs=[
            pl.BlockSpec((1, NA_TQ, BRANCH_W), lambda bi, g: (bi, g, 0)),
            pl.BlockSpec((1, l, BRANCH_W), lambda bi, g: (bi, 0, 1)),
            pl.BlockSpec((1, l, BRANCH_W), lambda bi, g: (bi, 0, 2)),
            pl.BlockSpec((1, lc, BRANCH_W), lambda bi, g: (bi, 0, 1)),
            pl.BlockSpec((1, lc, BRANCH_W), lambda bi, g: (bi, 0, 2)),
            pl.BlockSpec((1, N_HEADS, NA_TQ, NA_TK), pat),
        ],
        out_specs=pl.BlockSpec((1, NA_TQ, BRANCH_W), lambda bi, g: (bi, g, 0)),
        out_shape=jax.ShapeDtypeStruct((b, l, BRANCH_W), BF16),
        compiler_params=_params("parallel", "arbitrary"),
        name="na_latent",
    )(na_lat, na_lat, na_lat, na_ctx, na_ctx, bias)


def _na_ctx_kernel(q_ref, k_ref, v_ref, o_ref):
    q, k, v = q_ref[0], k_ref[0], v_ref[0]
    nt = (((1,), (1,)), ((), ()))
    out = jnp.zeros(q.shape, F32)
    for h in range(N_HEADS):
        hm = _head_lane_mask(h, q.shape)
        s = lax.dot_general(jnp.where(hm, q, jnp.zeros_like(q)), k, nt, preferred_element_type=F32)
        p = jnp.exp(s - jnp.max(s, axis=-1, keepdims=True))
        oh = jnp.dot(p.astype(BF16), v, preferred_element_type=F32) / jnp.sum(p, axis=-1, keepdims=True)
        out = jnp.where(hm, oh, out)
    o_ref[0] = out.astype(o_ref.dtype)


def na_context(na_ctx):
    b, lc, _ = na_ctx.shape
    spec = lambda col: pl.BlockSpec((1, lc, BRANCH_W), lambda bi: (bi, 0, col))
    return pl.pallas_call(
        _na_ctx_kernel,
        grid=(b,),
        in_specs=[spec(0), spec(1), spec(2)],
        out_specs=pl.BlockSpec((1, lc, BRANCH_W), lambda bi: (bi, 0, 0)),
        out_shape=jax.ShapeDtypeStruct((b, lc, BRANCH_W), BF16),
        compiler_params=_params("parallel"),
        name="na_context",
    )(na_ctx, na_ctx, na_ctx)


def _merge_kernel(y5_ref, u5_ref, o1_ref, o2_ref, o3_ref, gate_ref, d5_ref, wglu_ref, wb_ref, wo_ref, x_ref, g1_ref,
                  nw_ref, sh_ref, sc_ref, rw_ref, xo_ref, h_ref, aff_ref):
    z = jax.nn.gelu(y5_ref[0] + d5_ref[...] * u5_ref[0].astype(F32))
    o_s5 = (z * jax.nn.sigmoid(jnp.dot(z.astype(BF16), wglu_ref[...], preferred_element_type=F32))).astype(BF16)
    m = None
    for br in range(N_BRANCH):
        o_br = o_s5 if br == 0 else (o1_ref, o2_ref, o3_ref)[br - 1][0]
        proj = jnp.dot(o_br, wb_ref[br], preferred_element_type=F32)
        gate = jax.nn.sigmoid(gate_ref[0, :, br * D_MODEL:(br + 1) * D_MODEL].astype(F32))
        m = gate * proj if m is None else m + gate * proj
    mix = jnp.dot(m.astype(BF16), wo_ref[...], preferred_element_type=F32)
    x_new = x_ref[0] + g1_ref[0] * mix
    xo_ref[0] = x_new
    h = _norm_mod(x_new, nw_ref[...], sh_ref[0], sc_ref[0])
    h_ref[0] = h.astype(BF16)
    logits = lax.dot_general(rw_ref[...], h, (((1,), (1,)), ((), ())),
                             precision=lax.Precision.HIGHEST, preferred_element_type=F32)
    logits = logits - jnp.max(logits, axis=0, keepdims=True)
    e = jnp.exp(logits)
    aff_ref[0] = e / jnp.sum(e, axis=0, keepdims=True)


def merge_and_route(y_s5, u_s5, outs, gates, s5_d, s5_glu_bf16, w_branch_bf16, w_out_bf16, x, g1,
                    norm_w, shift, scale, router_w_t, tm):
    b, n, d = x.shape
    tok = lambda bi, i: (bi, i, 0)
    per_b = lambda bi, i: (bi, 0, 0)
    const2 = lambda bi, i: (0, 0)
    const3 = lambda bi, i: (0, 0, 0)
    return pl.pallas_call(
        _merge_kernel,
        grid=(b, n // tm),
        in_specs=[pl.BlockSpec((1, tm, BRANCH_W), tok)] * (N_BRANCH + 1) + [
            pl.BlockSpec((1, tm, N_BRANCH * d), tok),
            pl.BlockSpec((1, BRANCH_W), const2),
            pl.BlockSpec((BRANCH_W, BRANCH_W), const2),
            pl.BlockSpec((N_BRANCH, BRANCH_W, d), const3),
            pl.BlockSpec((d, d), const2),
            pl.BlockSpec((1, tm, d), tok),
            pl.BlockSpec((1, 1, d), per_b),
            pl.BlockSpec((1, d), const2),
            pl.BlockSpec((1, 1, d), per_b),
            pl.BlockSpec((1, 1, d), per_b),
            pl.BlockSpec((N_EXPERTS, d), const2),
        ],
        out_specs=[
            pl.BlockSpec((1, tm, d), tok),
            pl.BlockSpec((1, tm, d), tok),
            pl.BlockSpec((1, N_EXPERTS, tm), lambda bi, i: (bi, 0, i)),
        ],
        out_shape=[
            jax.ShapeDtypeStruct((b, n, d), F32),
            jax.ShapeDtypeStruct((b, n, d), BF16),
            jax.ShapeDtypeStruct((b, N_EXPERTS, n), F32),
        ],
        compiler_params=_params("parallel", "parallel"),
        name="merge_route",
    )(y_s5, u_s5, *outs, gates, s5_d, s5_glu_bf16, w_branch_bf16, w_out_bf16, x, g1, norm_w, shift, scale,
      router_w_t)


MXU_N = 256


EXPERT_ROWS = 512
def _cast_kernel(w_ref, o_ref):
    o_ref[0] = w_ref[0, 0].astype(BF16)


def cast_layer_bf16(w, layer):
    _, e, r, c = w.shape
    return pl.pallas_call(
        _cast_kernel,
        grid=(e,),
        in_specs=[pl.BlockSpec((1, 1, r, c), lambda ei: (layer, ei, 0, 0))],
        out_specs=pl.BlockSpec((1, r, c), lambda ei: (ei, 0, 0)),
        out_shape=jax.ShapeDtypeStruct((e, r, c), BF16),
        compiler_params=_params("parallel"),
        name="cast_bf16",
    )(w)


def _expert_kernel(x_ref, g_ref, g2_ref, wg_ref, wu_ref, wd_ref, y_ref, act_ref):
    nb, cap, d = x_ref.shape[1:]
    xb = x_ref[0].reshape(nb * cap, d)
    f = wg_ref.shape[2]
    for c0 in range(0, f, MXU_N):
        c1 = min(c0 + MXU_N, f)
        gate = jnp.dot(xb, wg_ref[0, :, c0:c1], preferred_element_type=F32)
        up = jnp.dot(xb, wu_ref[0, :, c0:c1], preferred_element_type=F32)
        act_ref[:, c0:c1] = (jax.nn.silu(gate) * up).astype(BF16)
    y = jnp.dot(act_ref[...], wd_ref[0], preferred_element_type=F32).reshape(nb, cap, d)
    y_ref[0] = (y * g_ref[0] * g2_ref[...]).astype(y_ref.dtype)


def expert_ffn(xe, ge, g2, w_gate, w_up, w_down):
    e, b, cap, d = xe.shape
    f = w_gate.shape[-1]
    nb = max(1, min(b, EXPERT_ROWS // cap))
    tok = lambda ei, i: (ei, i, 0, 0)
    return pl.pallas_call(
        _expert_kernel,
        grid=(e, b // nb),
        in_specs=[
            pl.BlockSpec((1, nb, cap, d), tok),
            pl.BlockSpec((1, nb, cap, 1), tok),
            pl.BlockSpec((nb, 1, d), lambda ei, i: (i, 0, 0)),
            pl.BlockSpec((1, d, f), lambda ei, i: (ei, 0, 0)),
            pl.BlockSpec((1, d, f), lambda ei, i: (ei, 0, 0)),
            pl.BlockSpec((1, f, d), lambda ei, i: (ei, 0, 0)),
        ],
        out_specs=pl.BlockSpec((1, nb, cap, d), tok),
        out_shape=jax.ShapeDtypeStruct((e, b, cap, d), BF16),
        scratch_shapes=[pltpu.VMEM((nb * cap, f), BF16)],
        compiler_params=_params("parallel", "arbitrary"),
        name="expert_ffn",
    )(xe, ge, g2, w_gate, w_up, w_down)


COMBINE_GROUP = 8


def _combine_kernel(idx_ref, y_ref, x_hbm, o_ref, yf_ref, sem):
    bi = pl.program_id(0)
    e = pl.program_id(1)
    cap = y_ref.shape[2]

    @pl.when(e == 0)
    def _():
        cp = pltpu.make_async_copy(x_hbm.at[bi], o_ref.at[0], sem)
        cp.start()
        cp.wait()

    yf_ref[...] = y_ref[0, 0].astype(F32)

    def body(c, carry):
        base = pl.multiple_of(c * COMBINE_GROUP, COMBINE_GROUP)
        ys = yf_ref[pl.ds(base, COMBINE_GROUP), :]
        rows = [idx_ref[0, 0, e * cap + base + k] for k in range(COMBINE_GROUP)]
        new = [o_ref[0, pl.ds(rows[k], 1), :] + ys[k:k + 1, :] for k in range(COMBINE_GROUP)]
        for k in range(COMBINE_GROUP):
            o_ref[0, pl.ds(rows[k], 1), :] = new[k]
        return carry

    lax.fori_loop(0, cap // COMBINE_GROUP, body, 0)


def moe_combine(x, y, idx):
    b, n, d = x.shape
    e, _, cap, _ = y.shape
    return pl.pallas_call(
        _combine_kernel,
        grid=(b, e),
        in_specs=[
            pl.BlockSpec((1, 1, e * cap), lambda bi, ei: (bi, 0, 0), memory_space=pltpu.SMEM),
            pl.BlockSpec((1, 1, cap, d), lambda bi, ei: (ei, bi, 0, 0)),
            pl.BlockSpec(memory_space=pl.ANY),
        ],
        out_specs=pl.BlockSpec((1, n, d), lambda bi, ei: (bi, 0, 0)),
        out_shape=jax.ShapeDtypeStruct((b, n, d), F32),
        scratch_shapes=[pltpu.VMEM((cap, d), F32), pltpu.SemaphoreType.DMA(())],
        compiler_params=_params("parallel", "arbitrary"),
        name="moe_combine",
    )(idx, y, x)


def expert_choice_ffn(x, g2, h_bf16, aff_t, w_gate, w_up, w_down):
    b, n, d = h_bf16.shape
    cap = EC_CAPACITY * n // N_EXPERTS
    g, idx = lax.top_k(aff_t, cap)
    idx_e = jnp.moveaxis(idx, 1, 0)
    g_e = jnp.moveaxis(g, 1, 0)[..., None]
    xe = h_bf16[jnp.arange(b)[None, :, None], idx_e]
    y = expert_ffn(xe, g_e, g2, w_gate, w_up, w_down)
    return moe_combine(x, y, idx.reshape(b, 1, N_EXPERTS * cap).astype(jnp.int32))


S5_BLK = 16
S5_ROW = S5_GROUPS * S5_BLK * S5_GROUP
S5_PAIRS = S5_GROUPS // 2
S5_LANES = S5_GROUPS * S5_STATE


def _pair_blockdiag(t):
    g, r, c = t.shape
    t = t.reshape(g // 2, 2, r, c)
    z = jnp.zeros_like(t[:, 0])
    top = jnp.concatenate([t[:, 0], z], axis=-1)
    bot = jnp.concatenate([z, t[:, 1]], axis=-1)
    return jnp.concatenate([top, bot], axis=-2)


def s5_operators(lam_re, lam_im, log_step, b_re, b_im, c_re, c_im):
    hp = lax.Precision.HIGHEST
    blk = S5_BLK
    dt = jnp.exp(log_step)[..., None]
    k = jnp.arange(blk + 1, dtype=F32)
    mag = jnp.exp((lam_re * dt)[..., None] * k)
    ang = (lam_im * dt)[..., None] * k
    pr, pi = mag * jnp.cos(ang), mag * jnp.sin(ang)
    ar, ai = pr[..., 1], pi[..., 1]
    den = lam_re * lam_re + lam_im * lam_im
    zr = ((ar - 1.0) * lam_re + ai * lam_im) / den
    zi = (ai * lam_re - (ar - 1.0) * lam_im) / den
    bb_re = zr[..., None] * b_re - zi[..., None] * b_im
    bb_im = zr[..., None] * b_im + zi[..., None] * b_re
    ca_re = c_re[..., None] * pr[:, :, None] - c_im[..., None] * pi[:, :, None]
    ca_im = c_re[..., None] * pi[:, :, None] + c_im[..., None] * pr[:, :, None]
    kern = (jnp.einsum('dgpnl,dgnq->dglpq', ca_re, bb_re, precision=hp)
            - jnp.einsum('dgpnl,dgnq->dglpq', ca_im, bb_im, precision=hp))
    j = np.arange(blk)[:, None]
    i = np.arange(blk)[None, :]
    ms, ws, rres, rims = [], [], [], []
    for d in range(2):
        lag = (i - j) if d == 0 else (j - i)
        valid = jnp.asarray(lag >= 0, F32)[None, :, None, :, None]
        kd = kern[d][:, np.clip(lag, 0, blk - 1)]
        m = jnp.transpose(kd, (0, 1, 4, 2, 3)) * valid
        ms.append(m.reshape(S5_GROUPS, blk * S5_GROUP, blk * S5_GROUP))
        pw = (blk - 1 - np.arange(blk)) if d == 0 else np.arange(blk)
        apr, api = pr[d][..., pw], pi[d][..., pw]
        w_re = apr[..., None] * bb_re[d][:, :, None] - api[..., None] * bb_im[d][:, :, None]
        w_im = apr[..., None] * bb_im[d][:, :, None] + api[..., None] * bb_re[d][:, :, None]
        to_w = lambda t: jnp.transpose(t, (0, 2, 3, 1)).reshape(S5_GROUPS, blk * S5_GROUP, S5_STATE)
        ws.append((_pair_blockdiag(to_w(w_re)), _pair_blockdiag(to_w(w_im))))
        ex = (np.arange(blk) + 1) if d == 0 else (blk - np.arange(blk))
        r_re = ca_re[d][..., ex]
        r_im = -ca_im[d][..., ex]
        to_r = lambda t: jnp.transpose(t, (0, 2, 3, 1)).reshape(S5_GROUPS, S5_STATE, blk * S5_GROUP)
        rres.append(_pair_blockdiag(to_r(r_re)))
        rims.append(_pair_blockdiag(to_r(r_im)))
    m_op = jnp.stack(ms).astype(BF16)
    w_re = jnp.stack([w[0] for w in ws]).astype(BF16)
    w_im = jnp.stack([w[1] for w in ws]).astype(BF16)
    r_re = jnp.stack(rres).astype(BF16)
    r_im = jnp.stack(rims).astype(BF16)
    a_blk = jnp.stack([pr[..., blk].reshape(2, 1, S5_LANES), pi[..., blk].reshape(2, 1, S5_LANES)], axis=1)
    return m_op, w_re, w_im, r_re, r_im, a_blk


def _s5_kernel(uc_ref, ul_ref, m_ref, wre_ref, wim_ref, rre_ref, rim_ref, a_ref, yc_ref, yl_ref,
               vre, vim, sre, sim):
    rc, rl = uc_ref.shape[1], ul_ref.shape[1]
    segs = ((uc_ref, yc_ref, 0, rc), (ul_ref, yl_ref, rc, rl))
    gw = S5_BLK * S5_GROUP
    for d in range(2):
        for u_ref, _, base, rows in segs:
            for h in range(S5_PAIRS):
                u_pair = u_ref[0, :, 2 * h * gw:2 * (h + 1) * gw]
                vre[base:base + rows, h * 128:(h + 1) * 128] = jnp.dot(
                    u_pair, wre_ref[d, h], preferred_element_type=F32)
                vim[base:base + rows, h * 128:(h + 1) * 128] = jnp.dot(
                    u_pair, wim_ref[d, h], preferred_element_type=F32)
        ar, ai = a_ref[d, 0], a_ref[d, 1]

        def run(base, rows, carry):
            def step(t, c):
                xr, xi = c
                idx = base + (t if d == 0 else rows - 1 - t)
                sre[pl.ds(idx, 1), :] = xr
                sim[pl.ds(idx, 1), :] = xi
                nr = ar * xr - ai * xi + vre[pl.ds(idx, 1), :]
                ni = ar * xi + ai * xr + vim[pl.ds(idx, 1), :]
                return nr, ni
            return lax.fori_loop(0, rows, step, carry)

        zero = jnp.zeros((1, S5_LANES), F32)
        carry = run(0, rc, (zero, zero))
        run(rc, rl, carry)
        for u_ref, y_ref, base, rows in segs:
            for h in range(S5_PAIRS):
                s_r = sre[base:base + rows, h * 128:(h + 1) * 128].astype(BF16)
                s_i = sim[base:base + rows, h * 128:(h + 1) * 128].astype(BF16)
                y = (jnp.dot(s_r, rre_ref[d, h], preferred_element_type=F32)
                     + jnp.dot(s_i, rim_ref[d, h], preferred_element_type=F32))
                for gl in range(2):
                    g = 2 * h + gl
                    yg = y[:, gl * gw:(gl + 1) * gw] + jnp.dot(
                        u_ref[0, :, g * gw:(g + 1) * gw], m_ref[d, g], preferred_element_type=F32)
                    if d == 0:
                        y_ref[0, :, g * gw:(g + 1) * gw] = yg
                    else:
                        y_ref[0, :, g * gw:(g + 1) * gw] += yg


def _to_s5_rows(u):
    b, n, _ = u.shape
    u = u.reshape(b, n // S5_BLK, S5_BLK, S5_GROUPS, S5_GROUP)
    return jnp.transpose(u, (0, 1, 3, 2, 4)).reshape(b, n // S5_BLK, S5_ROW)


def _from_s5_rows(y):
    b, r, _ = y.shape
    y = y.reshape(b, r, S5_GROUPS, S5_BLK, S5_GROUP)
    return jnp.transpose(y, (0, 1, 3, 2, 4)).reshape(b, r * S5_BLK, BRANCH_W)


def s5_scan_readout(u_ctx, u_lat, ops):
    m_op, w_re, w_im, r_re, r_im, a_blk = ops
    b = u_lat.shape[0]
    uc, ul = _to_s5_rows(u_ctx), _to_s5_rows(u_lat)
    rc, rl = uc.shape[1], ul.shape[1]
    per_b = lambda bi: (bi, 0, 0)
    c4 = lambda bi: (0, 0, 0, 0)
    yc, yl = pl.pallas_call(
        _s5_kernel,
        grid=(b,),
        in_specs=[
            pl.BlockSpec((1, rc, S5_ROW), per_b),
            pl.BlockSpec((1, rl, S5_ROW), per_b),
            pl.BlockSpec(m_op.shape, c4),
            pl.BlockSpec(w_re.shape, c4),
            pl.BlockSpec(w_im.shape, c4),
            pl.BlockSpec(r_re.shape, c4),
            pl.BlockSpec(r_im.shape, c4),
            pl.BlockSpec(a_blk.shape, c4),
        ],
        out_specs=[pl.BlockSpec((1, rc, S5_ROW), per_b), pl.BlockSpec((1, rl, S5_ROW), per_b)],
        out_shape=[jax.ShapeDtypeStruct((b, rc, S5_ROW), F32), jax.ShapeDtypeStruct((b, rl, S5_ROW), F32)],
        scratch_shapes=[pltpu.VMEM((rc + rl, S5_LANES), F32) for _ in range(4)],
        compiler_params=_params("parallel"),
        name="s5_scan",
    )(uc, ul, m_op, w_re, w_im, r_re, r_im, a_blk)
    return _from_s5_rows(yc), _from_s5_rows(yl)


LA_TILE = 128


def _stack_heads(q):
    return jnp.concatenate(
        [jnp.where(_head_lane_mask(h, q.shape), q, jnp.zeros_like(q)) for h in range(N_HEADS)], axis=0)


def _unstack_heads(o_stack, t):
    out = jnp.zeros((t, o_stack.shape[1]), o_stack.dtype)
    for h in range(N_HEADS):
        blk = o_stack[h * t:(h + 1) * t]
        out = jnp.where(_head_lane_mask(h, blk.shape), blk, out)
    return out


_NT = (((1,), (1,)), ((), ()))


def _state_update(st_ref, decay_lane, v, k_scaled, head_avg):
    vt = jnp.transpose(v.astype(F32)).astype(BF16)
    kv = jnp.dot(vt, k_scaled.astype(BF16), preferred_element_type=F32)
    st_ref[...] = decay_lane * st_ref[...] + jnp.where(head_avg > 0, kv, 0.0)


def _for_tiles(n_tiles, reverse, body):
    def step(i, carry):
        body(n_tiles - 1 - i if reverse else i)
        return carry
    lax.fori_loop(0, n_tiles, step, 0)


def _ret_kernel(qc_ref, kc_ref, vc_ref, gc_ref, ql_ref, kl_ref, vl_ref, gl_ref, dmask_ref, xi_ref, zeta_ref,
                gam_ref, havg_ref, oc_ref, ol_ref, st_ref, accc_ref, accl_ref):
    t = LA_TILE
    segs = ((qc_ref, kc_ref, vc_ref, accc_ref), (ql_ref, kl_ref, vl_ref, accl_ref))
    for d in range(2):
        st_ref[...] = jnp.zeros_like(st_ref)
        for q_ref, k_ref, v_ref, acc_ref in segs:
            def tile(i, q_ref=q_ref, k_ref=k_ref, v_ref=v_ref, acc_ref=acc_ref):
                rows = pl.ds(pl.multiple_of(i * t, t), t)
                q, k, v = q_ref[0, rows, :], k_ref[0, rows, :], v_ref[0, rows, :]
                att = lax.dot_general(_stack_heads(q), k, _NT, preferred_element_type=F32) * dmask_ref[d]
                intra = _unstack_heads(jnp.dot(att.astype(BF16), v, preferred_element_type=F32), t)
                cross = lax.dot_general(q, st_ref[...].astype(BF16), _NT, preferred_element_type=F32)
                o = intra + cross * xi_ref[d]
                if d == 0:
                    acc_ref[rows, :] = o
                else:
                    acc_ref[rows, :] += o
                _state_update(st_ref, gam_ref[d], v, k.astype(F32) * zeta_ref[d], havg_ref[...])
            _for_tiles(q_ref.shape[1] // t, d == 1, tile)
    for acc_ref, g_ref, o_ref in ((accc_ref, gc_ref, oc_ref), (accl_ref, gl_ref, ol_ref)):
        o = acc_ref[...]
        o = o * lax.rsqrt(_head_mean_sq(o, havg_ref[...]) + EPS)
        o_ref[0] = (o * jax.nn.silu(g_ref[0].astype(F32))).astype(o_ref.dtype)


def retention_tables(decay_logit):
    t = LA_TILE
    lg = jax.nn.log_sigmoid(decay_logit)
    idx = jnp.arange(t, dtype=F32)
    diff = idx[:, None] - idx[None, :]
    lgm = lg[:, :, None, None]
    fwd = jnp.where(diff >= 0, jnp.exp(jnp.maximum(diff, 0.0) * lgm[0]), 0.0)
    bwd = jnp.where(diff <= 0, jnp.exp(jnp.maximum(-diff, 0.0) * lgm[1]), 0.0)
    dmask = jnp.stack([fwd.reshape(N_HEADS * t, t), bwd.reshape(N_HEADS * t, t)])
    lane = lambda a: jnp.repeat(a, HEAD_DIM, axis=-1)
    lg_l = lane(lg)[:, None, :]
    steps_q = jnp.stack([idx + 1.0, t - idx])[:, :, None]
    steps_k = jnp.stack([t - 1.0 - idx, idx])[:, :, None]
    xi = jnp.exp(steps_q * lg_l)
    zeta = jnp.exp(steps_k * lg_l)
    gam = jnp.exp(t * lg_l)
    return dmask, xi, zeta, gam


def retention_pallas(rt_ctx, rt_lat, decay_logit):
    b, l, _ = rt_lat.shape
    lc = rt_ctx.shape[1]
    dmask, xi, zeta, gam = retention_tables(decay_logit)
    col = lambda n, c: pl.BlockSpec((1, n, BRANCH_W), lambda bi: (bi, 0, c))
    const = lambda a: pl.BlockSpec(a.shape, lambda bi: (0,) * a.ndim)
    havg = head_avg_matrix()
    return pl.pallas_call(
        _ret_kernel,
        grid=(b,),
        in_specs=[col(lc, c) for c in range(4)] + [col(l, c) for c in range(4)]
        + [const(dmask), const(xi), const(zeta), const(gam), const(havg)],
        out_specs=[col(lc, 0), col(l, 0)],
        out_shape=[jax.ShapeDtypeStruct((b, lc, BRANCH_W), BF16), jax.ShapeDtypeStruct((b, l, BRANCH_W), BF16)],
        scratch_shapes=[pltpu.VMEM((BRANCH_W, BRANCH_W), F32), pltpu.VMEM((lc, BRANCH_W), F32),
                        pltpu.VMEM((l, BRANCH_W), F32)],
        compiler_params=_params("parallel"),
        name="retention",
    )(rt_ctx, rt_ctx, rt_ctx, rt_ctx, rt_lat, rt_lat, rt_lat, rt_lat, dmask, xi, zeta, gam, havg)


HG_LEVELS = (32, 64, 128)


def hg_level_masks():
    t = LA_TILE
    i = (np.arange(N_HEADS * t) % t)[:, None]
    j = np.arange(t)[None, :]
    out = []
    for d in range(2):
        masks = [(i // HG_CHUNK == j // HG_CHUNK) & ((j <= i) if d == 0 else (j >= i))]
        for blk in HG_LEVELS:
            qi_late = (i % blk) >= blk // 2
            kj_late = (j % blk) >= blk // 2
            cross = (qi_late & ~kj_late) if d == 0 else (~qi_late & kj_late)
            masks.append((i // blk == j // blk) & cross)
        out.append(np.stack(masks))
    return jnp.asarray(np.stack(out), F32)


def _exact_rows_sum(sel, x):
    hi = x.astype(BF16)
    r1 = x - hi.astype(F32)
    mid = r1.astype(BF16)
    lo = (r1 - mid.astype(F32)).astype(BF16)
    return (jnp.dot(sel, hi, preferred_element_type=F32) + jnp.dot(sel, mid, preferred_element_type=F32)
            + jnp.dot(sel, lo, preferred_element_type=F32))


def _hg_tile(d, q, k, v, logf, st_ref, tri, mask_ref):
    t = LA_TILE
    w = q.shape[1]
    g = _exact_rows_sum(tri, logf)
    nb = t // HG_CHUNK
    g3 = g.reshape(nb, HG_CHUNK, w)
    if d == 0:
        edge = g3[:, HG_CHUNK - 1:HG_CHUNK, :]
        prev = jnp.concatenate([jnp.zeros((1, 1, w), F32), edge[:-1]], axis=0)
    else:
        edge = g3[:, 0:1, :]
        prev = jnp.concatenate([edge[1:], jnp.zeros((1, 1, w), F32)], axis=0)
    cum = (g3 - prev).reshape(t, w)
    qs = [q * jnp.exp(cum)]
    ks = [k * jnp.exp(-cum)]
    for blk in HG_LEVELS:
        gb = g.reshape(t // blk, blk, w)
        row = blk // 2 - 1 if d == 0 else blk // 2
        mid = jnp.broadcast_to(gb[:, row:row + 1, :], gb.shape).reshape(t, w)
        qs.append(q * jnp.exp(jnp.minimum(g - mid, 0.0)))
        ks.append(k * jnp.exp(jnp.minimum(mid - g, 0.0)))
    att = None
    for lvl, (qq, kk) in enumerate(zip(qs, ks)):
        a = lax.dot_general(_stack_heads(qq.astype(BF16)), kk.astype(BF16), _NT, preferred_element_type=F32)
        a = jnp.where(mask_ref[d, lvl] > 0.5, a, 0.0)
        att = a if att is None else att + a
    intra = _unstack_heads(jnp.dot(att.astype(BF16), v, preferred_element_type=F32), t)
    cross = lax.dot_general((q * jnp.exp(g)).astype(BF16), st_ref[...].astype(BF16), _NT,
                            preferred_element_type=F32)
    total = g[t - 1:t, :] if d == 0 else g[0:1, :]
    return intra + cross, jnp.exp(total), k * jnp.exp(total - g)


def _hg_kernel(pc_ref, pl_ref, lb_ref, nw_ref, havg_ref, mask_ref, oc_ref, ol_ref, st_ref, accc_ref, accl_ref):
    t = LA_TILE
    w = BRANCH_W
    lb = lb_ref[...]
    r = lax.broadcasted_iota(jnp.int32, (t, t), 0)
    c = lax.broadcasted_iota(jnp.int32, (t, t), 1)
    for d in range(2):
        tri = jnp.where((c <= r) if d == 0 else (c >= r), 1.0, 0.0).astype(BF16)
        st_ref[...] = jnp.zeros_like(st_ref)
        for p_ref, acc_ref in ((pc_ref, accc_ref), (pl_ref, accl_ref)):
            def tile(i, p_ref=p_ref, acc_ref=acc_ref):
                rows = pl.ds(pl.multiple_of(i * t, t), t)
                q = jax.nn.silu(p_ref[0, rows, 0:w].astype(F32))
                f_logit = p_ref[0, rows, (1 + d) * w:(2 + d) * w].astype(F32)
                v = p_ref[0, rows, 3 * w:4 * w]
                fg = lb + (1.0 - lb) * jax.nn.sigmoid(f_logit)
                o, decay, k_end = _hg_tile(d, q, 1.0 - fg, v, jnp.log(fg), st_ref, tri, mask_ref)
                if d == 0:
                    acc_ref[rows, :] = o
                else:
                    acc_ref[rows, :] += o
                _state_update(st_ref, decay, v, k_end, havg_ref[...])
            _for_tiles(p_ref.shape[1] // t, d == 1, tile)
    for acc_ref, p_ref, o_ref in ((accc_ref, pc_ref, oc_ref), (accl_ref, pl_ref, ol_ref)):
        o = acc_ref[...]
        o = o * lax.rsqrt(_head_mean_sq(o, havg_ref[...]) + EPS) * nw_ref[...]
        o_ref[0] = (o * jax.nn.silu(p_ref[0, :, 4 * w:5 * w].astype(F32))).astype(o_ref.dtype)


def hgrn2_pallas(hg_ctx, hg_lat, lower_bound, norm_w):
    b, l, width = hg_lat.shape
    lc = hg_ctx.shape[1]
    full = lambda n: pl.BlockSpec((1, n, width), lambda bi: (bi, 0, 0))
    out = lambda n: pl.BlockSpec((1, n, BRANCH_W), lambda bi: (bi, 0, 0))
    vec = pl.BlockSpec((1, BRANCH_W), lambda bi: (0, 0))
    havg = head_avg_matrix()
    masks = hg_level_masks()
    return pl.pallas_call(
        _hg_kernel,
        grid=(b,),
        in_specs=[full(lc), full(l), vec, vec, pl.BlockSpec(havg.shape, lambda bi: (0, 0)),
                  pl.BlockSpec(masks.shape, lambda bi: (0, 0, 0, 0))],
        out_specs=[out(lc), out(l)],
        out_shape=[jax.ShapeDtypeStruct((b, lc, BRANCH_W), BF16), jax.ShapeDtypeStruct((b, l, BRANCH_W), BF16)],
        scratch_shapes=[pltpu.VMEM((BRANCH_W, BRANCH_W), F32), pltpu.VMEM((lc, BRANCH_W), F32),
                        pltpu.VMEM((l, BRANCH_W), F32)],
        compiler_params=_params("parallel"),
        name="hgrn2",
    )(hg_ctx, hg_lat, lower_bound[None], jnp.tile(norm_w, N_HEADS)[None], havg, masks)


def kernel(x, c, ctx, c_ctx, ada_w, ada_b, norm_mix_w, norm_ffn_w, w_in, s5_lam_re, s5_lam_im, s5_log_step,
           s5_b_re, s5_b_im, s5_c_re, s5_c_im, s5_d, s5_glu_w, na_q_norm, na_k_norm, na_rpb, hg_lower_bounds,
           hg_norm_w, ret_decay_logit, w_branch, w_out, router_w, ex_w_gate, ex_w_up, ex_w_down):
    b = x.shape[0]
    lb_p = jax.nn.softmax(hg_lower_bounds, axis=0)
    lower_bounds = jnp.cumsum(lb_p, axis=0) - lb_p[0]
    cond_rows = jnp.concatenate([c, jnp.broadcast_to(c_ctx[None], c.shape)], axis=0)
    mods = ada_modulation(cond_rows, ada_w, ada_b, D_MODEL)
    xc = ctx
    for li in range(DEPTH):
        last = li == DEPTH - 1
        mod_l = [m[:, None] for m in jnp.split(mods[li, :b], 6, axis=-1)]
        mod_c = [m[:, None] for m in jnp.split(mods[li, b:], 6, axis=-1)]
        sh1_l, sc1_l, g1_l, sh2_l, sc2_l, g2_l = mod_l
        sh1_c, sc1_c, g1_c, sh2_c, sc2_c, g2_c = mod_c
        w_in_b = w_in[li].astype(BF16)
        wb_b = w_branch[li].astype(BF16)
        wo_b = w_out[li].astype(BF16)
        rw_t = router_w[li].T
        nmw = norm_mix_w[li][None]
        nfw = norm_ffn_w[li][None]

        rt_off = sum(IN_SPLITS[:3])
        w_rot_b = jnp.concatenate(
            [_swap_head_halves_cols(w_in[li][:, rt_off + i * BRANCH_W:rt_off + (i + 1) * BRANCH_W]) for i in range(2)],
            axis=1).astype(BF16)
        qk_w = jnp.stack([jnp.tile(na_q_norm[li], N_HEADS), jnp.tile(na_k_norm[li], N_HEADS)])
        s5_ops = s5_operators(s5_lam_re[li], s5_lam_im[li], s5_log_step[li], s5_b_re[li], s5_b_im[li],
                              s5_c_re[li], s5_c_im[li])
        s5_dl = s5_d[li][None]
        glu_b = s5_glu_w[li].astype(BF16)

        pl_ = in_proj(x, nmw, sh1_l, sc1_l, w_in_b, w_rot_b, qk_w, True, 512)
        pc_ = in_proj(xc, nmw, sh1_c, sc1_c, w_in_b, w_rot_b, qk_w, False, 256)
        y5_c, y5_l = s5_scan_readout(pc_[0], pl_[0], s5_ops)
        na_l = na_latent(pl_[1], pc_[1], na_bias(na_rpb[li], x.shape[1] // GRID_W))
        hg_c, hg_l = hgrn2_pallas(pc_[2], pl_[2], lower_bounds[li], hg_norm_w[li])
        rt_c, rt_l = retention_pallas(pc_[3], pl_[3], ret_decay_logit[li])

        x, h_l, aff_l = merge_and_route(y5_l, pl_[0], (na_l, hg_l, rt_l), pl_[4], s5_dl, glu_b, wb_b, wo_b, x, g1_l,
                                        nfw, sh2_l, sc2_l, rw_t, 512)
        ex_w = tuple(cast_layer_bf16(w, li) for w in (ex_w_gate, ex_w_up, ex_w_down))
        x = expert_choice_ffn(x, g2_l, h_l, aff_l, *ex_w)
        if not last:
            na_c = na_context(pc_[1])
            xc, h_c, aff_c = merge_and_route(y5_c, pc_[0], (na_c, hg_c, rt_c), pc_[4], s5_dl, glu_b, wb_b, wo_b, xc,
                                             g1_c, nfw, sh2_c, sc2_c, rw_t, 256)
            xc = expert_choice_ffn(xc, g2_c, h_c, aff_c, *ex_w)
    return x
```

```python
import functools
import math

import jax
import jax.numpy as jnp
import numpy as np
from jax import lax
from jax.experimental import pallas as pl
from jax.experimental.pallas import tpu as pltpu

D_MODEL = 1024
DEPTH = 2
GRID_W = 64
N_BRANCH = 4
BRANCH_W = 256
HEAD_DIM = 64
N_HEADS = BRANCH_W // HEAD_DIM
S5_GROUP = 16
S5_GROUPS = BRANCH_W // S5_GROUP
S5_STATE = 64
NA_ROWS = 8
NA_COLS = 16
HG_CHUNK = 16
RET_CHUNK = 128
N_EXPERTS = 16
EC_CAPACITY = 2
D_EXPERT = 2816
ROPE_BASE = 10000.0
EPS = 1e-6
IN_SPLITS = (BRANCH_W, 3 * BRANCH_W, 5 * BRANCH_W, 4 * BRANCH_W, N_BRANCH * D_MODEL)
D_IN = sum(IN_SPLITS)

F32 = jnp.float32
BF16 = jnp.bfloat16

V7X_VMEM_BYTES = 64 * 1024 * 1024
VMEM_LIMIT = V7X_VMEM_BYTES - 8 * 1024 * 1024


def _params(*sem):
    return pltpu.CompilerParams(dimension_semantics=sem, vmem_limit_bytes=VMEM_LIMIT)


def _norm_mod(x, norm_w, shift, scale):
    y = x * lax.rsqrt(jnp.mean(x * x, axis=-1, keepdims=True) + EPS) * norm_w
    return y * (1.0 + scale) + shift


def _ada_kernel(c_ref, w_ref, b_ref, o_ref):
    cond = jax.nn.silu(c_ref[...])
    o_ref[0] = jnp.dot(cond, w_ref[0], precision=lax.Precision.HIGHEST, preferred_element_type=F32) + b_ref[0]


def ada_modulation(cond_in, ada_w, ada_b, tn):
    r, d = cond_in.shape
    depth, _, n = ada_w.shape
    return pl.pallas_call(
        _ada_kernel,
        grid=(depth, n // tn),
        in_specs=[
            pl.BlockSpec((r, d), lambda l, j: (0, 0)),
            pl.BlockSpec((1, d, tn), lambda l, j: (l, 0, j)),
            pl.BlockSpec((1, 1, tn), lambda l, j: (l, 0, j)),
        ],
        out_specs=pl.BlockSpec((1, r, tn), lambda l, j: (l, 0, j)),
        out_shape=jax.ShapeDtypeStruct((depth, r, n), F32),
        compiler_params=_params("parallel", "parallel"),
        name="ada_modulation",
    )(cond_in, ada_w, ada_b[:, None, :])


def _head_mean_sq(t, head_avg):
    sq = t * t
    hi = sq.astype(BF16)
    lo = (sq - hi.astype(F32)).astype(BF16)
    return (jnp.dot(hi, head_avg, preferred_element_type=F32)
            + jnp.dot(lo, head_avg, preferred_element_type=F32))


def _in_proj_kernel(x_ref, nw_ref, sh_ref, sc_ref, w_ref, wrot_ref, qkw_ref, havg_ref, cos_ref, sin_ref,
                    s5_ref, na_ref, hg_ref, rt_ref, gate_ref, *, rope):
    hb = _norm_mod(x_ref[0], nw_ref[...], sh_ref[0], sc_ref[0]).astype(BF16)
    cw = BRANCH_W
    proj = lambda col: jnp.dot(hb, w_ref[:, col * cw:(col + 1) * cw], preferred_element_type=F32)
    col = 0
    s5_ref[0] = proj(col).astype(BF16)
    col += 1
    for part in range(3):
        t = proj(col + part)
        if part < 2:
            t = t * lax.rsqrt(_head_mean_sq(t, havg_ref[...]) + EPS) * qkw_ref[part:part + 1, :]
        if part == 0:
            t = t * (HEAD_DIM ** -0.5)
        na_ref[0, :, part * cw:(part + 1) * cw] = t.astype(BF16)
    col += 3
    for part in range(5):
        hg_ref[0, :, part * cw:(part + 1) * cw] = proj(col + part).astype(BF16)
    col += 5
    for part in range(4):
        t = proj(col + part)
        if part < 2 and rope:
            swapped = jnp.dot(hb, wrot_ref[:, part * cw:(part + 1) * cw], preferred_element_type=F32)
            t = t * cos_ref[...] + swapped * sin_ref[...]
        if part == 1:
            t = t * (HEAD_DIM ** -0.5)
        rt_ref[0, :, part * cw:(part + 1) * cw] = t.astype(BF16)
    col += 4
    for part in range(N_BRANCH * D_MODEL // cw):
        gate_ref[0, :, part * cw:(part + 1) * cw] = proj(col + part).astype(BF16)


def rope_tables(n_tokens):
    quarter = HEAD_DIM // 4
    t = jnp.arange(n_tokens, dtype=jnp.int32)
    inv = jnp.asarray(ROPE_BASE ** (-np.arange(quarter) / quarter), F32)
    ang = jnp.concatenate([(t // GRID_W).astype(F32)[:, None] * inv, (t % GRID_W).astype(F32)[:, None] * inv], axis=1)
    cos, sin = jnp.cos(ang), jnp.sin(ang)
    cos_h = jnp.concatenate([cos, cos], axis=1)
    sin_h = jnp.concatenate([-sin, sin], axis=1)
    return jnp.tile(cos_h, (1, N_HEADS)), jnp.tile(sin_h, (1, N_HEADS))


def _swap_head_halves_cols(w):
    d, c = w.shape
    w = w.reshape(d, c // HEAD_DIM, 2, HEAD_DIM // 2)
    return w[:, :, ::-1, :].reshape(d, c)


def head_avg_matrix():
    h = np.arange(BRANCH_W) // HEAD_DIM
    return jnp.asarray((h[:, None] == h[None, :]) / HEAD_DIM, BF16)


def in_proj(x, norm_w, shift, scale, w_in_bf16, w_rot_bf16, qk_norm_w, rope, tm):
    b, n, d = x.shape
    tok = lambda bi, i: (bi, i, 0)
    per_b = lambda bi, i: (bi, 0, 0)
    const2 = lambda bi, i: (0, 0)
    cos, sin = rope_tables(n)
    return pl.pallas_call(
        functools.partial(_in_proj_kernel, rope=rope),
        grid=(b, n // tm),
        in_specs=[
            pl.BlockSpec((1, tm, d), tok),
            pl.BlockSpec((1, d), const2),
            pl.BlockSpec((1, 1, d), per_b),
            pl.BlockSpec((1, 1, d), per_b),
            pl.BlockSpec((d, D_IN), const2, pipeline_mode=pl.Buffered(1)),
            pl.BlockSpec((d, 2 * BRANCH_W), const2, pipeline_mode=pl.Buffered(1)),
            pl.BlockSpec((2, BRANCH_W), const2),
            pl.BlockSpec((BRANCH_W, BRANCH_W), const2),
            pl.BlockSpec((tm, BRANCH_W), lambda bi, i: (i, 0)),
            pl.BlockSpec((tm, BRANCH_W), lambda bi, i: (i, 0)),
        ],
        out_specs=[pl.BlockSpec((1, tm, w), tok) for w in IN_SPLITS],
        out_shape=[jax.ShapeDtypeStruct((b, n, w), BF16) for w in IN_SPLITS],
        compiler_params=_params("parallel", "parallel"),
        name="in_proj",
    )(x, norm_w, shift, scale, w_in_bf16, w_rot_bf16, qk_norm_w, head_avg_matrix(), cos, sin)


NA_QROWS = 8
NA_KROWS = 16
NA_TQ = NA_QROWS * GRID_W
NA_TK = NA_KROWS * GRID_W
NA_MASKED = -1e30


def na_bias(rpb, rows):
    hp = lax.Precision.HIGHEST
    nblk = rows // NA_QROWS
    n_dr, n_dc = 2 * NA_ROWS - 1, 2 * NA_COLS - 1
    qc = np.arange(GRID_W)[:, None]
    kc = np.arange(GRID_W)[None, :]
    cs = np.clip(qc - NA_COLS // 2, 0, GRID_W - NA_COLS)
    col_ok = (kc >= cs) & (kc < cs + NA_COLS)
    dc = np.clip(kc - qc + NA_COLS - 1, 0, n_dc - 1).reshape(-1)
    sel_dc = jnp.asarray(dc[None, :] == np.arange(n_dc)[:, None], F32)
    by_col = jnp.einsum('hrc,cx->hrx', rpb.astype(F32), sel_dc, precision=hp)
    pats = []
    for g in (0, 1, nblk - 1):
        ks = int(np.clip(NA_QROWS * g - NA_ROWS // 2, 0, rows - NA_KROWS))
        qr = (NA_QROWS * g + np.arange(NA_QROWS))[:, None]
        kr = (ks + np.arange(NA_KROWS))[None, :]
        band = np.clip(qr - NA_ROWS // 2, 0, rows - NA_ROWS)
        row_ok = (kr >= band) & (kr < band + NA_ROWS)
        dr = np.clip(kr - qr + NA_ROWS - 1, 0, n_dr - 1).reshape(-1)
        sel_dr = jnp.asarray(dr[:, None] == np.arange(n_dr)[None, :], F32)
        t = jnp.einsum('yr,hrx->hyx', sel_dr, by_col, precision=hp)
        t = t.reshape(N_HEADS, NA_QROWS, NA_KROWS, GRID_W, GRID_W)
        t = jnp.transpose(t, (0, 1, 3, 2, 4)).reshape(N_HEADS, NA_TQ, NA_TK)
        ok = (row_ok[:, None, :, None] & col_ok[None, :, None, :]).reshape(NA_TQ, NA_TK)
        pats.append(jnp.where(jnp.asarray(ok)[None], t, NA_MASKED))
    return jnp.stack(pats)


def _head_lane_mask(h, shape):
    lane = lax.broadcasted_iota(jnp.int32, shape, len(shape) - 1)
    return (lane >= h * HEAD_DIM) & (lane < (h + 1) * HEAD_DIM)


def _na_kernel(q_ref, k_ref, v_ref, kc_ref, vc_ref, bias_ref, o_ref, *, rows):
    g = pl.program_id(1)
    ks = jnp.clip(NA_QROWS * g - NA_ROWS // 2, 0, rows - NA_KROWS)
    start = pl.multiple_of(ks * GRID_W, GRID_W * (NA_ROWS // 2))
    q = q_ref[0]
    k_win = k_ref[0, pl.ds(start, NA_TK), :]
    v_win = v_ref[0, pl.ds(start, NA_TK), :]
    kc, vc = kc_ref[0], vc_ref[0]
    nt = (((1,), (1,)), ((), ()))
    out = jnp.zeros(q.shape, F32)
    for h in range(N_HEADS):
        hm = _head_lane_mask(h, q.shape)
        qh = jnp.where(hm, q, jnp.zeros_like(q))
        s_loc = lax.dot_general(qh, k_win, nt, preferred_element_type=F32) + bias_ref[0, h]
        s_ctx = lax.dot_general(qh, kc, nt, preferred_element_type=F32)
        m = jnp.maximum(jnp.max(s_loc, axis=-1, keepdims=True), jnp.max(s_ctx, axis=-1, keepdims=True))
        p_loc = jnp.exp(s_loc - m)
        p_ctx = jnp.exp(s_ctx - m)
        denom = jnp.sum(p_loc, axis=-1, keepdims=True) + jnp.sum(p_ctx, axis=-1, keepdims=True)
        oh = (jnp.dot(p_loc.astype(BF16), v_win, preferred_element_type=F32)
              + jnp.dot(p_ctx.astype(BF16), vc, preferred_element_type=F32)) / denom
        out = jnp.where(hm, oh, out)
    o_ref[0] = out.astype(o_ref.dtype)


def na_latent(na_lat, na_ctx, bias):
    b, l, _ = na_lat.shape
    lc = na_ctx.shape[1]
    rows = l // GRID_W
    nblk = rows // NA_QROWS
    pat = lambda bi, g: (jnp.where(g == 0, 0, jnp.where(g == nblk - 1, 2, 1)), 0, 0, 0)
    return pl.pallas_call(
        functools.partial(_na_kernel, rows=rows),
        grid=(b, nblk),
        in_specs=[
            pl.BlockSpec((1, NA_TQ, BRANCH_W), lambda bi, g: (bi, g, 0)),
            pl.BlockSpec((1, l, BRANCH_W), lambda bi, g: (bi, 0, 1)),
            pl.BlockSpec((1, l, BRANCH_W), lambda bi, g: (bi, 0, 2)),
            pl.BlockSpec((1, lc, BRANCH_W), lambda bi, g: (bi, 0, 1)),
            pl.BlockSpec((1, lc, BRANCH_W), lambda bi, g: (bi, 0, 2)),
            pl.BlockSpec((1, N_HEADS, NA_TQ, NA_TK), pat),
        ],
        out_specs=pl.BlockSpec((1, NA_TQ, BRANCH_W), lambda bi, g: (bi, g, 0)),
        out_shape=jax.ShapeDtypeStruct((b, l, BRANCH_W), BF16),
        compiler_params=_params("parallel", "arbitrary"),
        name="na_latent",
    )(na_lat, na_lat, na_lat, na_ctx, na_ctx, bias)


def _na_ctx_kernel(q_ref, k_ref, v_ref, o_ref):
    q, k, v = q_ref[0], k_ref[0], v_ref[0]
    nt = (((1,), (1,)), ((), ()))
    out = jnp.zeros(q.shape, F32)
    for h in range(N_HEADS):
        hm = _head_lane_mask(h, q.shape)
        s = lax.dot_general(jnp.where(hm, q, jnp.zeros_like(q)), k, nt, preferred_element_type=F32)
        p = jnp.exp(s - jnp.max(s, axis=-1, keepdims=True))
        oh = jnp.dot(p.astype(BF16), v, preferred_element_type=F32) / jnp.sum(p, axis=-1, keepdims=True)
        out = jnp.where(hm, oh, out)
    o_ref[0] = out.astype(o_ref.dtype)


def na_context(na_ctx):
    b, lc, _ = na_ctx.shape
    spec = lambda col: pl.BlockSpec((1, lc, BRANCH_W), lambda bi: (bi, 0, col))
    return pl.pallas_call(
        _na_ctx_kernel,
        grid=(b,),
        in_specs=[spec(0), spec(1), spec(2)],
        out_specs=pl.BlockSpec((1, lc, BRANCH_W), lambda bi: (bi, 0, 0)),
        out_shape=jax.ShapeDtypeStruct((b, lc, BRANCH_W), BF16),
        compiler_params=_params("parallel"),
        name="na_context",
    )(na_ctx, na_ctx, na_ctx)


def _sigmoid_tanh(x):
    return 0.5 * jnp.tanh(0.5 * x) + 0.5


def _merge_kernel(y5_ref, u5_ref, o1_ref, o2_ref, o3_ref, gate_ref, d5_ref, wglu_ref, wb_ref, wo_ref, x_ref, g1_ref,
                  nw_ref, sh_ref, sc_ref, rw_ref, xo_ref, h_ref, aff_ref):
    z = jax.nn.gelu(y5_ref[0] + d5_ref[...] * u5_ref[0].astype(F32))
    o_s5 = (z * _sigmoid_tanh(jnp.dot(z.astype(BF16), wglu_ref[...], preferred_element_type=F32))).astype(BF16)
    m = None
    for br in range(N_BRANCH):
        o_br = o_s5 if br == 0 else (o1_ref, o2_ref, o3_ref)[br - 1][0]
        proj = jnp.dot(o_br, wb_ref[br], preferred_element_type=F32)
        gate = _sigmoid_tanh(gate_ref[0, :, br * D_MODEL:(br + 1) * D_MODEL].astype(F32))
        m = gate * proj if m is None else m + gate * proj
    mix = jnp.dot(m.astype(BF16), wo_ref[...], preferred_element_type=F32)
    x_new = x_ref[0] + g1_ref[0] * mix
    xo_ref[0] = x_new
    h = _norm_mod(x_new, nw_ref[...], sh_ref[0], sc_ref[0])
    h_ref[0] = h.astype(BF16)
    logits = lax.dot_general(rw_ref[...], h, (((1,), (1,)), ((), ())),
                             precision=lax.Precision.HIGHEST, preferred_element_type=F32)
    logits = logits - jnp.max(logits, axis=0, keepdims=True)
    e = jnp.exp(logits)
    aff_ref[0] = e / jnp.sum(e, axis=0, keepdims=True)


def merge_and_route(y_s5, u_s5, outs, gates, s5_d, s5_glu_bf16, w_branch_bf16, w_out_bf16, x, g1,
                    norm_w, shift, scale, router_w_t, tm):
    b, n, d = x.shape
    tok = lambda bi, i: (bi, i, 0)
    per_b = lambda bi, i: (bi, 0, 0)
    const2 = lambda bi, i: (0, 0)
    const3 = lambda bi, i: (0, 0, 0)
    return pl.pallas_call(
        _merge_kernel,
        grid=(b, n // tm),
        in_specs=[pl.BlockSpec((1, tm, BRANCH_W), tok)] * (N_BRANCH + 1) + [
            pl.BlockSpec((1, tm, N_BRANCH * d), tok),
            pl.BlockSpec((1, BRANCH_W), const2),
            pl.BlockSpec((BRANCH_W, BRANCH_W), const2),
            pl.BlockSpec((N_BRANCH, BRANCH_W, d), const3),
            pl.BlockSpec((d, d), const2),
            pl.BlockSpec((1, tm, d), tok),
            pl.BlockSpec((1, 1, d), per_b),
            pl.BlockSpec((1, d), const2),
            pl.BlockSpec((1, 1, d), per_b),
            pl.BlockSpec((1, 1, d), per_b),
            pl.BlockSpec((N_EXPERTS, d), const2),
        ],
        out_specs=[
            pl.BlockSpec((1, tm, d), tok),
            pl.BlockSpec((1, tm, d), tok),
            pl.BlockSpec((1, N_EXPERTS, tm), lambda bi, i: (bi, 0, i)),
        ],
        out_shape=[
            jax.ShapeDtypeStruct((b, n, d), F32),
            jax.ShapeDtypeStruct((b, n, d), BF16),
            jax.ShapeDtypeStruct((b, N_EXPERTS, n), F32),
        ],
        compiler_params=_params("parallel", "parallel"),
        name="merge_route",
    )(y_s5, u_s5, *outs, gates, s5_d, s5_glu_bf16, w_branch_bf16, w_out_bf16, x, g1, norm_w, shift, scale,
      router_w_t)


MXU_N = 256


EXPERT_ROWS = 512
def _cast_kernel(w_ref, o_ref):
    o_ref[0] = w_ref[0, 0].astype(BF16)


def cast_layer_bf16(w, layer):
    _, e, r, c = w.shape
    return pl.pallas_call(
        _cast_kernel,
        grid=(e,),
        in_specs=[pl.BlockSpec((1, 1, r, c), lambda ei: (layer, ei, 0, 0))],
        out_specs=pl.BlockSpec((1, r, c), lambda ei: (ei, 0, 0)),
        out_shape=jax.ShapeDtypeStruct((e, r, c), BF16),
        compiler_params=_params("parallel"),
        name="cast_bf16",
    )(w)


def _expert_kernel(x_ref, g_ref, g2_ref, wg_ref, wu_ref, wd_ref, y_ref, act_ref):
    nb, cap, d = x_ref.shape[1:]
    xb = x_ref[0].reshape(nb * cap, d)
    f = wg_ref.shape[2]
    for c0 in range(0, f, MXU_N):
        c1 = min(c0 + MXU_N, f)
        gate = jnp.dot(xb, wg_ref[0, :, c0:c1], preferred_element_type=F32)
        up = jnp.dot(xb, wu_ref[0, :, c0:c1], preferred_element_type=F32)
        act_ref[:, c0:c1] = (jax.nn.silu(gate) * up).astype(BF16)
    y = jnp.dot(act_ref[...], wd_ref[0], preferred_element_type=F32).reshape(nb, cap, d)
    y_ref[0] = (y * g_ref[0] * g2_ref[...]).astype(y_ref.dtype)


def expert_ffn(xe, ge, g2, w_gate, w_up, w_down):
    e, b, cap, d = xe.shape
    f = w_gate.shape[-1]
    nb = max(1, min(b, EXPERT_ROWS // cap))
    tok = lambda ei, i: (ei, i, 0, 0)
    return pl.pallas_call(
        _expert_kernel,
        grid=(e, b // nb),
        in_specs=[
            pl.BlockSpec((1, nb, cap, d), tok),
            pl.BlockSpec((1, nb, cap, 1), tok),
            pl.BlockSpec((nb, 1, d), lambda ei, i: (i, 0, 0)),
            pl.BlockSpec((1, d, f), lambda ei, i: (ei, 0, 0)),
            pl.BlockSpec((1, d, f), lambda ei, i: (ei, 0, 0)),
            pl.BlockSpec((1, f, d), lambda ei, i: (ei, 0, 0)),
        ],
        out_specs=pl.BlockSpec((1, nb, cap, d), tok),
        out_shape=jax.ShapeDtypeStruct((e, b, cap, d), BF16),
        scratch_shapes=[pltpu.VMEM((nb * cap, f), BF16)],
        compiler_params=_params("parallel", "arbitrary"),
        name="expert_ffn",
    )(xe, ge, g2, w_gate, w_up, w_down)


COMBINE_GROUP = 8


def _combine_kernel(idx_ref, y_ref, x_hbm, o_ref, yf_ref, sem):
    bi = pl.program_id(0)
    e = pl.program_id(1)
    cap = y_ref.shape[2]

    @pl.when(e == 0)
    def _():
        cp = pltpu.make_async_copy(x_hbm.at[bi], o_ref.at[0], sem)
        cp.start()
        cp.wait()

    yf_ref[...] = y_ref[0, 0].astype(F32)

    def body(c, carry):
        base = pl.multiple_of(c * COMBINE_GROUP, COMBINE_GROUP)
        ys = yf_ref[pl.ds(base, COMBINE_GROUP), :]
        rows = [idx_ref[0, 0, e * cap + base + k] for k in range(COMBINE_GROUP)]
        new = [o_ref[0, pl.ds(rows[k], 1), :] + ys[k:k + 1, :] for k in range(COMBINE_GROUP)]
        for k in range(COMBINE_GROUP):
            o_ref[0, pl.ds(rows[k], 1), :] = new[k]
        return carry

    lax.fori_loop(0, cap // COMBINE_GROUP, body, 0)


def moe_combine(x, y, idx):
    b, n, d = x.shape
    e, _, cap, _ = y.shape
    return pl.pallas_call(
        _combine_kernel,
        grid=(b, e),
        in_specs=[
            pl.BlockSpec((1, 1, e * cap), lambda bi, ei: (bi, 0, 0), memory_space=pltpu.SMEM),
            pl.BlockSpec((1, 1, cap, d), lambda bi, ei: (ei, bi, 0, 0)),
            pl.BlockSpec(memory_space=pl.ANY),
        ],
        out_specs=pl.BlockSpec((1, n, d), lambda bi, ei: (bi, 0, 0)),
        out_shape=jax.ShapeDtypeStruct((b, n, d), F32),
        scratch_shapes=[pltpu.VMEM((cap, d), F32), pltpu.SemaphoreType.DMA(())],
        compiler_params=_params("parallel", "arbitrary"),
        name="moe_combine",
    )(idx, y, x)


def expert_choice_ffn(x, g2, h_bf16, aff_t, w_gate, w_up, w_down):
    b, n, d = h_bf16.shape
    cap = EC_CAPACITY * n // N_EXPERTS
    g, idx = lax.top_k(aff_t, cap)
    idx_e = jnp.moveaxis(idx, 1, 0)
    g_e = jnp.moveaxis(g, 1, 0)[..., None]
    xe = h_bf16[jnp.arange(b)[None, :, None], idx_e]
    y = expert_ffn(xe, g_e, g2, w_gate, w_up, w_down)
    return moe_combine(x, y, idx.reshape(b, 1, N_EXPERTS * cap).astype(jnp.int32))


S5_BLK = 16
S5_ROW = S5_GROUPS * S5_BLK * S5_GROUP
S5_PAIRS = S5_GROUPS // 2
S5_LANES = S5_GROUPS * S5_STATE


def _pair_blockdiag(t):
    g, r, c = t.shape
    t = t.reshape(g // 2, 2, r, c)
    z = jnp.zeros_like(t[:, 0])
    top = jnp.concatenate([t[:, 0], z], axis=-1)
    bot = jnp.concatenate([z, t[:, 1]], axis=-1)
    return jnp.concatenate([top, bot], axis=-2)


def s5_operators(lam_re, lam_im, log_step, b_re, b_im, c_re, c_im):
    hp = lax.Precision.HIGHEST
    blk = S5_BLK
    dt = jnp.exp(log_step)[..., None]
    k = jnp.arange(blk + 1, dtype=F32)
    mag = jnp.exp((lam_re * dt)[..., None] * k)
    ang = (lam_im * dt)[..., None] * k
    pr, pi = mag * jnp.cos(ang), mag * jnp.sin(ang)
    ar, ai = pr[..., 1], pi[..., 1]
    den = lam_re * lam_re + lam_im * lam_im
    zr = ((ar - 1.0) * lam_re + ai * lam_im) / den
    zi = (ai * lam_re - (ar - 1.0) * lam_im) / den
    bb_re = zr[..., None] * b_re - zi[..., None] * b_im
    bb_im = zr[..., None] * b_im + zi[..., None] * b_re
    ca_re = c_re[..., None] * pr[:, :, None] - c_im[..., None] * pi[:, :, None]
    ca_im = c_re[..., None] * pi[:, :, None] + c_im[..., None] * pr[:, :, None]
    kern = (jnp.einsum('dgpnl,dgnq->dglpq', ca_re, bb_re, precision=hp)
            - jnp.einsum('dgpnl,dgnq->dglpq', ca_im, bb_im, precision=hp))
    j = np.arange(blk)[:, None]
    i = np.arange(blk)[None, :]
    ms, ws, rres, rims = [], [], [], []
    for d in range(2):
        lag = (i - j) if d == 0 else (j - i)
        valid = jnp.asarray(lag >= 0, F32)[None, :, None, :, None]
        kd = kern[d][:, np.clip(lag, 0, blk - 1)]
        m = jnp.transpose(kd, (0, 1, 4, 2, 3)) * valid
        ms.append(m.reshape(S5_GROUPS, blk * S5_GROUP, blk * S5_GROUP))
        pw = (blk - 1 - np.arange(blk)) if d == 0 else np.arange(blk)
        apr, api = pr[d][..., pw], pi[d][..., pw]
        w_re = apr[..., None] * bb_re[d][:, :, None] - api[..., None] * bb_im[d][:, :, None]
        w_im = apr[..., None] * bb_im[d][:, :, None] + api[..., None] * bb_re[d][:, :, None]
        to_w = lambda t: jnp.transpose(t, (0, 2, 3, 1)).reshape(S5_GROUPS, blk * S5_GROUP, S5_STATE)
        ws.append((_pair_blockdiag(to_w(w_re)), _pair_blockdiag(to_w(w_im))))
        ex = (np.arange(blk) + 1) if d == 0 else (blk - np.arange(blk))
        r_re = ca_re[d][..., ex]
        r_im = -ca_im[d][..., ex]
        to_r = lambda t: jnp.transpose(t, (0, 2, 3, 1)).reshape(S5_GROUPS, S5_STATE, blk * S5_GROUP)
        rres.append(_pair_blockdiag(to_r(r_re)))
        rims.append(_pair_blockdiag(to_r(r_im)))
    m_op = jnp.stack(ms).astype(BF16)
    w_re = jnp.stack([w[0] for w in ws]).astype(BF16)
    w_im = jnp.stack([w[1] for w in ws]).astype(BF16)
    r_re = jnp.stack(rres).astype(BF16)
    r_im = jnp.stack(rims).astype(BF16)
    a_blk = jnp.stack([pr[..., blk].reshape(2, 1, S5_LANES), pi[..., blk].reshape(2, 1, S5_LANES)], axis=1)
    return m_op, w_re, w_im, r_re, r_im, a_blk


def _s5_kernel(uc_ref, ul_ref, m_ref, wre_ref, wim_ref, rre_ref, rim_ref, a_ref, yc_ref, yl_ref,
               vre, vim, sre, sim):
    rc, rl = uc_ref.shape[1], ul_ref.shape[1]
    segs = ((uc_ref, yc_ref, 0, rc), (ul_ref, yl_ref, rc, rl))
    gw = S5_BLK * S5_GROUP
    for d in range(2):
        for u_ref, _, base, rows in segs:
            for h in range(S5_PAIRS):
                u_pair = u_ref[0, :, 2 * h * gw:2 * (h + 1) * gw]
                vre[base:base + rows, h * 128:(h + 1) * 128] = jnp.dot(
                    u_pair, wre_ref[d, h], preferred_element_type=F32)
                vim[base:base + rows, h * 128:(h + 1) * 128] = jnp.dot(
                    u_pair, wim_ref[d, h], preferred_element_type=F32)
        ar, ai = a_ref[d, 0], a_ref[d, 1]

        def run(base, rows, carry):
            def step(t, c):
                xr, xi = c
                idx = base + (t if d == 0 else rows - 1 - t)
                sre[pl.ds(idx, 1), :] = xr
                sim[pl.ds(idx, 1), :] = xi
                nr = ar * xr - ai * xi + vre[pl.ds(idx, 1), :]
                ni = ar * xi + ai * xr + vim[pl.ds(idx, 1), :]
                return nr, ni
            return lax.fori_loop(0, rows, step, carry)

        zero = jnp.zeros((1, S5_LANES), F32)
        carry = run(0, rc, (zero, zero))
        run(rc, rl, carry)
        for u_ref, y_ref, base, rows in segs:
            for h in range(S5_PAIRS):
                s_r = sre[base:base + rows, h * 128:(h + 1) * 128].astype(BF16)
                s_i = sim[base:base + rows, h * 128:(h + 1) * 128].astype(BF16)
                y = (jnp.dot(s_r, rre_ref[d, h], preferred_element_type=F32)
                     + jnp.dot(s_i, rim_ref[d, h], preferred_element_type=F32))
                for gl in range(2):
                    g = 2 * h + gl
                    yg = y[:, gl * gw:(gl + 1) * gw] + jnp.dot(
                        u_ref[0, :, g * gw:(g + 1) * gw], m_ref[d, g], preferred_element_type=F32)
                    if d == 0:
                        y_ref[0, :, g * gw:(g + 1) * gw] = yg
                    else:
                        y_ref[0, :, g * gw:(g + 1) * gw] += yg


def _to_s5_rows(u):
    b, n, _ = u.shape
    u = u.reshape(b, n // S5_BLK, S5_BLK, S5_GROUPS, S5_GROUP)
    return jnp.transpose(u, (0, 1, 3, 2, 4)).reshape(b, n // S5_BLK, S5_ROW)


def _from_s5_rows(y):
    b, r, _ = y.shape
    y = y.reshape(b, r, S5_GROUPS, S5_BLK, S5_GROUP)
    return jnp.transpose(y, (0, 1, 3, 2, 4)).reshape(b, r * S5_BLK, BRANCH_W)


def s5_scan_readout(u_ctx, u_lat, ops):
    m_op, w_re, w_im, r_re, r_im, a_blk = ops
    b = u_lat.shape[0]
    uc, ul = _to_s5_rows(u_ctx), _to_s5_rows(u_lat)
    rc, rl = uc.shape[1], ul.shape[1]
    per_b = lambda bi: (bi, 0, 0)
    c4 = lambda bi: (0, 0, 0, 0)
    yc, yl = pl.pallas_call(
        _s5_kernel,
        grid=(b,),
        in_specs=[
            pl.BlockSpec((1, rc, S5_ROW), per_b),
            pl.BlockSpec((1, rl, S5_ROW), per_b),
            pl.BlockSpec(m_op.shape, c4),
            pl.BlockSpec(w_re.shape, c4),
            pl.BlockSpec(w_im.shape, c4),
            pl.BlockSpec(r_re.shape, c4),
            pl.BlockSpec(r_im.shape, c4),
            pl.BlockSpec(a_blk.shape, c4),
        ],
        out_specs=[pl.BlockSpec((1, rc, S5_ROW), per_b), pl.BlockSpec((1, rl, S5_ROW), per_b)],
        out_shape=[jax.ShapeDtypeStruct((b, rc, S5_ROW), F32), jax.ShapeDtypeStruct((b, rl, S5_ROW), F32)],
        scratch_shapes=[pltpu.VMEM((rc + rl, S5_LANES), F32) for _ in range(4)],
        compiler_params=_params("parallel"),
        name="s5_scan",
    )(uc, ul, m_op, w_re, w_im, r_re, r_im, a_blk)
    return _from_s5_rows(yc), _from_s5_rows(yl)


LA_TILE = 128
LA_UNROLL = 2


def _stack_heads(q):
    return jnp.concatenate(
        [jnp.where(_head_lane_mask(h, q.shape), q, jnp.zeros_like(q)) for h in range(N_HEADS)], axis=0)


def _unstack_heads(o_stack, t):
    out = jnp.zeros((t, o_stack.shape[1]), o_stack.dtype)
    for h in range(N_HEADS):
        blk = o_stack[h * t:(h + 1) * t]
        out = jnp.where(_head_lane_mask(h, blk.shape), blk, out)
    return out


_NT = (((1,), (1,)), ((), ()))


def _state_update(st_ref, decay_lane, v, k_scaled, head_avg):
    vt = jnp.transpose(v.astype(F32)).astype(BF16)
    kv = jnp.dot(vt, k_scaled.astype(BF16), preferred_element_type=F32)
    st_ref[...] = decay_lane * st_ref[...] + jnp.where(head_avg > 0, kv, 0.0)


def _for_tiles(n_tiles, reverse, body):
    def step(i, carry):
        body(n_tiles - 1 - i if reverse else i)
        return carry
    lax.fori_loop(0, n_tiles, step, 0, unroll=LA_UNROLL)


def _ret_kernel(qc_ref, kc_ref, vc_ref, gc_ref, ql_ref, kl_ref, vl_ref, gl_ref, dmask_ref, xi_ref, zeta_ref,
                gam_ref, havg_ref, oc_ref, ol_ref, st_ref, accc_ref, accl_ref):
    t = LA_TILE
    segs = ((qc_ref, kc_ref, vc_ref, accc_ref), (ql_ref, kl_ref, vl_ref, accl_ref))
    for d in range(2):
        st_ref[...] = jnp.zeros_like(st_ref)
        for q_ref, k_ref, v_ref, acc_ref in segs:
            def tile(i, q_ref=q_ref, k_ref=k_ref, v_ref=v_ref, acc_ref=acc_ref):
                rows = pl.ds(pl.multiple_of(i * t, t), t)
                q, k, v = q_ref[0, rows, :], k_ref[0, rows, :], v_ref[0, rows, :]
                att = lax.dot_general(_stack_heads(q), k, _NT, preferred_element_type=F32) * dmask_ref[d]
                intra = _unstack_heads(jnp.dot(att.astype(BF16), v, preferred_element_type=F32), t)
                cross = lax.dot_general(q, st_ref[...].astype(BF16), _NT, preferred_element_type=F32)
                o = intra + cross * xi_ref[d]
                if d == 0:
                    acc_ref[rows, :] = o
                else:
                    acc_ref[rows, :] += o
                _state_update(st_ref, gam_ref[d], v, k.astype(F32) * zeta_ref[d], havg_ref[...])
            _for_tiles(q_ref.shape[1] // t, d == 1, tile)
    for acc_ref, g_ref, o_ref in ((accc_ref, gc_ref, oc_ref), (accl_ref, gl_ref, ol_ref)):
        o = acc_ref[...]
        o = o * lax.rsqrt(_head_mean_sq(o, havg_ref[...]) + EPS)
        o_ref[0] = (o * jax.nn.silu(g_ref[0].astype(F32))).astype(o_ref.dtype)


def retention_tables(decay_logit):
    t = LA_TILE
    lg = jax.nn.log_sigmoid(decay_logit)
    idx = jnp.arange(t, dtype=F32)
    diff = idx[:, None] - idx[None, :]
    lgm = lg[:, :, None, None]
    fwd = jnp.where(diff >= 0, jnp.exp(jnp.maximum(diff, 0.0) * lgm[0]), 0.0)
    bwd = jnp.where(diff <= 0, jnp.exp(jnp.maximum(-diff, 0.0) * lgm[1]), 0.0)
    dmask = jnp.stack([fwd.reshape(N_HEADS * t, t), bwd.reshape(N_HEADS * t, t)])
    lane = lambda a: jnp.repeat(a, HEAD_DIM, axis=-1)
    lg_l = lane(lg)[:, None, :]
    steps_q = jnp.stack([idx + 1.0, t - idx])[:, :, None]
    steps_k = jnp.stack([t - 1.0 - idx, idx])[:, :, None]
    xi = jnp.exp(steps_q * lg_l)
    zeta = jnp.exp(steps_k * lg_l)
    gam = jnp.exp(t * lg_l)
    return dmask, xi, zeta, gam


def retention_pallas(rt_ctx, rt_lat, decay_logit):
    b, l, _ = rt_lat.shape
    lc = rt_ctx.shape[1]
    dmask, xi, zeta, gam = retention_tables(decay_logit)
    col = lambda n, c: pl.BlockSpec((1, n, BRANCH_W), lambda bi: (bi, 0, c))
    const = lambda a: pl.BlockSpec(a.shape, lambda bi: (0,) * a.ndim)
    havg = head_avg_matrix()
    return pl.pallas_call(
        _ret_kernel,
        grid=(b,),
        in_specs=[col(lc, c) for c in range(4)] + [col(l, c) for c in range(4)]
        + [const(dmask), const(xi), const(zeta), const(gam), const(havg)],
        out_specs=[col(lc, 0), col(l, 0)],
        out_shape=[jax.ShapeDtypeStruct((b, lc, BRANCH_W), BF16), jax.ShapeDtypeStruct((b, l, BRANCH_W), BF16)],
        scratch_shapes=[pltpu.VMEM((BRANCH_W, BRANCH_W), F32), pltpu.VMEM((lc, BRANCH_W), F32),
                        pltpu.VMEM((l, BRANCH_W), F32)],
        compiler_params=_params("parallel"),
        name="retention",
    )(rt_ctx, rt_ctx, rt_ctx, rt_ctx, rt_lat, rt_lat, rt_lat, rt_lat, dmask, xi, zeta, gam, havg)


HG_LEVELS = (32, 64, 128)


def hg_level_masks():
    t = LA_TILE
    i = (np.arange(N_HEADS * t) % t)[:, None]
    j = np.arange(t)[None, :]
    out = []
    for d in range(2):
        masks = [(i // HG_CHUNK == j // HG_CHUNK) & ((j <= i) if d == 0 else (j >= i))]
        for blk in HG_LEVELS:
            qi_late = (i % blk) >= blk // 2
            kj_late = (j % blk) >= blk // 2
            cross = (qi_late & ~kj_late) if d == 0 else (~qi_late & kj_late)
            masks.append((i // blk == j // blk) & cross)
        out.append(np.stack(masks))
    return jnp.asarray(np.stack(out), F32)


def _exact_rows_sum(sel, x):
    hi = x.astype(BF16)
    r1 = x - hi.astype(F32)
    mid = r1.astype(BF16)
    lo = (r1 - mid.astype(F32)).astype(BF16)
    return (jnp.dot(sel, hi, preferred_element_type=F32) + jnp.dot(sel, mid, preferred_element_type=F32)
            + jnp.dot(sel, lo, preferred_element_type=F32))


def _hg_tile(d, q, k, v, logf, st_ref, tri, mask_ref):
    t = LA_TILE
    w = q.shape[1]
    g = _exact_rows_sum(tri, logf)
    nb = t // HG_CHUNK
    g3 = g.reshape(nb, HG_CHUNK, w)
    if d == 0:
        edge = g3[:, HG_CHUNK - 1:HG_CHUNK, :]
        prev = jnp.concatenate([jnp.zeros((1, 1, w), F32), edge[:-1]], axis=0)
    else:
        edge = g3[:, 0:1, :]
        prev = jnp.concatenate([edge[1:], jnp.zeros((1, 1, w), F32)], axis=0)
    cum = (g3 - prev).reshape(t, w)
    qs = [q * jnp.exp(cum)]
    ks = [k * jnp.exp(-cum)]
    for blk in HG_LEVELS:
        gb = g.reshape(t // blk, blk, w)
        row = blk // 2 - 1 if d == 0 else blk // 2
        mid = jnp.broadcast_to(gb[:, row:row + 1, :], gb.shape).reshape(t, w)
        decay_to_mid = jnp.exp(-jnp.abs(g - mid))
        qs.append(q * decay_to_mid)
        ks.append(k * decay_to_mid)
    att = None
    for lvl, (qq, kk) in enumerate(zip(qs, ks)):
        a = lax.dot_general(_stack_heads(qq.astype(BF16)), kk.astype(BF16), _NT, preferred_element_type=F32)
        a = jnp.where(mask_ref[d, lvl] > 0.5, a, 0.0)
        att = a if att is None else att + a
    intra = _unstack_heads(jnp.dot(att.astype(BF16), v, preferred_element_type=F32), t)
    cross = lax.dot_general((q * jnp.exp(g)).astype(BF16), st_ref[...].astype(BF16), _NT,
                            preferred_element_type=F32)
    total = g[t - 1:t, :] if d == 0 else g[0:1, :]
    return intra + cross, jnp.exp(total), k * jnp.exp(total - g)


def _hg_kernel(pc_ref, pl_ref, lb_ref, nw_ref, havg_ref, mask_ref, oc_ref, ol_ref, st_ref, accc_ref, accl_ref):
    t = LA_TILE
    w = BRANCH_W
    lb = lb_ref[...]
    r = lax.broadcasted_iota(jnp.int32, (t, t), 0)
    c = lax.broadcasted_iota(jnp.int32, (t, t), 1)
    for d in range(2):
        tri = jnp.where((c <= r) if d == 0 else (c >= r), 1.0, 0.0).astype(BF16)
        st_ref[...] = jnp.zeros_like(st_ref)
        for p_ref, acc_ref in ((pc_ref, accc_ref), (pl_ref, accl_ref)):
            def tile(i, p_ref=p_ref, acc_ref=acc_ref):
                rows = pl.ds(pl.multiple_of(i * t, t), t)
                q = jax.nn.silu(p_ref[0, rows, 0:w].astype(F32))
                f_logit = p_ref[0, rows, (1 + d) * w:(2 + d) * w].astype(F32)
                v = p_ref[0, rows, 3 * w:4 * w]
                fg = lb + (1.0 - lb) * jax.nn.sigmoid(f_logit)
                o, decay, k_end = _hg_tile(d, q, 1.0 - fg, v, jnp.log(fg), st_ref, tri, mask_ref)
                if d == 0:
                    acc_ref[rows, :] = o
                else:
                    acc_ref[rows, :] += o
                _state_update(st_ref, decay, v, k_end, havg_ref[...])
            _for_tiles(p_ref.shape[1] // t, d == 1, tile)
    for acc_ref, p_ref, o_ref in ((accc_ref, pc_ref, oc_ref), (accl_ref, pl_ref, ol_ref)):
        o = acc_ref[...]
        o = o * lax.rsqrt(_head_mean_sq(o, havg_ref[...]) + EPS) * nw_ref[...]
        o_ref[0] = (o * jax.nn.silu(p_ref[0, :, 4 * w:5 * w].astype(F32))).astype(o_ref.dtype)


def hgrn2_pallas(hg_ctx, hg_lat, lower_bound, norm_w):
    b, l, width = hg_lat.shape
    lc = hg_ctx.shape[1]
    full = lambda n: pl.BlockSpec((1, n, width), lambda bi: (bi, 0, 0))
    out = lambda n: pl.BlockSpec((1, n, BRANCH_W), lambda bi: (bi, 0, 0))
    vec = pl.BlockSpec((1, BRANCH_W), lambda bi: (0, 0))
    havg = head_avg_matrix()
    masks = hg_level_masks()
    return pl.pallas_call(
        _hg_kernel,
        grid=(b,),
        in_specs=[full(lc), full(l), vec, vec, pl.BlockSpec(havg.shape, lambda bi: (0, 0)),
                  pl.BlockSpec(masks.shape, lambda bi: (0, 0, 0, 0))],
        out_specs=[out(lc), out(l)],
        out_shape=[jax.ShapeDtypeStruct((b, lc, BRANCH_W), BF16), jax.ShapeDtypeStruct((b, l, BRANCH_W), BF16)],
        scratch_shapes=[pltpu.VMEM((BRANCH_W, BRANCH_W), F32), pltpu.VMEM((lc, BRANCH_W), F32),
                        pltpu.VMEM((l, BRANCH_W), F32)],
        compiler_params=_params("parallel"),
        name="hgrn2",
    )(hg_ctx, hg_lat, lower_bound[None], jnp.tile(norm_w, N_HEADS)[None], havg, masks)


def kernel(x, c, ctx, c_ctx, ada_w, ada_b, norm_mix_w, norm_ffn_w, w_in, s5_lam_re, s5_lam_im, s5_log_step,
           s5_b_re, s5_b_im, s5_c_re, s5_c_im, s5_d, s5_glu_w, na_q_norm, na_k_norm, na_rpb, hg_lower_bounds,
           hg_norm_w, ret_decay_logit, w_branch, w_out, router_w, ex_w_gate, ex_w_up, ex_w_down):
    b = x.shape[0]
    lb_p = jax.nn.softmax(hg_lower_bounds, axis=0)
    lower_bounds = jnp.cumsum(lb_p, axis=0) - lb_p[0]
    cond_rows = jnp.concatenate([c, jnp.broadcast_to(c_ctx[None], c.shape)], axis=0)
    mods = ada_modulation(cond_rows, ada_w, ada_b, D_MODEL)
    xc = ctx
    for li in range(DEPTH):
        last = li == DEPTH - 1
        mod_l = [m[:, None] for m in jnp.split(mods[li, :b], 6, axis=-1)]
        mod_c = [m[:, None] for m in jnp.split(mods[li, b:], 6, axis=-1)]
        sh1_l, sc1_l, g1_l, sh2_l, sc2_l, g2_l = mod_l
        sh1_c, sc1_c, g1_c, sh2_c, sc2_c, g2_c = mod_c
        w_in_b = w_in[li].astype(BF16)
        wb_b = w_branch[li].astype(BF16)
        wo_b = w_out[li].astype(BF16)
        rw_t = router_w[li].T
        nmw = norm_mix_w[li][None]
        nfw = norm_ffn_w[li][None]

        rt_off = sum(IN_SPLITS[:3])
        w_rot_b = jnp.concatenate(
            [_swap_head_halves_cols(w_in[li][:, rt_off + i * BRANCH_W:rt_off + (i + 1) * BRANCH_W]) for i in range(2)],
            axis=1).astype(BF16)
        qk_w = jnp.stack([jnp.tile(na_q_norm[li], N_HEADS), jnp.tile(na_k_norm[li], N_HEADS)])
        s5_ops = s5_operators(s5_lam_re[li], s5_lam_im[li], s5_log_step[li], s5_b_re[li], s5_b_im[li],
                              s5_c_re[li], s5_c_im[li])
        s5_dl = s5_d[li][None]
        glu_b = s5_glu_w[li].astype(BF16)

        pl_ = in_proj(x, nmw, sh1_l, sc1_l, w_in_b, w_rot_b, qk_w, True, 512)
        pc_ = in_proj(xc, nmw, sh1_c, sc1_c, w_in_b, w_rot_b, qk_w, False, 256)
        y5_c, y5_l = s5_scan_readout(pc_[0], pl_[0], s5_ops)
        na_l = na_latent(pl_[1], pc_[1], na_bias(na_rpb[li], x.shape[1] // GRID_W))
        hg_c, hg_l = hgrn2_pallas(pc_[2], pl_[2], lower_bounds[li], hg_norm_w[li])
        rt_c, rt_l = retention_pallas(pc_[3], pl_[3], ret_decay_logit[li])

        x, h_l, aff_l = merge_and_route(y5_l, pl_[0], (na_l, hg_l, rt_l), pl_[4], s5_dl, glu_b, wb_b, wo_b, x, g1_l,
                                        nfw, sh2_l, sc2_l, rw_t, 512)
        ex_w = tuple(cast_layer_bf16(w, li) for w in (ex_w_gate, ex_w_up, ex_w_down))
        x = expert_choice_ffn(x, g2_l, h_l, aff_l, *ex_w)
        if not last:
            na_c = na_context(pc_[1])
            xc, h_c, aff_c = merge_and_route(y5_c, pc_[0], (na_c, hg_c, rt_c), pc_[4], s5_dl, glu_b, wb_b, wo_b, xc,
                                             g1_c, nfw, sh2_c, sc2_c, rw_t, 256)
            xc = expert_choice_ffn(xc, g2_c, h_c, aff_c, *ex_w)
    return x
```

```python
import functools
import math

import jax
import jax.numpy as jnp
import numpy as np
from jax import lax
from jax.experimental import pallas as pl
from jax.experimental.pallas import tpu as pltpu

D_MODEL = 1024
DEPTH = 2
GRID_W = 64
N_BRANCH = 4
BRANCH_W = 256
HEAD_DIM = 64
N_HEADS = BRANCH_W // HEAD_DIM
S5_GROUP = 16
S5_GROUPS = BRANCH_W // S5_GROUP
S5_STATE = 64
NA_ROWS = 8
NA_COLS = 16
HG_CHUNK = 16
RET_CHUNK = 128
N_EXPERTS = 16
EC_CAPACITY = 2
D_EXPERT = 2816
ROPE_BASE = 10000.0
EPS = 1e-6
IN_SPLITS = (BRANCH_W, 3 * BRANCH_W, 5 * BRANCH_W, 4 * BRANCH_W, N_BRANCH * D_MODEL)
D_IN = sum(IN_SPLITS)

F32 = jnp.float32
BF16 = jnp.bfloat16

V7X_VMEM_BYTES = 64 * 1024 * 1024
VMEM_LIMIT = V7X_VMEM_BYTES - 8 * 1024 * 1024


def _params(*sem):
    return pltpu.CompilerParams(dimension_semantics=sem, vmem_limit_bytes=VMEM_LIMIT)


def _norm_mod(x, norm_w, shift, scale):
    y = x * lax.rsqrt(jnp.mean(x * x, axis=-1, keepdims=True) + EPS) * norm_w
    return y * (1.0 + scale) + shift


def _ada_kernel(c_ref, w_ref, b_ref, o_ref):
    cond = jax.nn.silu(c_ref[...])
    o_ref[0] = jnp.dot(cond, w_ref[0], precision=lax.Precision.HIGHEST, preferred_element_type=F32) + b_ref[0]


def ada_modulation(cond_in, ada_w, ada_b, tn):
    r, d = cond_in.shape
    depth, _, n = ada_w.shape
    return pl.pallas_call(
        _ada_kernel,
        grid=(depth, n // tn),
        in_specs=[
            pl.BlockSpec((r, d), lambda l, j: (0, 0)),
            pl.BlockSpec((1, d, tn), lambda l, j: (l, 0, j)),
            pl.BlockSpec((1, 1, tn), lambda l, j: (l, 0, j)),
        ],
        out_specs=pl.BlockSpec((1, r, tn), lambda l, j: (l, 0, j)),
        out_shape=jax.ShapeDtypeStruct((depth, r, n), F32),
        compiler_params=_params("parallel", "parallel"),
        name="ada_modulation",
    )(cond_in, ada_w, ada_b[:, None, :])


def _head_mean_sq(t, head_avg):
    sq = t * t
    hi = sq.astype(BF16)
    lo = (sq - hi.astype(F32)).astype(BF16)
    return (jnp.dot(hi, head_avg, preferred_element_type=F32)
            + jnp.dot(lo, head_avg, preferred_element_type=F32))


def _in_proj_kernel(x_ref, nw_ref, sh_ref, sc_ref, w_ref, wrot_ref, qkw_ref, havg_ref, cos_ref, sin_ref,
                    *refs, rope, n_cast):
    cast_in, (s5_ref, na_ref, hg_ref, rt_ref, gate_ref), cast_out = (
        refs[:n_cast], refs[n_cast:n_cast + 5], refs[n_cast + 5:])
    for src, dst in zip(cast_in, cast_out):
        dst[0] = src[0, 0].astype(BF16)
    hb = _norm_mod(x_ref[0], nw_ref[...], sh_ref[0], sc_ref[0]).astype(BF16)
    cw = BRANCH_W
    proj = lambda col: jnp.dot(hb, w_ref[:, col * cw:(col + 1) * cw], preferred_element_type=F32)
    col = 0
    s5_ref[0] = proj(col).astype(BF16)
    col += 1
    for part in range(3):
        t = proj(col + part)
        if part < 2:
            t = t * lax.rsqrt(_head_mean_sq(t, havg_ref[...]) + EPS) * qkw_ref[part:part + 1, :]
        if part == 0:
            t = t * (HEAD_DIM ** -0.5)
        na_ref[0, :, part * cw:(part + 1) * cw] = t.astype(BF16)
    col += 3
    for part in range(5):
        hg_ref[0, :, part * cw:(part + 1) * cw] = proj(col + part).astype(BF16)
    col += 5
    for part in range(4):
        t = proj(col + part)
        if part < 2 and rope:
            swapped = jnp.dot(hb, wrot_ref[:, part * cw:(part + 1) * cw], preferred_element_type=F32)
            t = t * cos_ref[...] + swapped * sin_ref[...]
        if part == 1:
            t = t * (HEAD_DIM ** -0.5)
        rt_ref[0, :, part * cw:(part + 1) * cw] = t.astype(BF16)
    col += 4
    for part in range(N_BRANCH * D_MODEL // cw):
        gate_ref[0, :, part * cw:(part + 1) * cw] = proj(col + part).astype(BF16)


def rope_tables(n_tokens):
    quarter = HEAD_DIM // 4
    t = jnp.arange(n_tokens, dtype=jnp.int32)
    inv = jnp.asarray(ROPE_BASE ** (-np.arange(quarter) / quarter), F32)
    ang = jnp.concatenate([(t // GRID_W).astype(F32)[:, None] * inv, (t % GRID_W).astype(F32)[:, None] * inv], axis=1)
    cos, sin = jnp.cos(ang), jnp.sin(ang)
    cos_h = jnp.concatenate([cos, cos], axis=1)
    sin_h = jnp.concatenate([-sin, sin], axis=1)
    return jnp.tile(cos_h, (1, N_HEADS)), jnp.tile(sin_h, (1, N_HEADS))


def _swap_head_halves_cols(w):
    d, c = w.shape
    w = w.reshape(d, c // HEAD_DIM, 2, HEAD_DIM // 2)
    return w[:, :, ::-1, :].reshape(d, c)


def head_avg_matrix():
    h = np.arange(BRANCH_W) // HEAD_DIM
    return jnp.asarray((h[:, None] == h[None, :]) / HEAD_DIM, BF16)


def in_proj(x, norm_w, shift, scale, w_in_bf16, w_rot_bf16, qk_norm_w, rope, tm, cast_stacks=(), layer=0):
    b, n, d = x.shape
    nt = n // tm
    tok = lambda bi, i: (bi, i, 0)
    per_b = lambda bi, i: (bi, 0, 0)
    const2 = lambda bi, i: (0, 0)
    cos, sin = rope_tables(n)
    cast_in_specs, cast_out_specs, cast_out_shapes = [], [], []
    for w in cast_stacks:
        _, e, r, c = w.shape
        per_expert = (b * nt) // e
        rows = r // per_expert
        assert per_expert * e == b * nt and rows * per_expert == r and rows % 16 == 0, (w.shape, b, nt)
        cast_in_specs.append(pl.BlockSpec(
            (1, 1, rows, c), lambda bi, i, pe=per_expert: (layer, (bi * nt + i) // pe, (bi * nt + i) % pe, 0)))
        cast_out_specs.append(pl.BlockSpec(
            (1, rows, c), lambda bi, i, pe=per_expert: ((bi * nt + i) // pe, (bi * nt + i) % pe, 0)))
        cast_out_shapes.append(jax.ShapeDtypeStruct((e, r, c), BF16))
    return pl.pallas_call(
        functools.partial(_in_proj_kernel, rope=rope, n_cast=len(cast_stacks)),
        grid=(b, nt),
        in_specs=[
            pl.BlockSpec((1, tm, d), tok),
            pl.BlockSpec((1, d), const2),
            pl.BlockSpec((1, 1, d), per_b),
            pl.BlockSpec((1, 1, d), per_b),
            pl.BlockSpec((d, D_IN), const2, pipeline_mode=pl.Buffered(1)),
            pl.BlockSpec((d, 2 * BRANCH_W), const2, pipeline_mode=pl.Buffered(1)),
            pl.BlockSpec((2, BRANCH_W), const2),
            pl.BlockSpec((BRANCH_W, BRANCH_W), const2),
            pl.BlockSpec((tm, BRANCH_W), lambda bi, i: (i, 0)),
            pl.BlockSpec((tm, BRANCH_W), lambda bi, i: (i, 0)),
        ] + cast_in_specs,
        out_specs=[pl.BlockSpec((1, tm, w), tok) for w in IN_SPLITS] + cast_out_specs,
        out_shape=[jax.ShapeDtypeStruct((b, n, w), BF16) for w in IN_SPLITS] + cast_out_shapes,
        compiler_params=_params("parallel", "parallel"),
        name="in_proj",
    )(x, norm_w, shift, scale, w_in_bf16, w_rot_bf16, qk_norm_w, head_avg_matrix(), cos, sin, *cast_stacks)


NA_QROWS = 8
NA_KROWS = 16
NA_TQ = NA_QROWS * GRID_W
NA_TK = NA_KROWS * GRID_W
NA_MASKED = -1e30


def na_bias(rpb, rows):
    hp = lax.Precision.HIGHEST
    nblk = rows // NA_QROWS
    n_dr, n_dc = 2 * NA_ROWS - 1, 2 * NA_COLS - 1
    qc = np.arange(GRID_W)[:, None]
    kc = np.arange(GRID_W)[None, :]
    cs = np.clip(qc - NA_COLS // 2, 0, GRID_W - NA_COLS)
    col_ok = (kc >= cs) & (kc < cs + NA_COLS)
    dc = np.clip(kc - qc + NA_COLS - 1, 0, n_dc - 1).reshape(-1)
    sel_dc = jnp.asarray(dc[None, :] == np.arange(n_dc)[:, None], F32)
    by_col = jnp.einsum('hrc,cx->hrx', rpb.astype(F32), sel_dc, precision=hp)
    pats = []
    for g in (0, 1, nblk - 1):
        ks = int(np.clip(NA_QROWS * g - NA_ROWS // 2, 0, rows - NA_KROWS))
        qr = (NA_QROWS * g + np.arange(NA_QROWS))[:, None]
        kr = (ks + np.arange(NA_KROWS))[None, :]
        band = np.clip(qr - NA_ROWS // 2, 0, rows - NA_ROWS)
        row_ok = (kr >= band) & (kr < band + NA_ROWS)
        dr = np.clip(kr - qr + NA_ROWS - 1, 0, n_dr - 1).reshape(-1)
        sel_dr = jnp.asarray(dr[:, None] == np.arange(n_dr)[None, :], F32)
        t = jnp.einsum('yr,hrx->hyx', sel_dr, by_col, precision=hp)
        t = t.reshape(N_HEADS, NA_QROWS, NA_KROWS, GRID_W, GRID_W)
        t = jnp.transpose(t, (0, 1, 3, 2, 4)).reshape(N_HEADS, NA_TQ, NA_TK)
        ok = (row_ok[:, None, :, None] & col_ok[None, :, None, :]).reshape(NA_TQ, NA_TK)
        pats.append(jnp.where(jnp.asarray(ok)[None], t, NA_MASKED))
    return jnp.stack(pats)


def _head_lane_mask(h, shape):
    lane = lax.broadcasted_iota(jnp.int32, shape, len(shape) - 1)
    return (lane >= h * HEAD_DIM) & (lane < (h + 1) * HEAD_DIM)


def _na_kernel(q_ref, k_ref, v_ref, kc_ref, vc_ref, bias_ref, o_ref, *, rows):
    g = pl.program_id(1)
    ks = jnp.clip(NA_QROWS * g - NA_ROWS // 2, 0, rows - NA_KROWS)
    start = pl.multiple_of(ks * GRID_W, GRID_W * (NA_ROWS // 2))
    q = q_ref[0]
    k_win = k_ref[0, pl.ds(start, NA_TK), :]
    v_win = v_ref[0, pl.ds(start, NA_TK), :]
    kc, vc = kc_ref[0], vc_ref[0]
    nt = (((1,), (1,)), ((), ()))
    out = jnp.zeros(q.shape, F32)
    for h in range(N_HEADS):
        hm = _head_lane_mask(h, q.shape)
        qh = jnp.where(hm, q, jnp.zeros_like(q))
        s_loc = lax.dot_general(qh, k_win, nt, preferred_element_type=F32) + bias_ref[0, h]
        s_ctx = lax.dot_general(qh, kc, nt, preferred_element_type=F32)
        m = jnp.maximum(jnp.max(s_loc, axis=-1, keepdims=True), jnp.max(s_ctx, axis=-1, keepdims=True))
        p_loc = jnp.exp(s_loc - m)
        p_ctx = jnp.exp(s_ctx - m)
        denom = jnp.sum(p_loc, axis=-1, keepdims=True) + jnp.sum(p_ctx, axis=-1, keepdims=True)
        oh = (jnp.dot(p_loc.astype(BF16), v_win, preferred_element_type=F32)
              + jnp.dot(p_ctx.astype(BF16), vc, preferred_element_type=F32)) / denom
        out = jnp.where(hm, oh, out)
    o_ref[0] = out.astype(o_ref.dtype)


def na_latent(na_lat, na_ctx, bias):
    b, l, _ = na_lat.shape
    lc = na_ctx.shape[1]
    rows = l // GRID_W
    nblk = rows // NA_QROWS
    pat = lambda bi, g: (jnp.where(g == 0, 0, jnp.where(g == nblk - 1, 2, 1)), 0, 0, 0)
    return pl.pallas_call(
        functools.partial(_na_kernel, rows=rows),
        grid=(b, nblk),
        in_specs=[
            pl.BlockSpec((1, NA_TQ, BRANCH_W), lambda bi, g: (bi, g, 0)),
            pl.BlockSpec((1, l, BRANCH_W), lambda bi, g: (bi, 0, 1)),
            pl.BlockSpec((1, l, BRANCH_W), lambda bi, g: (bi, 0, 2)),
            pl.BlockSpec((1, lc, BRANCH_W), lambda bi, g: (bi, 0, 1)),
            pl.BlockSpec((1, lc, BRANCH_W), lambda bi, g: (bi, 0, 2)),
            pl.BlockSpec((1, N_HEADS, NA_TQ, NA_TK), pat),
        ],
        out_specs=pl.BlockSpec((1, NA_TQ, BRANCH_W), lambda bi, g: (bi, g, 0)),
        out_shape=jax.ShapeDtypeStruct((b, l, BRANCH_W), BF16),
        compiler_params=_params("parallel", "arbitrary"),
        name="na_latent",
    )(na_lat, na_lat, na_lat, na_ctx, na_ctx, bias)


def _na_ctx_kernel(q_ref, k_ref, v_ref, o_ref):
    q, k, v = q_ref[0], k_ref[0], v_ref[0]
    nt = (((1,), (1,)), ((), ()))
    out = jnp.zeros(q.shape, F32)
    for h in range(N_HEADS):
        hm = _head_lane_mask(h, q.shape)
        s = lax.dot_general(jnp.where(hm, q, jnp.zeros_like(q)), k, nt, preferred_element_type=F32)
        p = jnp.exp(s - jnp.max(s, axis=-1, keepdims=True))
        oh = jnp.dot(p.astype(BF16), v, preferred_element_type=F32) / jnp.sum(p, axis=-1, keepdims=True)
        out = jnp.where(hm, oh, out)
    o_ref[0] = out.astype(o_ref.dtype)


def na_context(na_ctx):
    b, lc, _ = na_ctx.shape
    spec = lambda col: pl.BlockSpec((1, lc, BRANCH_W), lambda bi: (bi, 0, col))
    return pl.pallas_call(
        _na_ctx_kernel,
        grid=(b,),
        in_specs=[spec(0), spec(1), spec(2)],
        out_specs=pl.BlockSpec((1, lc, BRANCH_W), lambda bi: (bi, 0, 0)),
        out_shape=jax.ShapeDtypeStruct((b, lc, BRANCH_W), BF16),
        compiler_params=_params("parallel"),
        name="na_context",
    )(na_ctx, na_ctx, na_ctx)


def _sigmoid_tanh(x):
    return 0.5 * jnp.tanh(0.5 * x) + 0.5


def _merge_kernel(y5_ref, u5_ref, o1_ref, o2_ref, o3_ref, gate_ref, d5_ref, wglu_ref, wb_ref, wo_ref, x_ref, g1_ref,
                  nw_ref, sh_ref, sc_ref, rw_ref, xo_ref, h_ref, aff_ref):
    z = jax.nn.gelu(y5_ref[0] + d5_ref[...] * u5_ref[0].astype(F32))
    o_s5 = (z * _sigmoid_tanh(jnp.dot(z.astype(BF16), wglu_ref[...], preferred_element_type=F32))).astype(BF16)
    m = None
    for br in range(N_BRANCH):
        o_br = o_s5 if br == 0 else (o1_ref, o2_ref, o3_ref)[br - 1][0]
        proj = jnp.dot(o_br, wb_ref[br], preferred_element_type=F32)
        gate = _sigmoid_tanh(gate_ref[0, :, br * D_MODEL:(br + 1) * D_MODEL].astype(F32))
        m = gate * proj if m is None else m + gate * proj
    mix = jnp.dot(m.astype(BF16), wo_ref[...], preferred_element_type=F32)
    x_new = x_ref[0] + g1_ref[0] * mix
    xo_ref[0] = x_new
    h = _norm_mod(x_new, nw_ref[...], sh_ref[0], sc_ref[0])
    h_ref[0] = h.astype(BF16)
    logits = lax.dot_general(rw_ref[...], h, (((1,), (1,)), ((), ())),
                             precision=lax.Precision.HIGHEST, preferred_element_type=F32)
    logits = logits - jnp.max(logits, axis=0, keepdims=True)
    e = jnp.exp(logits)
    aff_ref[0] = e / jnp.sum(e, axis=0, keepdims=True)


def merge_and_route(y_s5, u_s5, outs, gates, s5_d, s5_glu_bf16, w_branch_bf16, w_out_bf16, x, g1,
                    norm_w, shift, scale, router_w_t, tm):
    b, n, d = x.shape
    tok = lambda bi, i: (bi, i, 0)
    per_b = lambda bi, i: (bi, 0, 0)
    const2 = lambda bi, i: (0, 0)
    const3 = lambda bi, i: (0, 0, 0)
    return pl.pallas_call(
        _merge_kernel,
        grid=(b, n // tm),
        in_specs=[pl.BlockSpec((1, tm, BRANCH_W), tok)] * (N_BRANCH + 1) + [
            pl.BlockSpec((1, tm, N_BRANCH * d), tok),
            pl.BlockSpec((1, BRANCH_W), const2),
            pl.BlockSpec((BRANCH_W, BRANCH_W), const2),
            pl.BlockSpec((N_BRANCH, BRANCH_W, d), const3),
            pl.BlockSpec((d, d), const2),
            pl.BlockSpec((1, tm, d), tok),
            pl.BlockSpec((1, 1, d), per_b),
            pl.BlockSpec((1, d), const2),
            pl.BlockSpec((1, 1, d), per_b),
            pl.BlockSpec((1, 1, d), per_b),
            pl.BlockSpec((N_EXPERTS, d), const2),
        ],
        out_specs=[
            pl.BlockSpec((1, tm, d), tok),
            pl.BlockSpec((1, tm, d), tok),
            pl.BlockSpec((1, N_EXPERTS, tm), lambda bi, i: (bi, 0, i)),
        ],
        out_shape=[
            jax.ShapeDtypeStruct((b, n, d), F32),
            jax.ShapeDtypeStruct((b, n, d), BF16),
            jax.ShapeDtypeStruct((b, N_EXPERTS, n), F32),
        ],
        compiler_params=_params("parallel", "parallel"),
        name="merge_route",
    )(y_s5, u_s5, *outs, gates, s5_d, s5_glu_bf16, w_branch_bf16, w_out_bf16, x, g1, norm_w, shift, scale,
      router_w_t)


MXU_N = 256


EXPERT_ROWS = 512
def _expert_kernel(x_ref, g_ref, g2_ref, wg_ref, wu_ref, wd_ref, y_ref, act_ref):
    nb, cap, d = x_ref.shape[1:]
    xb = x_ref[0].reshape(nb * cap, d)
    f = wg_ref.shape[2]
    for c0 in range(0, f, MXU_N):
        c1 = min(c0 + MXU_N, f)
        gate = jnp.dot(xb, wg_ref[0, :, c0:c1], preferred_element_type=F32)
        up = jnp.dot(xb, wu_ref[0, :, c0:c1], preferred_element_type=F32)
        act_ref[:, c0:c1] = (jax.nn.silu(gate) * up).astype(BF16)
    y = jnp.dot(act_ref[...], wd_ref[0], preferred_element_type=F32).reshape(nb, cap, d)
    y_ref[0] = (y * g_ref[0] * g2_ref[...]).astype(y_ref.dtype)


def expert_ffn(xe, ge, g2, w_gate, w_up, w_down):
    e, b, cap, d = xe.shape
    f = w_gate.shape[-1]
    nb = max(1, min(b, EXPERT_ROWS // cap))
    tok = lambda ei, i: (ei, i, 0, 0)
    return pl.pallas_call(
        _expert_kernel,
        grid=(e, b // nb),
        in_specs=[
            pl.BlockSpec((1, nb, cap, d), tok),
            pl.BlockSpec((1, nb, cap, 1), tok),
            pl.BlockSpec((nb, 1, d), lambda ei, i: (i, 0, 0)),
            pl.BlockSpec((1, d, f), lambda ei, i: (ei, 0, 0)),
            pl.BlockSpec((1, d, f), lambda ei, i: (ei, 0, 0)),
            pl.BlockSpec((1, f, d), lambda ei, i: (ei, 0, 0)),
        ],
        out_specs=pl.BlockSpec((1, nb, cap, d), tok),
        out_shape=jax.ShapeDtypeStruct((e, b, cap, d), BF16),
        scratch_shapes=[pltpu.VMEM((nb * cap, f), BF16)],
        compiler_params=_params("parallel", "arbitrary"),
        name="expert_ffn",
    )(xe, ge, g2, w_gate, w_up, w_down)


COMBINE_GROUP = 8


def _combine_kernel(idx_ref, y_ref, x_hbm, o_ref, yf_ref, sem):
    bi = pl.program_id(0)
    e = pl.program_id(1)
    cap = y_ref.shape[2]

    @pl.when(e == 0)
    def _():
        cp = pltpu.make_async_copy(x_hbm.at[bi], o_ref.at[0], sem)
        cp.start()
        cp.wait()

    yf_ref[...] = y_ref[0, 0].astype(F32)

    def body(c, carry):
        base = pl.multiple_of(c * COMBINE_GROUP, COMBINE_GROUP)
        ys = yf_ref[pl.ds(base, COMBINE_GROUP), :]
        rows = [idx_ref[0, 0, e * cap + base + k] for k in range(COMBINE_GROUP)]
        new = [o_ref[0, pl.ds(rows[k], 1), :] + ys[k:k + 1, :] for k in range(COMBINE_GROUP)]
        for k in range(COMBINE_GROUP):
            o_ref[0, pl.ds(rows[k], 1), :] = new[k]
        return carry

    lax.fori_loop(0, cap // COMBINE_GROUP, body, 0)


def moe_combine(x, y, idx):
    b, n, d = x.shape
    e, _, cap, _ = y.shape
    return pl.pallas_call(
        _combine_kernel,
        grid=(b, e),
        in_specs=[
            pl.BlockSpec((1, 1, e * cap), lambda bi, ei: (bi, 0, 0), memory_space=pltpu.SMEM),
            pl.BlockSpec((1, 1, cap, d), lambda bi, ei: (ei, bi, 0, 0)),
            pl.BlockSpec(memory_space=pl.ANY),
        ],
        out_specs=pl.BlockSpec((1, n, d), lambda bi, ei: (bi, 0, 0)),
        out_shape=jax.ShapeDtypeStruct((b, n, d), F32),
        scratch_shapes=[pltpu.VMEM((cap, d), F32), pltpu.SemaphoreType.DMA(())],
        compiler_params=_params("parallel", "arbitrary"),
        name="moe_combine",
    )(idx, y, x)


def expert_choice_ffn(x, g2, h_bf16, aff_t, w_gate, w_up, w_down):
    b, n, d = h_bf16.shape
    cap = EC_CAPACITY * n // N_EXPERTS
    g, idx = lax.top_k(aff_t, cap)
    idx_e = jnp.moveaxis(idx, 1, 0)
    g_e = jnp.moveaxis(g, 1, 0)[..., None]
    xe = h_bf16[jnp.arange(b)[None, :, None], idx_e]
    y = expert_ffn(xe, g_e, g2, w_gate, w_up, w_down)
    return moe_combine(x, y, idx.reshape(b, 1, N_EXPERTS * cap).astype(jnp.int32))


S5_BLK = 16
S5_ROW = S5_GROUPS * S5_BLK * S5_GROUP
S5_PAIRS = S5_GROUPS // 2
S5_LANES = S5_GROUPS * S5_STATE


def _pair_blockdiag(t):
    g, r, c = t.shape
    t = t.reshape(g // 2, 2, r, c)
    z = jnp.zeros_like(t[:, 0])
    top = jnp.concatenate([t[:, 0], z], axis=-1)
    bot = jnp.concatenate([z, t[:, 1]], axis=-1)
    return jnp.concatenate([top, bot], axis=-2)


def s5_operators(lam_re, lam_im, log_step, b_re, b_im, c_re, c_im):
    hp = lax.Precision.HIGHEST
    blk = S5_BLK
    dt = jnp.exp(log_step)[..., None]
    k = jnp.arange(blk + 1, dtype=F32)
    mag = jnp.exp((lam_re * dt)[..., None] * k)
    ang = (lam_im * dt)[..., None] * k
    pr, pi = mag * jnp.cos(ang), mag * jnp.sin(ang)
    ar, ai = pr[..., 1], pi[..., 1]
    den = lam_re * lam_re + lam_im * lam_im
    zr = ((ar - 1.0) * lam_re + ai * lam_im) / den
    zi = (ai * lam_re - (ar - 1.0) * lam_im) / den
    bb_re = zr[..., None] * b_re - zi[..., None] * b_im
    bb_im = zr[..., None] * b_im + zi[..., None] * b_re
    ca_re = c_re[..., None] * pr[:, :, None] - c_im[..., None] * pi[:, :, None]
    ca_im = c_re[..., None] * pi[:, :, None] + c_im[..., None] * pr[:, :, None]
    kern = (jnp.einsum('dgpnl,dgnq->dglpq', ca_re, bb_re, precision=hp)
            - jnp.einsum('dgpnl,dgnq->dglpq', ca_im, bb_im, precision=hp))
    j = np.arange(blk)[:, None]
    i = np.arange(blk)[None, :]
    ms, ws, rres, rims = [], [], [], []
    for d in range(2):
        lag = (i - j) if d == 0 else (j - i)
        valid = jnp.asarray(lag >= 0, F32)[None, :, None, :, None]
        kd = kern[d][:, np.clip(lag, 0, blk - 1)]
        m = jnp.transpose(kd, (0, 1, 4, 2, 3)) * valid
        ms.append(m.reshape(S5_GROUPS, blk * S5_GROUP, blk * S5_GROUP))
        pw = (blk - 1 - np.arange(blk)) if d == 0 else np.arange(blk)
        apr, api = pr[d][..., pw], pi[d][..., pw]
        w_re = apr[..., None] * bb_re[d][:, :, None] - api[..., None] * bb_im[d][:, :, None]
        w_im = apr[..., None] * bb_im[d][:, :, None] + api[..., None] * bb_re[d][:, :, None]
        to_w = lambda t: jnp.transpose(t, (0, 2, 3, 1)).reshape(S5_GROUPS, blk * S5_GROUP, S5_STATE)
        ws.append((_pair_blockdiag(to_w(w_re)), _pair_blockdiag(to_w(w_im))))
        ex = (np.arange(blk) + 1) if d == 0 else (blk - np.arange(blk))
        r_re = ca_re[d][..., ex]
        r_im = -ca_im[d][..., ex]
        to_r = lambda t: jnp.transpose(t, (0, 2, 3, 1)).reshape(S5_GROUPS, S5_STATE, blk * S5_GROUP)
        rres.append(_pair_blockdiag(to_r(r_re)))
        rims.append(_pair_blockdiag(to_r(r_im)))
    m_op = jnp.stack(ms).astype(BF16)
    w_re = jnp.stack([w[0] for w in ws]).astype(BF16)
    w_im = jnp.stack([w[1] for w in ws]).astype(BF16)
    r_re = jnp.stack(rres).astype(BF16)
    r_im = jnp.stack(rims).astype(BF16)
    a_blk = jnp.stack([pr[..., blk].reshape(2, 1, S5_LANES), pi[..., blk].reshape(2, 1, S5_LANES)], axis=1)
    return m_op, w_re, w_im, r_re, r_im, a_blk


def _s5_kernel(uc_ref, ul_ref, m_ref, wre_ref, wim_ref, rre_ref, rim_ref, a_ref, yc_ref, yl_ref,
               vre, vim, sre, sim):
    rc, rl = uc_ref.shape[1], ul_ref.shape[1]
    segs = ((uc_ref, yc_ref, 0, rc), (ul_ref, yl_ref, rc, rl))
    gw = S5_BLK * S5_GROUP
    for d in range(2):
        for u_ref, _, base, rows in segs:
            for h in range(S5_PAIRS):
                u_pair = u_ref[0, :, 2 * h * gw:2 * (h + 1) * gw]
                vre[base:base + rows, h * 128:(h + 1) * 128] = jnp.dot(
                    u_pair, wre_ref[d, h], preferred_element_type=F32)
                vim[base:base + rows, h * 128:(h + 1) * 128] = jnp.dot(
                    u_pair, wim_ref[d, h], preferred_element_type=F32)
        ar, ai = a_ref[d, 0], a_ref[d, 1]

        def run(base, rows, carry):
            def step(t, c):
                xr, xi = c
                idx = base + (t if d == 0 else rows - 1 - t)
                sre[pl.ds(idx, 1), :] = xr
                sim[pl.ds(idx, 1), :] = xi
                nr = ar * xr - ai * xi + vre[pl.ds(idx, 1), :]
                ni = ar * xi + ai * xr + vim[pl.ds(idx, 1), :]
                return nr, ni
            return lax.fori_loop(0, rows, step, carry)

        zero = jnp.zeros((1, S5_LANES), F32)
        carry = run(0, rc, (zero, zero))
        run(rc, rl, carry)
        for u_ref, y_ref, base, rows in segs:
            for h in range(S5_PAIRS):
                s_r = sre[base:base + rows, h * 128:(h + 1) * 128].astype(BF16)
                s_i = sim[base:base + rows, h * 128:(h + 1) * 128].astype(BF16)
                y = (jnp.dot(s_r, rre_ref[d, h], preferred_element_type=F32)
                     + jnp.dot(s_i, rim_ref[d, h], preferred_element_type=F32))
                for gl in range(2):
                    g = 2 * h + gl
                    yg = y[:, gl * gw:(gl + 1) * gw] + jnp.dot(
                        u_ref[0, :, g * gw:(g + 1) * gw], m_ref[d, g], preferred_element_type=F32)
                    if d == 0:
                        y_ref[0, :, g * gw:(g + 1) * gw] = yg
                    else:
                        y_ref[0, :, g * gw:(g + 1) * gw] += yg


def _to_s5_rows(u):
    b, n, _ = u.shape
    u = u.reshape(b, n // S5_BLK, S5_BLK, S5_GROUPS, S5_GROUP)
    return jnp.transpose(u, (0, 1, 3, 2, 4)).reshape(b, n // S5_BLK, S5_ROW)


def _from_s5_rows(y):
    b, r, _ = y.shape
    y = y.reshape(b, r, S5_GROUPS, S5_BLK, S5_GROUP)
    return jnp.transpose(y, (0, 1, 3, 2, 4)).reshape(b, r * S5_BLK, BRANCH_W)


def s5_scan_readout(u_ctx, u_lat, ops):
    m_op, w_re, w_im, r_re, r_im, a_blk = ops
    b = u_lat.shape[0]
    uc, ul = _to_s5_rows(u_ctx), _to_s5_rows(u_lat)
    rc, rl = uc.shape[1], ul.shape[1]
    per_b = lambda bi: (bi, 0, 0)
    c4 = lambda bi: (0, 0, 0, 0)
    yc, yl = pl.pallas_call(
        _s5_kernel,
        grid=(b,),
        in_specs=[
            pl.BlockSpec((1, rc, S5_ROW), per_b),
            pl.BlockSpec((1, rl, S5_ROW), per_b),
            pl.BlockSpec(m_op.shape, c4),
            pl.BlockSpec(w_re.shape, c4),
            pl.BlockSpec(w_im.shape, c4),
            pl.BlockSpec(r_re.shape, c4),
            pl.BlockSpec(r_im.shape, c4),
            pl.BlockSpec(a_blk.shape, c4),
        ],
        out_specs=[pl.BlockSpec((1, rc, S5_ROW), per_b), pl.BlockSpec((1, rl, S5_ROW), per_b)],
        out_shape=[jax.ShapeDtypeStruct((b, rc, S5_ROW), F32), jax.ShapeDtypeStruct((b, rl, S5_ROW), F32)],
        scratch_shapes=[pltpu.VMEM((rc + rl, S5_LANES), F32) for _ in range(4)],
        compiler_params=_params("parallel"),
        name="s5_scan",
    )(uc, ul, m_op, w_re, w_im, r_re, r_im, a_blk)
    return _from_s5_rows(yc), _from_s5_rows(yl)


LA_TILE = 128
LA_UNROLL = 2


def _stack_heads(q):
    return jnp.concatenate(
        [jnp.where(_head_lane_mask(h, q.shape), q, jnp.zeros_like(q)) for h in range(N_HEADS)], axis=0)


def _unstack_heads(o_stack, t):
    out = jnp.zeros((t, o_stack.shape[1]), o_stack.dtype)
    for h in range(N_HEADS):
        blk = o_stack[h * t:(h + 1) * t]
        out = jnp.where(_head_lane_mask(h, blk.shape), blk, out)
    return out


_NT = (((1,), (1,)), ((), ()))


def _state_update(st_ref, decay_lane, v, k_scaled, head_avg):
    vt = jnp.transpose(v.astype(F32)).astype(BF16)
    kv = jnp.dot(vt, k_scaled.astype(BF16), preferred_element_type=F32)
    st_ref[...] = decay_lane * st_ref[...] + jnp.where(head_avg > 0, kv, 0.0)


def _for_tiles(n_tiles, reverse, body):
    def step(i, carry):
        body(n_tiles - 1 - i if reverse else i)
        return carry
    lax.fori_loop(0, n_tiles, step, 0, unroll=LA_UNROLL)


def _ret_kernel(qc_ref, kc_ref, vc_ref, gc_ref, ql_ref, kl_ref, vl_ref, gl_ref, dmask_ref, xi_ref, zeta_ref,
                gam_ref, havg_ref, oc_ref, ol_ref, st_ref, accc_ref, accl_ref):
    t = LA_TILE
    segs = ((qc_ref, kc_ref, vc_ref, accc_ref), (ql_ref, kl_ref, vl_ref, accl_ref))
    for d in range(2):
        st_ref[...] = jnp.zeros_like(st_ref)
        for q_ref, k_ref, v_ref, acc_ref in segs:
            def tile(i, q_ref=q_ref, k_ref=k_ref, v_ref=v_ref, acc_ref=acc_ref):
                rows = pl.ds(pl.multiple_of(i * t, t), t)
                q, k, v = q_ref[0, rows, :], k_ref[0, rows, :], v_ref[0, rows, :]
                att = lax.dot_general(_stack_heads(q), k, _NT, preferred_element_type=F32) * dmask_ref[d]
                intra = _unstack_heads(jnp.dot(att.astype(BF16), v, preferred_element_type=F32), t)
                cross = lax.dot_general(q, st_ref[...].astype(BF16), _NT, preferred_element_type=F32)
                o = intra + cross * xi_ref[d]
                if d == 0:
                    acc_ref[rows, :] = o
                else:
                    acc_ref[rows, :] += o
                _state_update(st_ref, gam_ref[d], v, k.astype(F32) * zeta_ref[d], havg_ref[...])
            _for_tiles(q_ref.shape[1] // t, d == 1, tile)
    for acc_ref, g_ref, o_ref in ((accc_ref, gc_ref, oc_ref), (accl_ref, gl_ref, ol_ref)):
        o = acc_ref[...]
        o = o * lax.rsqrt(_head_mean_sq(o, havg_ref[...]) + EPS)
        o_ref[0] = (o * jax.nn.silu(g_ref[0].astype(F32))).astype(o_ref.dtype)


def retention_tables(decay_logit):
    t = LA_TILE
    lg = jax.nn.log_sigmoid(decay_logit)
    idx = jnp.arange(t, dtype=F32)
    diff = idx[:, None] - idx[None, :]
    lgm = lg[:, :, None, None]
    fwd = jnp.where(diff >= 0, jnp.exp(jnp.maximum(diff, 0.0) * lgm[0]), 0.0)
    bwd = jnp.where(diff <= 0, jnp.exp(jnp.maximum(-diff, 0.0) * lgm[1]), 0.0)
    dmask = jnp.stack([fwd.reshape(N_HEADS * t, t), bwd.reshape(N_HEADS * t, t)])
    lane = lambda a: jnp.repeat(a, HEAD_DIM, axis=-1)
    lg_l = lane(lg)[:, None, :]
    steps_q = jnp.stack([idx + 1.0, t - idx])[:, :, None]
    steps_k = jnp.stack([t - 1.0 - idx, idx])[:, :, None]
    xi = jnp.exp(steps_q * lg_l)
    zeta = jnp.exp(steps_k * lg_l)
    gam = jnp.exp(t * lg_l)
    return dmask, xi, zeta, gam


def retention_pallas(rt_ctx, rt_lat, decay_logit):
    b, l, _ = rt_lat.shape
    lc = rt_ctx.shape[1]
    dmask, xi, zeta, gam = retention_tables(decay_logit)
    col = lambda n, c: pl.BlockSpec((1, n, BRANCH_W), lambda bi: (bi, 0, c))
    const = lambda a: pl.BlockSpec(a.shape, lambda bi: (0,) * a.ndim)
    havg = head_avg_matrix()
    return pl.pallas_call(
        _ret_kernel,
        grid=(b,),
        in_specs=[col(lc, c) for c in range(4)] + [col(l, c) for c in range(4)]
        + [const(dmask), const(xi), const(zeta), const(gam), const(havg)],
        out_specs=[col(lc, 0), col(l, 0)],
        out_shape=[jax.ShapeDtypeStruct((b, lc, BRANCH_W), BF16), jax.ShapeDtypeStruct((b, l, BRANCH_W), BF16)],
        scratch_shapes=[pltpu.VMEM((BRANCH_W, BRANCH_W), F32), pltpu.VMEM((lc, BRANCH_W), F32),
                        pltpu.VMEM((l, BRANCH_W), F32)],
        compiler_params=_params("parallel"),
        name="retention",
    )(rt_ctx, rt_ctx, rt_ctx, rt_ctx, rt_lat, rt_lat, rt_lat, rt_lat, dmask, xi, zeta, gam, havg)


HG_LEVELS = (32, 64, 128)


def hg_level_masks():
    t = LA_TILE
    i = (np.arange(N_HEADS * t) % t)[:, None]
    j = np.arange(t)[None, :]
    out = []
    for d in range(2):
        masks = [(i // HG_CHUNK == j // HG_CHUNK) & ((j <= i) if d == 0 else (j >= i))]
        for blk in HG_LEVELS:
            qi_late = (i % blk) >= blk // 2
            kj_late = (j % blk) >= blk // 2
            cross = (qi_late & ~kj_late) if d == 0 else (~qi_late & kj_late)
            masks.append((i // blk == j // blk) & cross)
        out.append(np.stack(masks))
    return jnp.asarray(np.stack(out), F32)


def _exact_rows_sum(sel, x):
    hi = x.astype(BF16)
    r1 = x - hi.astype(F32)
    mid = r1.astype(BF16)
    lo = (r1 - mid.astype(F32)).astype(BF16)
    return (jnp.dot(sel, hi, preferred_element_type=F32) + jnp.dot(sel, mid, preferred_element_type=F32)
            + jnp.dot(sel, lo, preferred_element_type=F32))


def _hg_tile(d, q, k, v, logf, st_ref, tri, mask_ref):
    t = LA_TILE
    w = q.shape[1]
    g = _exact_rows_sum(tri, logf)
    nb = t // HG_CHUNK
    g3 = g.reshape(nb, HG_CHUNK, w)
    if d == 0:
        edge = g3[:, HG_CHUNK - 1:HG_CHUNK, :]
        prev = jnp.concatenate([jnp.zeros((1, 1, w), F32), edge[:-1]], axis=0)
    else:
        edge = g3[:, 0:1, :]
        prev = jnp.concatenate([edge[1:], jnp.zeros((1, 1, w), F32)], axis=0)
    cum = (g3 - prev).reshape(t, w)
    qs = [q * jnp.exp(cum)]
    ks = [k * jnp.exp(-cum)]
    for blk in HG_LEVELS:
        gb = g.reshape(t // blk, blk, w)
        row = blk // 2 - 1 if d == 0 else blk // 2
        mid = jnp.broadcast_to(gb[:, row:row + 1, :], gb.shape).reshape(t, w)
        decay_to_mid = jnp.exp(-jnp.abs(g - mid))
        qs.append(q * decay_to_mid)
        ks.append(k * decay_to_mid)
    att = None
    for lvl, (qq, kk) in enumerate(zip(qs, ks)):
        a = lax.dot_general(_stack_heads(qq.astype(BF16)), kk.astype(BF16), _NT, preferred_element_type=F32)
        a = jnp.where(mask_ref[d, lvl] > 0.5, a, 0.0)
        att = a if att is None else att + a
    intra = _unstack_heads(jnp.dot(att.astype(BF16), v, preferred_element_type=F32), t)
    cross = lax.dot_general((q * jnp.exp(g)).astype(BF16), st_ref[...].astype(BF16), _NT,
                            preferred_element_type=F32)
    total = g[t - 1:t, :] if d == 0 else g[0:1, :]
    return intra + cross, jnp.exp(total), k * jnp.exp(total - g)


def _hg_kernel(pc_ref, pl_ref, lb_ref, nw_ref, havg_ref, mask_ref, oc_ref, ol_ref, st_ref, accc_ref, accl_ref):
    t = LA_TILE
    w = BRANCH_W
    lb = lb_ref[...]
    r = lax.broadcasted_iota(jnp.int32, (t, t), 0)
    c = lax.broadcasted_iota(jnp.int32, (t, t), 1)
    for d in range(2):
        tri = jnp.where((c <= r) if d == 0 else (c >= r), 1.0, 0.0).astype(BF16)
        st_ref[...] = jnp.zeros_like(st_ref)
        for p_ref, acc_ref in ((pc_ref, accc_ref), (pl_ref, accl_ref)):
            def tile(i, p_ref=p_ref, acc_ref=acc_ref):
                rows = pl.ds(pl.multiple_of(i * t, t), t)
                q = jax.nn.silu(p_ref[0, rows, 0:w].astype(F32))
                f_logit = p_ref[0, rows, (1 + d) * w:(2 + d) * w].astype(F32)
                v = p_ref[0, rows, 3 * w:4 * w]
                fg = lb + (1.0 - lb) * jax.nn.sigmoid(f_logit)
                o, decay, k_end = _hg_tile(d, q, 1.0 - fg, v, jnp.log(fg), st_ref, tri, mask_ref)
                if d == 0:
                    acc_ref[rows, :] = o
                else:
                    acc_ref[rows, :] += o
                _state_update(st_ref, decay, v, k_end, havg_ref[...])
            _for_tiles(p_ref.shape[1] // t, d == 1, tile)
    for acc_ref, p_ref, o_ref in ((accc_ref, pc_ref, oc_ref), (accl_ref, pl_ref, ol_ref)):
        o = acc_ref[...]
        o = o * lax.rsqrt(_head_mean_sq(o, havg_ref[...]) + EPS) * nw_ref[...]
        o_ref[0] = (o * jax.nn.silu(p_ref[0, :, 4 * w:5 * w].astype(F32))).astype(o_ref.dtype)


def hgrn2_pallas(hg_ctx, hg_lat, lower_bound, norm_w):
    b, l, width = hg_lat.shape
    lc = hg_ctx.shape[1]
    full = lambda n: pl.BlockSpec((1, n, width), lambda bi: (bi, 0, 0))
    out = lambda n: pl.BlockSpec((1, n, BRANCH_W), lambda bi: (bi, 0, 0))
    vec = pl.BlockSpec((1, BRANCH_W), lambda bi: (0, 0))
    havg = head_avg_matrix()
    masks = hg_level_masks()
    return pl.pallas_call(
        _hg_kernel,
        grid=(b,),
        in_specs=[full(lc), full(l), vec, vec, pl.BlockSpec(havg.shape, lambda bi: (0, 0)),
                  pl.BlockSpec(masks.shape, lambda bi: (0, 0, 0, 0))],
        out_specs=[out(lc), out(l)],
        out_shape=[jax.ShapeDtypeStruct((b, lc, BRANCH_W), BF16), jax.ShapeDtypeStruct((b, l, BRANCH_W), BF16)],
        scratch_shapes=[pltpu.VMEM((BRANCH_W, BRANCH_W), F32), pltpu.VMEM((lc, BRANCH_W), F32),
                        pltpu.VMEM((l, BRANCH_W), F32)],
        compiler_params=_params("parallel"),
        name="hgrn2",
    )(hg_ctx, hg_lat, lower_bound[None], jnp.tile(norm_w, N_HEADS)[None], havg, masks)


def kernel(x, c, ctx, c_ctx, ada_w, ada_b, norm_mix_w, norm_ffn_w, w_in, s5_lam_re, s5_lam_im, s5_log_step,
           s5_b_re, s5_b_im, s5_c_re, s5_c_im, s5_d, s5_glu_w, na_q_norm, na_k_norm, na_rpb, hg_lower_bounds,
           hg_norm_w, ret_decay_logit, w_branch, w_out, router_w, ex_w_gate, ex_w_up, ex_w_down):
    b = x.shape[0]
    lb_p = jax.nn.softmax(hg_lower_bounds, axis=0)
    lower_bounds = jnp.cumsum(lb_p, axis=0) - lb_p[0]
    cond_rows = jnp.concatenate([c, jnp.broadcast_to(c_ctx[None], c.shape)], axis=0)
    mods = ada_modulation(cond_rows, ada_w, ada_b, D_MODEL)
    xc = ctx
    for li in range(DEPTH):
        last = li == DEPTH - 1
        mod_l = [m[:, None] for m in jnp.split(mods[li, :b], 6, axis=-1)]
        mod_c = [m[:, None] for m in jnp.split(mods[li, b:], 6, axis=-1)]
        sh1_l, sc1_l, g1_l, sh2_l, sc2_l, g2_l = mod_l
        sh1_c, sc1_c, g1_c, sh2_c, sc2_c, g2_c = mod_c
        w_in_b = w_in[li].astype(BF16)
        wb_b = w_branch[li].astype(BF16)
        wo_b = w_out[li].astype(BF16)
        rw_t = router_w[li].T
        nmw = norm_mix_w[li][None]
        nfw = norm_ffn_w[li][None]

        rt_off = sum(IN_SPLITS[:3])
        w_rot_b = jnp.concatenate(
            [_swap_head_halves_cols(w_in[li][:, rt_off + i * BRANCH_W:rt_off + (i + 1) * BRANCH_W]) for i in range(2)],
            axis=1).astype(BF16)
        qk_w = jnp.stack([jnp.tile(na_q_norm[li], N_HEADS), jnp.tile(na_k_norm[li], N_HEADS)])
        s5_ops = s5_operators(s5_lam_re[li], s5_lam_im[li], s5_log_step[li], s5_b_re[li], s5_b_im[li],
                              s5_c_re[li], s5_c_im[li])
        s5_dl = s5_d[li][None]
        glu_b = s5_glu_w[li].astype(BF16)

        *pl_, ex_wg, ex_wu, ex_wd = in_proj(x, nmw, sh1_l, sc1_l, w_in_b, w_rot_b, qk_w, True, 256,
                                            (ex_w_gate, ex_w_up, ex_w_down), li)
        pc_ = in_proj(xc, nmw, sh1_c, sc1_c, w_in_b, w_rot_b, qk_w, False, 256)
        y5_c, y5_l = s5_scan_readout(pc_[0], pl_[0], s5_ops)
        na_l = na_latent(pl_[1], pc_[1], na_bias(na_rpb[li], x.shape[1] // GRID_W))
        hg_c, hg_l = hgrn2_pallas(pc_[2], pl_[2], lower_bounds[li], hg_norm_w[li])
        rt_c, rt_l = retention_pallas(pc_[3], pl_[3], ret_decay_logit[li])

        x, h_l, aff_l = merge_and_route(y5_l, pl_[0], (na_l, hg_l, rt_l), pl_[4], s5_dl, glu_b, wb_b, wo_b, x, g1_l,
                                        nfw, sh2_l, sc2_l, rw_t, 512)
        ex_w = (ex_wg, ex_wu, ex_wd)
        x = expert_choice_ffn(x, g2_l, h_l, aff_l, *ex_w)
        if not last:
            na_c = na_context(pc_[1])
            xc, h_c, aff_c = merge_and_route(y5_c, pc_[0], (na_c, hg_c, rt_c), pc_[4], s5_dl, glu_b, wb_b, wo_b, xc,
                                             g1_c, nfw, sh2_c, sc2_c, rw_t, 256)
            xc = expert_choice_ffn(xc, g2_c, h_c, aff_c, *ex_w)
    return x
```

```python
import functools
import math

import jax
import jax.numpy as jnp
import numpy as np
from jax import lax
from jax.experimental import pallas as pl
from jax.experimental.pallas import tpu as pltpu

D_MODEL = 1024
DEPTH = 2
GRID_W = 64
N_BRANCH = 4
BRANCH_W = 256
HEAD_DIM = 64
N_HEADS = BRANCH_W // HEAD_DIM
S5_GROUP = 16
S5_GROUPS = BRANCH_W // S5_GROUP
S5_STATE = 64
NA_ROWS = 8
NA_COLS = 16
HG_CHUNK = 16
RET_CHUNK = 128
N_EXPERTS = 16
EC_CAPACITY = 2
D_EXPERT = 2816
ROPE_BASE = 10000.0
EPS = 1e-6
IN_SPLITS = (BRANCH_W, 3 * BRANCH_W, 5 * BRANCH_W, 4 * BRANCH_W, N_BRANCH * D_MODEL)
D_IN = sum(IN_SPLITS)

F32 = jnp.float32
BF16 = jnp.bfloat16

V7X_VMEM_BYTES = 64 * 1024 * 1024
VMEM_LIMIT = V7X_VMEM_BYTES - 8 * 1024 * 1024
IN_PROJ_ROWS = 256
MERGE_ROWS = 512


def _params(*sem):
    return pltpu.CompilerParams(dimension_semantics=sem, vmem_limit_bytes=VMEM_LIMIT)


def _norm_mod(x, norm_w, shift, scale):
    y = x * lax.rsqrt(jnp.mean(x * x, axis=-1, keepdims=True) + EPS) * norm_w
    return y * (1.0 + scale) + shift


def _ada_kernel(c_ref, w_ref, b_ref, o_ref):
    cond = jax.nn.silu(c_ref[...])
    o_ref[0] = jnp.dot(cond, w_ref[0], precision=lax.Precision.HIGHEST, preferred_element_type=F32) + b_ref[0]


def ada_modulation(cond_in, ada_w, ada_b, tn):
    r, d = cond_in.shape
    depth, _, n = ada_w.shape
    return pl.pallas_call(
        _ada_kernel,
        grid=(depth, n // tn),
        in_specs=[
            pl.BlockSpec((r, d), lambda l, j: (0, 0)),
            pl.BlockSpec((1, d, tn), lambda l, j: (l, 0, j)),
            pl.BlockSpec((1, 1, tn), lambda l, j: (l, 0, j)),
        ],
        out_specs=pl.BlockSpec((1, r, tn), lambda l, j: (l, 0, j)),
        out_shape=jax.ShapeDtypeStruct((depth, r, n), F32),
        compiler_params=_params("parallel", "parallel"),
        name="ada_modulation",
    )(cond_in, ada_w, ada_b[:, None, :])


def _head_mean_sq(t, head_avg):
    sq = t * t
    hi = sq.astype(BF16)
    lo = (sq - hi.astype(F32)).astype(BF16)
    return (jnp.dot(hi, head_avg, preferred_element_type=F32)
            + jnp.dot(lo, head_avg, preferred_element_type=F32))


def _in_proj_kernel(x_ref, nw_ref, sh_ref, sc_ref, w_ref, wrot_ref, qkw_ref, havg_ref, cos_ref, sin_ref,
                    *refs, rope, n_cast):
    cast_in, (s5_ref, na_ref, hg_ref, rt_ref, gate_ref), cast_out = (
        refs[:n_cast], refs[n_cast:n_cast + 5], refs[n_cast + 5:])
    for src, dst in zip(cast_in, cast_out):
        dst[0] = src[0, 0].astype(BF16)
    hb = _norm_mod(x_ref[0], nw_ref[...], sh_ref[0], sc_ref[0]).astype(BF16)
    cw = BRANCH_W
    proj = lambda col: jnp.dot(hb, w_ref[:, col * cw:(col + 1) * cw], preferred_element_type=F32)
    col = 0
    s5_ref[0] = proj(col).astype(BF16)
    col += 1
    for part in range(3):
        t = proj(col + part)
        if part < 2:
            t = t * lax.rsqrt(_head_mean_sq(t, havg_ref[...]) + EPS) * qkw_ref[part:part + 1, :]
        if part == 0:
            t = t * (HEAD_DIM ** -0.5)
        na_ref[0, :, part * cw:(part + 1) * cw] = t.astype(BF16)
    col += 3
    for part in range(5):
        hg_ref[0, :, part * cw:(part + 1) * cw] = proj(col + part).astype(BF16)
    col += 5
    for part in range(4):
        t = proj(col + part)
        if part < 2 and rope:
            swapped = jnp.dot(hb, wrot_ref[:, part * cw:(part + 1) * cw], preferred_element_type=F32)
            t = t * cos_ref[...] + swapped * sin_ref[...]
        if part == 1:
            t = t * (HEAD_DIM ** -0.5)
        rt_ref[0, :, part * cw:(part + 1) * cw] = t.astype(BF16)
    col += 4
    for part in range(N_BRANCH * D_MODEL // cw):
        gate_ref[0, :, part * cw:(part + 1) * cw] = proj(col + part).astype(BF16)


def rope_tables(n_tokens):
    quarter = HEAD_DIM // 4
    t = jnp.arange(n_tokens, dtype=jnp.int32)
    inv = jnp.asarray(ROPE_BASE ** (-np.arange(quarter) / quarter), F32)
    ang = jnp.concatenate([(t // GRID_W).astype(F32)[:, None] * inv, (t % GRID_W).astype(F32)[:, None] * inv], axis=1)
    cos, sin = jnp.cos(ang), jnp.sin(ang)
    cos_h = jnp.concatenate([cos, cos], axis=1)
    sin_h = jnp.concatenate([-sin, sin], axis=1)
    return jnp.tile(cos_h, (1, N_HEADS)), jnp.tile(sin_h, (1, N_HEADS))


def _swap_head_halves_cols(w):
    d, c = w.shape
    w = w.reshape(d, c // HEAD_DIM, 2, HEAD_DIM // 2)
    return w[:, :, ::-1, :].reshape(d, c)


def head_avg_matrix():
    h = np.arange(BRANCH_W) // HEAD_DIM
    return jnp.asarray((h[:, None] == h[None, :]) / HEAD_DIM, BF16)


def in_proj(x, norm_w, shift, scale, w_in_bf16, w_rot_bf16, qk_norm_w, rope, tm, cast_stacks=(), layer=0):
    b, n, d = x.shape
    nt = n // tm
    tok = lambda bi, i: (bi, i, 0)
    per_b = lambda bi, i: (bi, 0, 0)
    const2 = lambda bi, i: (0, 0)
    cos, sin = rope_tables(n)
    cast_in_specs, cast_out_specs, cast_out_shapes = [], [], []
    for w in cast_stacks:
        _, e, r, c = w.shape
        per_expert = (b * nt) // e
        rows = r // per_expert
        assert per_expert * e == b * nt and rows * per_expert == r and rows % 16 == 0, (w.shape, b, nt)
        cast_in_specs.append(pl.BlockSpec(
            (1, 1, rows, c), lambda bi, i, pe=per_expert: (layer, (bi * nt + i) // pe, (bi * nt + i) % pe, 0)))
        cast_out_specs.append(pl.BlockSpec(
            (1, rows, c), lambda bi, i, pe=per_expert: ((bi * nt + i) // pe, (bi * nt + i) % pe, 0)))
        cast_out_shapes.append(jax.ShapeDtypeStruct((e, r, c), BF16))
    return pl.pallas_call(
        functools.partial(_in_proj_kernel, rope=rope, n_cast=len(cast_stacks)),
        grid=(b, nt),
        in_specs=[
            pl.BlockSpec((1, tm, d), tok),
            pl.BlockSpec((1, d), const2),
            pl.BlockSpec((1, 1, d), per_b),
            pl.BlockSpec((1, 1, d), per_b),
            pl.BlockSpec((d, D_IN), const2, pipeline_mode=pl.Buffered(1)),
            pl.BlockSpec((d, 2 * BRANCH_W), const2, pipeline_mode=pl.Buffered(1)),
            pl.BlockSpec((2, BRANCH_W), const2),
            pl.BlockSpec((BRANCH_W, BRANCH_W), const2),
            pl.BlockSpec((tm, BRANCH_W), lambda bi, i: (i, 0)),
            pl.BlockSpec((tm, BRANCH_W), lambda bi, i: (i, 0)),
        ] + cast_in_specs,
        out_specs=[pl.BlockSpec((1, tm, w), tok) for w in IN_SPLITS] + cast_out_specs,
        out_shape=[jax.ShapeDtypeStruct((b, n, w), BF16) for w in IN_SPLITS] + cast_out_shapes,
        compiler_params=_params("parallel", "parallel"),
        name="in_proj",
    )(x, norm_w, shift, scale, w_in_bf16, w_rot_bf16, qk_norm_w, head_avg_matrix(), cos, sin, *cast_stacks)


NA_QROWS = 8
NA_KROWS = 16
NA_TQ = NA_QROWS * GRID_W
NA_TK = NA_KROWS * GRID_W
NA_MASKED = -1e30


def na_bias(rpb, rows):
    hp = lax.Precision.HIGHEST
    nblk = rows // NA_QROWS
    n_dr, n_dc = 2 * NA_ROWS - 1, 2 * NA_COLS - 1
    qc = np.arange(GRID_W)[:, None]
    kc = np.arange(GRID_W)[None, :]
    cs = np.clip(qc - NA_COLS // 2, 0, GRID_W - NA_COLS)
    col_ok = (kc >= cs) & (kc < cs + NA_COLS)
    dc = np.clip(kc - qc + NA_COLS - 1, 0, n_dc - 1).reshape(-1)
    sel_dc = jnp.asarray(dc[None, :] == np.arange(n_dc)[:, None], F32)
    by_col = jnp.einsum('hrc,cx->hrx', rpb.astype(F32), sel_dc, precision=hp)
    pats = []
    for g in (0, 1, nblk - 1):
        ks = int(np.clip(NA_QROWS * g - NA_ROWS // 2, 0, rows - NA_KROWS))
        qr = (NA_QROWS * g + np.arange(NA_QROWS))[:, None]
        kr = (ks + np.arange(NA_KROWS))[None, :]
        band = np.clip(qr - NA_ROWS // 2, 0, rows - NA_ROWS)
        row_ok = (kr >= band) & (kr < band + NA_ROWS)
        dr = np.clip(kr - qr + NA_ROWS - 1, 0, n_dr - 1).reshape(-1)
        sel_dr = jnp.asarray(dr[:, None] == np.arange(n_dr)[None, :], F32)
        t = jnp.einsum('yr,hrx->hyx', sel_dr, by_col, precision=hp)
        t = t.reshape(N_HEADS, NA_QROWS, NA_KROWS, GRID_W, GRID_W)
        t = jnp.transpose(t, (0, 1, 3, 2, 4)).reshape(N_HEADS, NA_TQ, NA_TK)
        ok = (row_ok[:, None, :, None] & col_ok[None, :, None, :]).reshape(NA_TQ, NA_TK)
        pats.append(jnp.where(jnp.asarray(ok)[None], t, NA_MASKED))
    return jnp.stack(pats)


def _head_lane_mask(h, shape):
    lane = lax.broadcasted_iota(jnp.int32, shape, len(shape) - 1)
    return (lane >= h * HEAD_DIM) & (lane < (h + 1) * HEAD_DIM)


def _na_kernel(q_ref, k_ref, v_ref, kc_ref, vc_ref, bias_ref, o_ref, *, rows):
    g = pl.program_id(1)
    last = pl.num_programs(1) - 1
    pat = jnp.where(g == 0, 0, jnp.where(g == last, 2, 1))
    ks = jnp.clip(NA_QROWS * g - NA_ROWS // 2, 0, rows - NA_KROWS)
    start = pl.multiple_of(ks * GRID_W, GRID_W * (NA_ROWS // 2))
    q = q_ref[0]
    k_win = k_ref[0, pl.ds(start, NA_TK), :]
    v_win = v_ref[0, pl.ds(start, NA_TK), :]
    kc, vc = kc_ref[0], vc_ref[0]
    nt = (((1,), (1,)), ((), ()))
    out = jnp.zeros(q.shape, F32)
    for h in range(N_HEADS):
        hm = _head_lane_mask(h, q.shape)
        qh = jnp.where(hm, q, jnp.zeros_like(q))
        s_loc = lax.dot_general(qh, k_win, nt, preferred_element_type=F32) + bias_ref[pat, h]
        s_ctx = lax.dot_general(qh, kc, nt, preferred_element_type=F32)
        m = jnp.maximum(jnp.max(s_loc, axis=-1, keepdims=True), jnp.max(s_ctx, axis=-1, keepdims=True))
        p_loc = jnp.exp(s_loc - m)
        p_ctx = jnp.exp(s_ctx - m)
        denom = jnp.sum(p_loc, axis=-1, keepdims=True) + jnp.sum(p_ctx, axis=-1, keepdims=True)
        oh = (jnp.dot(p_loc.astype(BF16), v_win, preferred_element_type=F32)
              + jnp.dot(p_ctx.astype(BF16), vc, preferred_element_type=F32)) / denom
        out = jnp.where(hm, oh, out)
    o_ref[0] = out.astype(o_ref.dtype)


def na_latent(na_lat, na_ctx, bias):
    b, l, _ = na_lat.shape
    lc = na_ctx.shape[1]
    rows = l // GRID_W
    nblk = rows // NA_QROWS
    return pl.pallas_call(
        functools.partial(_na_kernel, rows=rows),
        grid=(b, nblk),
        in_specs=[
            pl.BlockSpec((1, NA_TQ, BRANCH_W), lambda bi, g: (bi, g, 0)),
            pl.BlockSpec((1, l, BRANCH_W), lambda bi, g: (bi, 0, 1)),
            pl.BlockSpec((1, l, BRANCH_W), lambda bi, g: (bi, 0, 2)),
            pl.BlockSpec((1, lc, BRANCH_W), lambda bi, g: (bi, 0, 1)),
            pl.BlockSpec((1, lc, BRANCH_W), lambda bi, g: (bi, 0, 2)),
            pl.BlockSpec(bias.shape, lambda bi, g: (0, 0, 0, 0), pipeline_mode=pl.Buffered(1)),
        ],
        out_specs=pl.BlockSpec((1, NA_TQ, BRANCH_W), lambda bi, g: (bi, g, 0)),
        out_shape=jax.ShapeDtypeStruct((b, l, BRANCH_W), BF16),
        compiler_params=_params("parallel", "arbitrary"),
        name="na_latent",
    )(na_lat, na_lat, na_lat, na_ctx, na_ctx, bias)


def _na_ctx_kernel(q_ref, k_ref, v_ref, o_ref):
    q, k, v = q_ref[0], k_ref[0], v_ref[0]
    nt = (((1,), (1,)), ((), ()))
    out = jnp.zeros(q.shape, F32)
    for h in range(N_HEADS):
        hm = _head_lane_mask(h, q.shape)
        s = lax.dot_general(jnp.where(hm, q, jnp.zeros_like(q)), k, nt, preferred_element_type=F32)
        p = jnp.exp(s - jnp.max(s, axis=-1, keepdims=True))
        oh = jnp.dot(p.astype(BF16), v, preferred_element_type=F32) / jnp.sum(p, axis=-1, keepdims=True)
        out = jnp.where(hm, oh, out)
    o_ref[0] = out.astype(o_ref.dtype)


def na_context(na_ctx):
    b, lc, _ = na_ctx.shape
    spec = lambda col: pl.BlockSpec((1, lc, BRANCH_W), lambda bi: (bi, 0, col))
    return pl.pallas_call(
        _na_ctx_kernel,
        grid=(b,),
        in_specs=[spec(0), spec(1), spec(2)],
        out_specs=pl.BlockSpec((1, lc, BRANCH_W), lambda bi: (bi, 0, 0)),
        out_shape=jax.ShapeDtypeStruct((b, lc, BRANCH_W), BF16),
        compiler_params=_params("parallel"),
        name="na_context",
    )(na_ctx, na_ctx, na_ctx)


def _sigmoid_tanh(x):
    return 0.5 * jnp.tanh(0.5 * x) + 0.5


def _merge_kernel(y5_ref, u5_ref, o1_ref, o2_ref, o3_ref, gate_ref, d5_ref, wglu_ref, wb_ref, wo_ref, x_ref, g1_ref,
                  nw_ref, sh_ref, sc_ref, rw_ref, xo_ref, h_ref, aff_ref):
    z = jax.nn.gelu(y5_ref[0] + d5_ref[...] * u5_ref[0].astype(F32))
    o_s5 = (z * _sigmoid_tanh(jnp.dot(z.astype(BF16), wglu_ref[...], preferred_element_type=F32))).astype(BF16)
    m = None
    for br in range(N_BRANCH):
        o_br = o_s5 if br == 0 else (o1_ref, o2_ref, o3_ref)[br - 1][0]
        proj = jnp.dot(o_br, wb_ref[br], preferred_element_type=F32)
        gate = _sigmoid_tanh(gate_ref[0, :, br * D_MODEL:(br + 1) * D_MODEL].astype(F32))
        m = gate * proj if m is None else m + gate * proj
    mix = jnp.dot(m.astype(BF16), wo_ref[...], preferred_element_type=F32)
    x_new = x_ref[0] + g1_ref[0] * mix
    xo_ref[0] = x_new
    h = _norm_mod(x_new, nw_ref[...], sh_ref[0], sc_ref[0])
    h_ref[0] = h.astype(BF16)
    logits = lax.dot_general(rw_ref[...], h, (((1,), (1,)), ((), ())),
                             precision=lax.Precision.HIGHEST, preferred_element_type=F32)
    logits = logits - jnp.max(logits, axis=0, keepdims=True)
    e = jnp.exp(logits)
    aff_ref[0] = e / jnp.sum(e, axis=0, keepdims=True)


def merge_and_route(y_s5, u_s5, outs, gates, s5_d, s5_glu_bf16, w_branch_bf16, w_out_bf16, x, g1,
                    norm_w, shift, scale, router_w_t, tm):
    b, n, d = x.shape
    tok = lambda bi, i: (bi, i, 0)
    per_b = lambda bi, i: (bi, 0, 0)
    const2 = lambda bi, i: (0, 0)
    const3 = lambda bi, i: (0, 0, 0)
    return pl.pallas_call(
        _merge_kernel,
        grid=(b, n // tm),
        in_specs=[pl.BlockSpec((1, tm, BRANCH_W), tok)] * (N_BRANCH + 1) + [
            pl.BlockSpec((1, tm, N_BRANCH * d), tok),
            pl.BlockSpec((1, BRANCH_W), const2),
            pl.BlockSpec((BRANCH_W, BRANCH_W), const2),
            pl.BlockSpec((N_BRANCH, BRANCH_W, d), const3),
            pl.BlockSpec((d, d), const2),
            pl.BlockSpec((1, tm, d), tok),
            pl.BlockSpec((1, 1, d), per_b),
            pl.BlockSpec((1, d), const2),
            pl.BlockSpec((1, 1, d), per_b),
            pl.BlockSpec((1, 1, d), per_b),
            pl.BlockSpec((N_EXPERTS, d), const2),
        ],
        out_specs=[
            pl.BlockSpec((1, tm, d), tok),
            pl.BlockSpec((1, tm, d), tok),
            pl.BlockSpec((1, N_EXPERTS, tm), lambda bi, i: (bi, 0, i)),
        ],
        out_shape=[
            jax.ShapeDtypeStruct((b, n, d), F32),
            jax.ShapeDtypeStruct((b, n, d), BF16),
            jax.ShapeDtypeStruct((b, N_EXPERTS, n), F32),
        ],
        compiler_params=_params("parallel", "parallel"),
        name="merge_route",
    )(y_s5, u_s5, *outs, gates, s5_d, s5_glu_bf16, w_branch_bf16, w_out_bf16, x, g1, norm_w, shift, scale,
      router_w_t)


MXU_N = 256


EXPERT_ROWS = 512
def _expert_kernel(x_ref, g_ref, g2_ref, wg_ref, wu_ref, wd_ref, y_ref, act_ref):
    nb, cap, d = x_ref.shape[1:]
    xb = x_ref[0].reshape(nb * cap, d)
    f = wg_ref.shape[2]
    for c0 in range(0, f, MXU_N):
        c1 = min(c0 + MXU_N, f)
        gate = jnp.dot(xb, wg_ref[0, :, c0:c1], preferred_element_type=F32)
        up = jnp.dot(xb, wu_ref[0, :, c0:c1], preferred_element_type=F32)
        act_ref[:, c0:c1] = (jax.nn.silu(gate) * up).astype(BF16)
    y = jnp.dot(act_ref[...], wd_ref[0], preferred_element_type=F32).reshape(nb, cap, d)
    y_ref[0] = (y * g_ref[0] * g2_ref[...]).astype(y_ref.dtype)


def expert_ffn(xe, ge, g2, w_gate, w_up, w_down):
    e, b, cap, d = xe.shape
    f = w_gate.shape[-1]
    nb = max(1, min(b, EXPERT_ROWS // cap))
    tok = lambda ei, i: (ei, i, 0, 0)
    return pl.pallas_call(
        _expert_kernel,
        grid=(e, b // nb),
        in_specs=[
            pl.BlockSpec((1, nb, cap, d), tok),
            pl.BlockSpec((1, nb, cap, 1), tok),
            pl.BlockSpec((nb, 1, d), lambda ei, i: (i, 0, 0)),
            pl.BlockSpec((1, d, f), lambda ei, i: (ei, 0, 0)),
            pl.BlockSpec((1, d, f), lambda ei, i: (ei, 0, 0)),
            pl.BlockSpec((1, f, d), lambda ei, i: (ei, 0, 0)),
        ],
        out_specs=pl.BlockSpec((1, nb, cap, d), tok),
        out_shape=jax.ShapeDtypeStruct((e, b, cap, d), BF16),
        scratch_shapes=[pltpu.VMEM((nb * cap, f), BF16)],
        compiler_params=_params("parallel", "arbitrary"),
        name="expert_ffn",
    )(xe, ge, g2, w_gate, w_up, w_down)


COMBINE_GROUP = 8


def _combine_kernel(idx_ref, y_ref, x_hbm, o_ref, yf_ref, sem):
    bi = pl.program_id(0)
    e = pl.program_id(1)
    cap = y_ref.shape[2]

    @pl.when(e == 0)
    def _():
        cp = pltpu.make_async_copy(x_hbm.at[bi], o_ref.at[0], sem)
        cp.start()
        cp.wait()

    yf_ref[...] = y_ref[0, 0].astype(F32)

    def body(c, carry):
        base = pl.multiple_of(c * COMBINE_GROUP, COMBINE_GROUP)
        ys = yf_ref[pl.ds(base, COMBINE_GROUP), :]
        rows = [idx_ref[0, 0, e * cap + base + k] for k in range(COMBINE_GROUP)]
        new = [o_ref[0, pl.ds(rows[k], 1), :] + ys[k:k + 1, :] for k in range(COMBINE_GROUP)]
        for k in range(COMBINE_GROUP):
            o_ref[0, pl.ds(rows[k], 1), :] = new[k]
        return carry

    lax.fori_loop(0, cap // COMBINE_GROUP, body, 0)


def moe_combine(x, y, idx):
    b, n, d = x.shape
    e, _, cap, _ = y.shape
    return pl.pallas_call(
        _combine_kernel,
        grid=(b, e),
        in_specs=[
            pl.BlockSpec((1, 1, e * cap), lambda bi, ei: (bi, 0, 0), memory_space=pltpu.SMEM),
            pl.BlockSpec((1, 1, cap, d), lambda bi, ei: (ei, bi, 0, 0)),
            pl.BlockSpec(memory_space=pl.ANY),
        ],
        out_specs=pl.BlockSpec((1, n, d), lambda bi, ei: (bi, 0, 0)),
        out_shape=jax.ShapeDtypeStruct((b, n, d), F32),
        scratch_shapes=[pltpu.VMEM((cap, d), F32), pltpu.SemaphoreType.DMA(())],
        compiler_params=_params("parallel", "arbitrary"),
        name="moe_combine",
    )(idx, y, x)


def expert_choice_ffn(x, g2, h_bf16, aff_t, w_gate, w_up, w_down):
    b, n, d = h_bf16.shape
    cap = EC_CAPACITY * n // N_EXPERTS
    g, idx = lax.top_k(aff_t, cap)
    idx_e = jnp.moveaxis(idx, 1, 0)
    g_e = jnp.moveaxis(g, 1, 0)[..., None]
    xe = h_bf16[jnp.arange(b)[None, :, None], idx_e]
    y = expert_ffn(xe, g_e, g2, w_gate, w_up, w_down)
    return moe_combine(x, y, idx.reshape(b, 1, N_EXPERTS * cap).astype(jnp.int32))


S5_BLK = 16
S5_ROW = S5_GROUPS * S5_BLK * S5_GROUP
S5_PAIRS = S5_GROUPS // 2
S5_LANES = S5_GROUPS * S5_STATE


def _pair_blockdiag(t):
    g, r, c = t.shape
    t = t.reshape(g // 2, 2, r, c)
    z = jnp.zeros_like(t[:, 0])
    top = jnp.concatenate([t[:, 0], z], axis=-1)
    bot = jnp.concatenate([z, t[:, 1]], axis=-1)
    return jnp.concatenate([top, bot], axis=-2)


def s5_operators(lam_re, lam_im, log_step, b_re, b_im, c_re, c_im):
    hp = lax.Precision.HIGHEST
    blk = S5_BLK
    dt = jnp.exp(log_step)[..., None]
    k = jnp.arange(blk + 1, dtype=F32)
    mag = jnp.exp((lam_re * dt)[..., None] * k)
    ang = (lam_im * dt)[..., None] * k
    pr, pi = mag * jnp.cos(ang), mag * jnp.sin(ang)
    ar, ai = pr[..., 1], pi[..., 1]
    den = lam_re * lam_re + lam_im * lam_im
    zr = ((ar - 1.0) * lam_re + ai * lam_im) / den
    zi = (ai * lam_re - (ar - 1.0) * lam_im) / den
    bb_re = zr[..., None] * b_re - zi[..., None] * b_im
    bb_im = zr[..., None] * b_im + zi[..., None] * b_re
    ca_re = c_re[..., None] * pr[:, :, None] - c_im[..., None] * pi[:, :, None]
    ca_im = c_re[..., None] * pi[:, :, None] + c_im[..., None] * pr[:, :, None]
    kern = (jnp.einsum('dgpnl,dgnq->dglpq', ca_re, bb_re, precision=hp)
            - jnp.einsum('dgpnl,dgnq->dglpq', ca_im, bb_im, precision=hp))
    j = np.arange(blk)[:, None]
    i = np.arange(blk)[None, :]
    ms, ws, rres, rims = [], [], [], []
    for d in range(2):
        lag = (i - j) if d == 0 else (j - i)
        valid = jnp.asarray(lag >= 0, F32)[None, :, None, :, None]
        kd = kern[d][:, np.clip(lag, 0, blk - 1)]
        m = jnp.transpose(kd, (0, 1, 4, 2, 3)) * valid
        ms.append(m.reshape(S5_GROUPS, blk * S5_GROUP, blk * S5_GROUP))
        pw = (blk - 1 - np.arange(blk)) if d == 0 else np.arange(blk)
        apr, api = pr[d][..., pw], pi[d][..., pw]
        w_re = apr[..., None] * bb_re[d][:, :, None] - api[..., None] * bb_im[d][:, :, None]
        w_im = apr[..., None] * bb_im[d][:, :, None] + api[..., None] * bb_re[d][:, :, None]
        to_w = lambda t: jnp.transpose(t, (0, 2, 3, 1)).reshape(S5_GROUPS, blk * S5_GROUP, S5_STATE)
        ws.append((_pair_blockdiag(to_w(w_re)), _pair_blockdiag(to_w(w_im))))
        ex = (np.arange(blk) + 1) if d == 0 else (blk - np.arange(blk))
        r_re = ca_re[d][..., ex]
        r_im = -ca_im[d][..., ex]
        to_r = lambda t: jnp.transpose(t, (0, 2, 3, 1)).reshape(S5_GROUPS, S5_STATE, blk * S5_GROUP)
        rres.append(_pair_blockdiag(to_r(r_re)))
        rims.append(_pair_blockdiag(to_r(r_im)))
    m_op = jnp.stack(ms).astype(BF16)
    w_re = jnp.stack([w[0] for w in ws]).astype(BF16)
    w_im = jnp.stack([w[1] for w in ws]).astype(BF16)
    r_re = jnp.stack(rres).astype(BF16)
    r_im = jnp.stack(rims).astype(BF16)
    a_blk = jnp.stack([pr[..., blk].reshape(2, 1, S5_LANES), pi[..., blk].reshape(2, 1, S5_LANES)], axis=1)
    return m_op, w_re, w_im, r_re, r_im, a_blk


def _s5_kernel(uc_ref, ul_ref, m_ref, wre_ref, wim_ref, rre_ref, rim_ref, a_ref, yc_ref, yl_ref,
               vre, vim, sre, sim):
    rc, rl = uc_ref.shape[1], ul_ref.shape[1]
    segs = ((uc_ref, yc_ref, 0, rc), (ul_ref, yl_ref, rc, rl))
    gw = S5_BLK * S5_GROUP
    for d in range(2):
        for u_ref, _, base, rows in segs:
            for h in range(S5_PAIRS):
                u_pair = u_ref[0, :, 2 * h * gw:2 * (h + 1) * gw]
                vre[base:base + rows, h * 128:(h + 1) * 128] = jnp.dot(
                    u_pair, wre_ref[d, h], preferred_element_type=F32)
                vim[base:base + rows, h * 128:(h + 1) * 128] = jnp.dot(
                    u_pair, wim_ref[d, h], preferred_element_type=F32)
        ar, ai = a_ref[d, 0], a_ref[d, 1]

        def run(base, rows, carry):
            def step(t, c):
                xr, xi = c
                idx = base + (t if d == 0 else rows - 1 - t)
                sre[pl.ds(idx, 1), :] = xr
                sim[pl.ds(idx, 1), :] = xi
                nr = ar * xr - ai * xi + vre[pl.ds(idx, 1), :]
                ni = ar * xi + ai * xr + vim[pl.ds(idx, 1), :]
                return nr, ni
            return lax.fori_loop(0, rows, step, carry)

        zero = jnp.zeros((1, S5_LANES), F32)
        carry = run(0, rc, (zero, zero))
        run(rc, rl, carry)
        for u_ref, y_ref, base, rows in segs:
            for h in range(S5_PAIRS):
                s_r = sre[base:base + rows, h * 128:(h + 1) * 128].astype(BF16)
                s_i = sim[base:base + rows, h * 128:(h + 1) * 128].astype(BF16)
                y = (jnp.dot(s_r, rre_ref[d, h], preferred_element_type=F32)
                     + jnp.dot(s_i, rim_ref[d, h], preferred_element_type=F32))
                for gl in range(2):
                    g = 2 * h + gl
                    yg = y[:, gl * gw:(gl + 1) * gw] + jnp.dot(
                        u_ref[0, :, g * gw:(g + 1) * gw], m_ref[d, g], preferred_element_type=F32)
                    if d == 0:
                        y_ref[0, :, g * gw:(g + 1) * gw] = yg
                    else:
                        y_ref[0, :, g * gw:(g + 1) * gw] += yg


def _to_s5_rows(u):
    b, n, _ = u.shape
    u = u.reshape(b, n // S5_BLK, S5_BLK, S5_GROUPS, S5_GROUP)
    return jnp.transpose(u, (0, 1, 3, 2, 4)).reshape(b, n // S5_BLK, S5_ROW)


def _from_s5_rows(y):
    b, r, _ = y.shape
    y = y.reshape(b, r, S5_GROUPS, S5_BLK, S5_GROUP)
    return jnp.transpose(y, (0, 1, 3, 2, 4)).reshape(b, r * S5_BLK, BRANCH_W)


def s5_scan_readout(u_ctx, u_lat, ops):
    m_op, w_re, w_im, r_re, r_im, a_blk = ops
    b = u_lat.shape[0]
    uc, ul = _to_s5_rows(u_ctx), _to_s5_rows(u_lat)
    rc, rl = uc.shape[1], ul.shape[1]
    per_b = lambda bi: (bi, 0, 0)
    c4 = lambda bi: (0, 0, 0, 0)
    yc, yl = pl.pallas_call(
        _s5_kernel,
        grid=(b,),
        in_specs=[
            pl.BlockSpec((1, rc, S5_ROW), per_b),
            pl.BlockSpec((1, rl, S5_ROW), per_b),
            pl.BlockSpec(m_op.shape, c4),
            pl.BlockSpec(w_re.shape, c4),
            pl.BlockSpec(w_im.shape, c4),
            pl.BlockSpec(r_re.shape, c4),
            pl.BlockSpec(r_im.shape, c4),
            pl.BlockSpec(a_blk.shape, c4),
        ],
        out_specs=[pl.BlockSpec((1, rc, S5_ROW), per_b), pl.BlockSpec((1, rl, S5_ROW), per_b)],
        out_shape=[jax.ShapeDtypeStruct((b, rc, S5_ROW), F32), jax.ShapeDtypeStruct((b, rl, S5_ROW), F32)],
        scratch_shapes=[pltpu.VMEM((rc + rl, S5_LANES), F32) for _ in range(4)],
        compiler_params=_params("parallel"),
        name="s5_scan",
    )(uc, ul, m_op, w_re, w_im, r_re, r_im, a_blk)
    return _from_s5_rows(yc), _from_s5_rows(yl)


LA_TILE = 128
LA_UNROLL = 2


def _stack_heads(q):
    return jnp.concatenate(
        [jnp.where(_head_lane_mask(h, q.shape), q, jnp.zeros_like(q)) for h in range(N_HEADS)], axis=0)


def _unstack_heads(o_stack, t):
    out = jnp.zeros((t, o_stack.shape[1]), o_stack.dtype)
    for h in range(N_HEADS):
        blk = o_stack[h * t:(h + 1) * t]
        out = jnp.where(_head_lane_mask(h, blk.shape), blk, out)
    return out


_NT = (((1,), (1,)), ((), ()))


def _state_update(st_ref, decay_lane, v, k_scaled, head_avg):
    vt = jnp.transpose(v.astype(F32)).astype(BF16)
    kv = jnp.dot(vt, k_scaled.astype(BF16), preferred_element_type=F32)
    st_ref[...] = decay_lane * st_ref[...] + jnp.where(head_avg > 0, kv, 0.0)


def _for_tiles(n_tiles, reverse, body):
    def step(i, carry):
        body(n_tiles - 1 - i if reverse else i)
        return carry
    lax.fori_loop(0, n_tiles, step, 0, unroll=LA_UNROLL)


def _ret_kernel(qc_ref, kc_ref, vc_ref, gc_ref, ql_ref, kl_ref, vl_ref, gl_ref, dmask_ref, xi_ref, zeta_ref,
                gam_ref, havg_ref, oc_ref, ol_ref, st_ref, accc_ref, accl_ref):
    t = LA_TILE
    segs = ((qc_ref, kc_ref, vc_ref, accc_ref), (ql_ref, kl_ref, vl_ref, accl_ref))
    for d in range(2):
        st_ref[...] = jnp.zeros_like(st_ref)
        for q_ref, k_ref, v_ref, acc_ref in segs:
            def tile(i, q_ref=q_ref, k_ref=k_ref, v_ref=v_ref, acc_ref=acc_ref):
                rows = pl.ds(pl.multiple_of(i * t, t), t)
                q, k, v = q_ref[0, rows, :], k_ref[0, rows, :], v_ref[0, rows, :]
                att = lax.dot_general(_stack_heads(q), k, _NT, preferred_element_type=F32) * dmask_ref[d]
                intra = _unstack_heads(jnp.dot(att.astype(BF16), v, preferred_element_type=F32), t)
                cross = lax.dot_general(q, st_ref[...].astype(BF16), _NT, preferred_element_type=F32)
                o = intra + cross * xi_ref[d]
                if d == 0:
                    acc_ref[rows, :] = o
                else:
                    acc_ref[rows, :] += o
                _state_update(st_ref, gam_ref[d], v, k.astype(F32) * zeta_ref[d], havg_ref[...])
            _for_tiles(q_ref.shape[1] // t, d == 1, tile)
    for acc_ref, g_ref, o_ref in ((accc_ref, gc_ref, oc_ref), (accl_ref, gl_ref, ol_ref)):
        o = acc_ref[...]
        o = o * lax.rsqrt(_head_mean_sq(o, havg_ref[...]) + EPS)
        o_ref[0] = (o * jax.nn.silu(g_ref[0].astype(F32))).astype(o_ref.dtype)


def retention_tables(decay_logit):
    t = LA_TILE
    lg = jax.nn.log_sigmoid(decay_logit)
    idx = jnp.arange(t, dtype=F32)
    diff = idx[:, None] - idx[None, :]
    lgm = lg[:, :, None, None]
    fwd = jnp.where(diff >= 0, jnp.exp(jnp.maximum(diff, 0.0) * lgm[0]), 0.0)
    bwd = jnp.where(diff <= 0, jnp.exp(jnp.maximum(-diff, 0.0) * lgm[1]), 0.0)
    dmask = jnp.stack([fwd.reshape(N_HEADS * t, t), bwd.reshape(N_HEADS * t, t)])
    lane = lambda a: jnp.repeat(a, HEAD_DIM, axis=-1)
    lg_l = lane(lg)[:, None, :]
    steps_q = jnp.stack([idx + 1.0, t - idx])[:, :, None]
    steps_k = jnp.stack([t - 1.0 - idx, idx])[:, :, None]
    xi = jnp.exp(steps_q * lg_l)
    zeta = jnp.exp(steps_k * lg_l)
    gam = jnp.exp(t * lg_l)
    return dmask, xi, zeta, gam


def retention_pallas(rt_ctx, rt_lat, decay_logit):
    b, l, _ = rt_lat.shape
    lc = rt_ctx.shape[1]
    dmask, xi, zeta, gam = retention_tables(decay_logit)
    col = lambda n, c: pl.BlockSpec((1, n, BRANCH_W), lambda bi: (bi, 0, c))
    const = lambda a: pl.BlockSpec(a.shape, lambda bi: (0,) * a.ndim)
    havg = head_avg_matrix()
    return pl.pallas_call(
        _ret_kernel,
        grid=(b,),
        in_specs=[col(lc, c) for c in range(4)] + [col(l, c) for c in range(4)]
        + [const(dmask), const(xi), const(zeta), const(gam), const(havg)],
        out_specs=[col(lc, 0), col(l, 0)],
        out_shape=[jax.ShapeDtypeStruct((b, lc, BRANCH_W), BF16), jax.ShapeDtypeStruct((b, l, BRANCH_W), BF16)],
        scratch_shapes=[pltpu.VMEM((BRANCH_W, BRANCH_W), F32), pltpu.VMEM((lc, BRANCH_W), F32),
                        pltpu.VMEM((l, BRANCH_W), F32)],
        compiler_params=_params("parallel"),
        name="retention",
    )(rt_ctx, rt_ctx, rt_ctx, rt_ctx, rt_lat, rt_lat, rt_lat, rt_lat, dmask, xi, zeta, gam, havg)


HG_LEVELS = (32, 64, 128)


def hg_level_masks():
    t = LA_TILE
    i = (np.arange(N_HEADS * t) % t)[:, None]
    j = np.arange(t)[None, :]
    out = []
    for d in range(2):
        masks = [(i // HG_CHUNK == j // HG_CHUNK) & ((j <= i) if d == 0 else (j >= i))]
        for blk in HG_LEVELS:
            qi_late = (i % blk) >= blk // 2
            kj_late = (j % blk) >= blk // 2
            cross = (qi_late & ~kj_late) if d == 0 else (~qi_late & kj_late)
            masks.append((i // blk == j // blk) & cross)
        out.append(np.stack(masks))
    return jnp.asarray(np.stack(out), F32)


def _exact_rows_sum(sel, x):
    hi = x.astype(BF16)
    r1 = x - hi.astype(F32)
    mid = r1.astype(BF16)
    lo = (r1 - mid.astype(F32)).astype(BF16)
    return (jnp.dot(sel, hi, preferred_element_type=F32) + jnp.dot(sel, mid, preferred_element_type=F32)
            + jnp.dot(sel, lo, preferred_element_type=F32))


def _hg_tile(d, q, k, v, logf, st_ref, tri, mask_ref):
    t = LA_TILE
    w = q.shape[1]
    g = _exact_rows_sum(tri, logf)
    nb = t // HG_CHUNK
    g3 = g.reshape(nb, HG_CHUNK, w)
    if d == 0:
        edge = g3[:, HG_CHUNK - 1:HG_CHUNK, :]
        prev = jnp.concatenate([jnp.zeros((1, 1, w), F32), edge[:-1]], axis=0)
    else:
        edge = g3[:, 0:1, :]
        prev = jnp.concatenate([edge[1:], jnp.zeros((1, 1, w), F32)], axis=0)
    cum = (g3 - prev).reshape(t, w)
    qs = [q * jnp.exp(cum)]
    ks = [k * jnp.exp(-cum)]
    for blk in HG_LEVELS:
        gb = g.reshape(t // blk, blk, w)
        row = blk // 2 - 1 if d == 0 else blk // 2
        mid = jnp.broadcast_to(gb[:, row:row + 1, :], gb.shape).reshape(t, w)
        decay_to_mid = jnp.exp(-jnp.abs(g - mid))
        qs.append(q * decay_to_mid)
        ks.append(k * decay_to_mid)
    att = None
    for lvl, (qq, kk) in enumerate(zip(qs, ks)):
        a = lax.dot_general(_stack_heads(qq.astype(BF16)), kk.astype(BF16), _NT, preferred_element_type=F32)
        a = jnp.where(mask_ref[d, lvl] > 0.5, a, 0.0)
        att = a if att is None else att + a
    intra = _unstack_heads(jnp.dot(att.astype(BF16), v, preferred_element_type=F32), t)
    cross = lax.dot_general((q * jnp.exp(g)).astype(BF16), st_ref[...].astype(BF16), _NT,
                            preferred_element_type=F32)
    total = g[t - 1:t, :] if d == 0 else g[0:1, :]
    return intra + cross, jnp.exp(total), k * jnp.exp(total - g)


def _hg_kernel(pc_ref, pl_ref, lb_ref, nw_ref, havg_ref, mask_ref, oc_ref, ol_ref, st_ref, accc_ref, accl_ref):
    t = LA_TILE
    w = BRANCH_W
    lb = lb_ref[...]
    r = lax.broadcasted_iota(jnp.int32, (t, t), 0)
    c = lax.broadcasted_iota(jnp.int32, (t, t), 1)
    for d in range(2):
        tri = jnp.where((c <= r) if d == 0 else (c >= r), 1.0, 0.0).astype(BF16)
        st_ref[...] = jnp.zeros_like(st_ref)
        for p_ref, acc_ref in ((pc_ref, accc_ref), (pl_ref, accl_ref)):
            def tile(i, p_ref=p_ref, acc_ref=acc_ref):
                rows = pl.ds(pl.multiple_of(i * t, t), t)
                q = jax.nn.silu(p_ref[0, rows, 0:w].astype(F32))
                f_logit = p_ref[0, rows, (1 + d) * w:(2 + d) * w].astype(F32)
                v = p_ref[0, rows, 3 * w:4 * w]
                fg = lb + (1.0 - lb) * jax.nn.sigmoid(f_logit)
                o, decay, k_end = _hg_tile(d, q, 1.0 - fg, v, jnp.log(fg), st_ref, tri, mask_ref)
                if d == 0:
                    acc_ref[rows, :] = o
                else:
                    acc_ref[rows, :] += o
                _state_update(st_ref, decay, v, k_end, havg_ref[...])
            _for_tiles(p_ref.shape[1] // t, d == 1, tile)
    for acc_ref, p_ref, o_ref in ((accc_ref, pc_ref, oc_ref), (accl_ref, pl_ref, ol_ref)):
        o = acc_ref[...]
        o = o * lax.rsqrt(_head_mean_sq(o, havg_ref[...]) + EPS) * nw_ref[...]
        o_ref[0] = (o * jax.nn.silu(p_ref[0, :, 4 * w:5 * w].astype(F32))).astype(o_ref.dtype)


def hgrn2_pallas(hg_ctx, hg_lat, lower_bound, norm_w):
    b, l, width = hg_lat.shape
    lc = hg_ctx.shape[1]
    full = lambda n: pl.BlockSpec((1, n, width), lambda bi: (bi, 0, 0))
    out = lambda n: pl.BlockSpec((1, n, BRANCH_W), lambda bi: (bi, 0, 0))
    vec = pl.BlockSpec((1, BRANCH_W), lambda bi: (0, 0))
    havg = head_avg_matrix()
    masks = hg_level_masks()
    return pl.pallas_call(
        _hg_kernel,
        grid=(b,),
        in_specs=[full(lc), full(l), vec, vec, pl.BlockSpec(havg.shape, lambda bi: (0, 0)),
                  pl.BlockSpec(masks.shape, lambda bi: (0, 0, 0, 0))],
        out_specs=[out(lc), out(l)],
        out_shape=[jax.ShapeDtypeStruct((b, lc, BRANCH_W), BF16), jax.ShapeDtypeStruct((b, l, BRANCH_W), BF16)],
        scratch_shapes=[pltpu.VMEM((BRANCH_W, BRANCH_W), F32), pltpu.VMEM((lc, BRANCH_W), F32),
                        pltpu.VMEM((l, BRANCH_W), F32)],
        compiler_params=_params("parallel"),
        name="hgrn2",
    )(hg_ctx, hg_lat, lower_bound[None], jnp.tile(norm_w, N_HEADS)[None], havg, masks)


def kernel(x, c, ctx, c_ctx, ada_w, ada_b, norm_mix_w, norm_ffn_w, w_in, s5_lam_re, s5_lam_im, s5_log_step,
           s5_b_re, s5_b_im, s5_c_re, s5_c_im, s5_d, s5_glu_w, na_q_norm, na_k_norm, na_rpb, hg_lower_bounds,
           hg_norm_w, ret_decay_logit, w_branch, w_out, router_w, ex_w_gate, ex_w_up, ex_w_down):
    b = x.shape[0]
    lb_p = jax.nn.softmax(hg_lower_bounds, axis=0)
    lower_bounds = jnp.cumsum(lb_p, axis=0) - lb_p[0]
    cond_rows = jnp.concatenate([c, jnp.broadcast_to(c_ctx[None], c.shape)], axis=0)
    mods = ada_modulation(cond_rows, ada_w, ada_b, D_MODEL)
    xc = ctx
    for li in range(DEPTH):
        last = li == DEPTH - 1
        mod_l = [m[:, None] for m in jnp.split(mods[li, :b], 6, axis=-1)]
        mod_c = [m[:, None] for m in jnp.split(mods[li, b:], 6, axis=-1)]
        sh1_l, sc1_l, g1_l, sh2_l, sc2_l, g2_l = mod_l
        sh1_c, sc1_c, g1_c, sh2_c, sc2_c, g2_c = mod_c
        w_in_b = w_in[li].astype(BF16)
        wb_b = w_branch[li].astype(BF16)
        wo_b = w_out[li].astype(BF16)
        rw_t = router_w[li].T
        nmw = norm_mix_w[li][None]
        nfw = norm_ffn_w[li][None]

        rt_off = sum(IN_SPLITS[:3])
        w_rot_b = jnp.concatenate(
            [_swap_head_halves_cols(w_in[li][:, rt_off + i * BRANCH_W:rt_off + (i + 1) * BRANCH_W]) for i in range(2)],
            axis=1).astype(BF16)
        qk_w = jnp.stack([jnp.tile(na_q_norm[li], N_HEADS), jnp.tile(na_k_norm[li], N_HEADS)])
        s5_ops = s5_operators(s5_lam_re[li], s5_lam_im[li], s5_log_step[li], s5_b_re[li], s5_b_im[li],
                              s5_c_re[li], s5_c_im[li])
        s5_dl = s5_d[li][None]
        glu_b = s5_glu_w[li].astype(BF16)

        *pl_, ex_wg, ex_wu, ex_wd = in_proj(x, nmw, sh1_l, sc1_l, w_in_b, w_rot_b, qk_w, True, IN_PROJ_ROWS,
                                            (ex_w_gate, ex_w_up, ex_w_down), li)
        pc_ = in_proj(xc, nmw, sh1_c, sc1_c, w_in_b, w_rot_b, qk_w, False, min(xc.shape[1], IN_PROJ_ROWS))
        y5_c, y5_l = s5_scan_readout(pc_[0], pl_[0], s5_ops)
        na_l = na_latent(pl_[1], pc_[1], na_bias(na_rpb[li], x.shape[1] // GRID_W))
        hg_c, hg_l = hgrn2_pallas(pc_[2], pl_[2], lower_bounds[li], hg_norm_w[li])
        rt_c, rt_l = retention_pallas(pc_[3], pl_[3], ret_decay_logit[li])

        x, h_l, aff_l = merge_and_route(y5_l, pl_[0], (na_l, hg_l, rt_l), pl_[4], s5_dl, glu_b, wb_b, wo_b, x, g1_l,
                                        nfw, sh2_l, sc2_l, rw_t, MERGE_ROWS)
        ex_w = (ex_wg, ex_wu, ex_wd)
        x = expert_choice_ffn(x, g2_l, h_l, aff_l, *ex_w)
        if not last:
            na_c = na_context(pc_[1])
            xc, h_c, aff_c = merge_and_route(y5_c, pc_[0], (na_c, hg_c, rt_c), pc_[4], s5_dl, glu_b, wb_b, wo_b, xc,
                                             g1_c, nfw, sh2_c, sc2_c, rw_t, min(xc.shape[1], MERGE_ROWS))
            xc = expert_choice_ffn(xc, g2_c, h_c, aff_c, *ex_w)
    return x
```

```python
import functools
import math

import jax
import jax.numpy as jnp
import numpy as np
from jax import lax
from jax.experimental import pallas as pl
from jax.experimental.pallas import tpu as pltpu

D_MODEL = 1024
DEPTH = 2
GRID_W = 64
N_BRANCH = 4
BRANCH_W = 256
HEAD_DIM = 64
N_HEADS = BRANCH_W // HEAD_DIM
S5_GROUP = 16
S5_GROUPS = BRANCH_W // S5_GROUP
S5_STATE = 64
NA_ROWS = 8
NA_COLS = 16
HG_CHUNK = 16
RET_CHUNK = 128
N_EXPERTS = 16
EC_CAPACITY = 2
D_EXPERT = 2816
ROPE_BASE = 10000.0
EPS = 1e-6
IN_SPLITS = (BRANCH_W, 3 * BRANCH_W, 5 * BRANCH_W, 4 * BRANCH_W, N_BRANCH * D_MODEL)
D_IN = sum(IN_SPLITS)

F32 = jnp.float32
BF16 = jnp.bfloat16

V7X_VMEM_BYTES = 64 * 1024 * 1024
VMEM_LIMIT = V7X_VMEM_BYTES - 8 * 1024 * 1024


def _params(*sem):
    return pltpu.CompilerParams(dimension_semantics=sem, vmem_limit_bytes=VMEM_LIMIT)


def _norm_mod(x, norm_w, shift, scale):
    y = x * lax.rsqrt(jnp.mean(x * x, axis=-1, keepdims=True) + EPS) * norm_w
    return y * (1.0 + scale) + shift


def _ada_kernel(c_ref, w_ref, b_ref, o_ref):
    cond = jax.nn.silu(c_ref[...])
    o_ref[0] = jnp.dot(cond, w_ref[0], precision=lax.Precision.HIGHEST, preferred_element_type=F32) + b_ref[0]


def ada_modulation(cond_in, ada_w, ada_b, tn):
    r, d = cond_in.shape
    depth, _, n = ada_w.shape
    return pl.pallas_call(
        _ada_kernel,
        grid=(depth, n // tn),
        in_specs=[
            pl.BlockSpec((r, d), lambda l, j: (0, 0)),
            pl.BlockSpec((1, d, tn), lambda l, j: (l, 0, j)),
            pl.BlockSpec((1, 1, tn), lambda l, j: (l, 0, j)),
        ],
        out_specs=pl.BlockSpec((1, r, tn), lambda l, j: (l, 0, j)),
        out_shape=jax.ShapeDtypeStruct((depth, r, n), F32),
        compiler_params=_params("parallel", "parallel"),
        name="ada_modulation",
    )(cond_in, ada_w, ada_b[:, None, :])


def _head_mean_sq(t, head_avg):
    sq = t * t
    hi = sq.astype(BF16)
    lo = (sq - hi.astype(F32)).astype(BF16)
    return (jnp.dot(hi, head_avg, preferred_element_type=F32)
            + jnp.dot(lo, head_avg, preferred_element_type=F32))


def _in_proj_kernel(x_ref, nw_ref, sh_ref, sc_ref, w_ref, wrot_ref, qkw_ref, havg_ref, cos_ref, sin_ref,
                    *refs, rope, n_cast):
    cast_in, (s5_ref, na_ref, hg_ref, rt_ref, gate_ref), cast_out = (
        refs[:n_cast], refs[n_cast:n_cast + 5], refs[n_cast + 5:])
    for src, dst in zip(cast_in, cast_out):
        dst[0] = src[0, 0].astype(BF16)
    hb = _norm_mod(x_ref[0], nw_ref[...], sh_ref[0], sc_ref[0]).astype(BF16)
    cw = BRANCH_W
    proj = lambda col: jnp.dot(hb, w_ref[:, col * cw:(col + 1) * cw], preferred_element_type=F32)
    col = 0
    s5_ref[0] = proj(col).astype(BF16)
    col += 1
    for part in range(3):
        t = proj(col + part)
        if part < 2:
            t = t * lax.rsqrt(_head_mean_sq(t, havg_ref[...]) + EPS) * qkw_ref[part:part + 1, :]
        if part == 0:
            t = t * (HEAD_DIM ** -0.5)
        na_ref[0, :, part * cw:(part + 1) * cw] = t.astype(BF16)
    col += 3
    for part in range(5):
        hg_ref[0, :, part * cw:(part + 1) * cw] = proj(col + part).astype(BF16)
    col += 5
    for part in range(4):
        t = proj(col + part)
        if part < 2 and rope:
            swapped = jnp.dot(hb, wrot_ref[:, part * cw:(part + 1) * cw], preferred_element_type=F32)
            t = t * cos_ref[...] + swapped * sin_ref[...]
        if part == 1:
            t = t * (HEAD_DIM ** -0.5)
        rt_ref[0, :, part * cw:(part + 1) * cw] = t.astype(BF16)
    col += 4
    for part in range(N_BRANCH * D_MODEL // cw):
        gate_ref[0, :, part * cw:(part + 1) * cw] = proj(col + part).astype(BF16)


def rope_tables(n_tokens):
    quarter = HEAD_DIM // 4
    t = jnp.arange(n_tokens, dtype=jnp.int32)
    inv = jnp.asarray(ROPE_BASE ** (-np.arange(quarter) / quarter), F32)
    ang = jnp.concatenate([(t // GRID_W).astype(F32)[:, None] * inv, (t % GRID_W).astype(F32)[:, None] * inv], axis=1)
    cos, sin = jnp.cos(ang), jnp.sin(ang)
    cos_h = jnp.concatenate([cos, cos], axis=1)
    sin_h = jnp.concatenate([-sin, sin], axis=1)
    return jnp.tile(cos_h, (1, N_HEADS)), jnp.tile(sin_h, (1, N_HEADS))


def _swap_head_halves_cols(w):
    d, c = w.shape
    w = w.reshape(d, c // HEAD_DIM, 2, HEAD_DIM // 2)
    return w[:, :, ::-1, :].reshape(d, c)


def head_avg_matrix():
    h = np.arange(BRANCH_W) // HEAD_DIM
    return jnp.asarray((h[:, None] == h[None, :]) / HEAD_DIM, BF16)


def in_proj(x, norm_w, shift, scale, w_in_bf16, w_rot_bf16, qk_norm_w, rope, tm, cast_stacks=(), layer=0):
    b, n, d = x.shape
    nt = n // tm
    tok = lambda bi, i: (bi, i, 0)
    per_b = lambda bi, i: (bi, 0, 0)
    const2 = lambda bi, i: (0, 0)
    cos, sin = rope_tables(n)
    cast_in_specs, cast_out_specs, cast_out_shapes = [], [], []
    for w in cast_stacks:
        _, e, r, c = w.shape
        per_expert = (b * nt) // e
        rows = r // per_expert
        assert per_expert * e == b * nt and rows * per_expert == r and rows % 16 == 0, (w.shape, b, nt)
        cast_in_specs.append(pl.BlockSpec(
            (1, 1, rows, c), lambda bi, i, pe=per_expert: (layer, (bi * nt + i) // pe, (bi * nt + i) % pe, 0)))
        cast_out_specs.append(pl.BlockSpec(
            (1, rows, c), lambda bi, i, pe=per_expert: ((bi * nt + i) // pe, (bi * nt + i) % pe, 0)))
        cast_out_shapes.append(jax.ShapeDtypeStruct((e, r, c), BF16))
    return pl.pallas_call(
        functools.partial(_in_proj_kernel, rope=rope, n_cast=len(cast_stacks)),
        grid=(b, nt),
        in_specs=[
            pl.BlockSpec((1, tm, d), tok),
            pl.BlockSpec((1, d), const2),
            pl.BlockSpec((1, 1, d), per_b),
            pl.BlockSpec((1, 1, d), per_b),
            pl.BlockSpec((d, D_IN), const2, pipeline_mode=pl.Buffered(1)),
            pl.BlockSpec((d, 2 * BRANCH_W), const2, pipeline_mode=pl.Buffered(1)),
            pl.BlockSpec((2, BRANCH_W), const2),
            pl.BlockSpec((BRANCH_W, BRANCH_W), const2),
            pl.BlockSpec((tm, BRANCH_W), lambda bi, i: (i, 0)),
            pl.BlockSpec((tm, BRANCH_W), lambda bi, i: (i, 0)),
        ] + cast_in_specs,
        out_specs=[pl.BlockSpec((1, tm, w), tok) for w in IN_SPLITS] + cast_out_specs,
        out_shape=[jax.ShapeDtypeStruct((b, n, w), BF16) for w in IN_SPLITS] + cast_out_shapes,
        compiler_params=_params("parallel", "parallel"),
        name="in_proj",
    )(x, norm_w, shift, scale, w_in_bf16, w_rot_bf16, qk_norm_w, head_avg_matrix(), cos, sin, *cast_stacks)


NA_QROWS = 8
NA_KROWS = 16
NA_TQ = NA_QROWS * GRID_W
NA_TK = NA_KROWS * GRID_W
NA_MASKED = -1e30


def na_bias(rpb, rows):
    hp = lax.Precision.HIGHEST
    nblk = rows // NA_QROWS
    n_dr, n_dc = 2 * NA_ROWS - 1, 2 * NA_COLS - 1
    qc = np.arange(GRID_W)[:, None]
    kc = np.arange(GRID_W)[None, :]
    cs = np.clip(qc - NA_COLS // 2, 0, GRID_W - NA_COLS)
    col_ok = (kc >= cs) & (kc < cs + NA_COLS)
    dc = np.clip(kc - qc + NA_COLS - 1, 0, n_dc - 1).reshape(-1)
    sel_dc = jnp.asarray(dc[None, :] == np.arange(n_dc)[:, None], F32)
    by_col = jnp.einsum('hrc,cx->hrx', rpb.astype(F32), sel_dc, precision=hp)
    pats = []
    for g in (0, 1, nblk - 1):
        ks = int(np.clip(NA_QROWS * g - NA_ROWS // 2, 0, rows - NA_KROWS))
        qr = (NA_QROWS * g + np.arange(NA_QROWS))[:, None]
        kr = (ks + np.arange(NA_KROWS))[None, :]
        band = np.clip(qr - NA_ROWS // 2, 0, rows - NA_ROWS)
        row_ok = (kr >= band) & (kr < band + NA_ROWS)
        dr = np.clip(kr - qr + NA_ROWS - 1, 0, n_dr - 1).reshape(-1)
        sel_dr = jnp.asarray(dr[:, None] == np.arange(n_dr)[None, :], F32)
        t = jnp.einsum('yr,hrx->hyx', sel_dr, by_col, precision=hp)
        t = t.reshape(N_HEADS, NA_QROWS, NA_KROWS, GRID_W, GRID_W)
        t = jnp.transpose(t, (0, 1, 3, 2, 4)).reshape(N_HEADS, NA_TQ, NA_TK)
        ok = (row_ok[:, None, :, None] & col_ok[None, :, None, :]).reshape(NA_TQ, NA_TK)
        pats.append(jnp.where(jnp.asarray(ok)[None], t, NA_MASKED))
    return jnp.stack(pats)


def _head_lane_mask(h, shape):
    lane = lax.broadcasted_iota(jnp.int32, shape, len(shape) - 1)
    return (lane >= h * HEAD_DIM) & (lane < (h + 1) * HEAD_DIM)


def _na_kernel(q_ref, k_ref, v_ref, kc_ref, vc_ref, bias_ref, o_ref, *, rows):
    g = pl.program_id(1)
    ks = jnp.clip(NA_QROWS * g - NA_ROWS // 2, 0, rows - NA_KROWS)
    start = pl.multiple_of(ks * GRID_W, GRID_W * (NA_ROWS // 2))
    q = q_ref[0]
    k_win = k_ref[0, pl.ds(start, NA_TK), :]
    v_win = v_ref[0, pl.ds(start, NA_TK), :]
    kc, vc = kc_ref[0], vc_ref[0]
    nt = (((1,), (1,)), ((), ()))
    out = jnp.zeros(q.shape, F32)
    for h in range(N_HEADS):
        hm = _head_lane_mask(h, q.shape)
        qh = jnp.where(hm, q, jnp.zeros_like(q))
        s_loc = lax.dot_general(qh, k_win, nt, preferred_element_type=F32) + bias_ref[0, h]
        s_ctx = lax.dot_general(qh, kc, nt, preferred_element_type=F32)
        m = jnp.maximum(jnp.max(s_loc, axis=-1, keepdims=True), jnp.max(s_ctx, axis=-1, keepdims=True))
        p_loc = jnp.exp(s_loc - m)
        p_ctx = jnp.exp(s_ctx - m)
        denom = jnp.sum(p_loc, axis=-1, keepdims=True) + jnp.sum(p_ctx, axis=-1, keepdims=True)
        oh = (jnp.dot(p_loc.astype(BF16), v_win, preferred_element_type=F32)
              + jnp.dot(p_ctx.astype(BF16), vc, preferred_element_type=F32)) / denom
        out = jnp.where(hm, oh, out)
    o_ref[0] = out.astype(o_ref.dtype)


def na_latent(na_lat, na_ctx, bias):
    b, l, _ = na_lat.shape
    lc = na_ctx.shape[1]
    rows = l // GRID_W
    nblk = rows // NA_QROWS
    pat = lambda bi, g: (jnp.where(g == 0, 0, jnp.where(g == nblk - 1, 2, 1)), 0, 0, 0)
    return pl.pallas_call(
        functools.partial(_na_kernel, rows=rows),
        grid=(b, nblk),
        in_specs=[
            pl.BlockSpec((1, NA_TQ, BRANCH_W), lambda bi, g: (bi, g, 0)),
            pl.BlockSpec((1, l, BRANCH_W), lambda bi, g: (bi, 0, 1)),
            pl.BlockSpec((1, l, BRANCH_W), lambda bi, g: (bi, 0, 2)),
            pl.BlockSpec((1, lc, BRANCH_W), lambda bi, g: (bi, 0, 1)),
            pl.BlockSpec((1, lc, BRANCH_W), lambda bi, g: (bi, 0, 2)),
            pl.BlockSpec((1, N_HEADS, NA_TQ, NA_TK), pat),
        ],
        out_specs=pl.BlockSpec((1, NA_TQ, BRANCH_W), lambda bi, g: (bi, g, 0)),
        out_shape=jax.ShapeDtypeStruct((b, l, BRANCH_W), BF16),
        compiler_params=_params("parallel", "arbitrary"),
        name="na_latent",
    )(na_lat, na_lat, na_lat, na_ctx, na_ctx, bias)


def _na_ctx_kernel(q_ref, k_ref, v_ref, o_ref):
    q, k, v = q_ref[0], k_ref[0], v_ref[0]
    nt = (((1,), (1,)), ((), ()))
    out = jnp.zeros(q.shape, F32)
    for h in range(N_HEADS):
        hm = _head_lane_mask(h, q.shape)
        s = lax.dot_general(jnp.where(hm, q, jnp.zeros_like(q)), k, nt, preferred_element_type=F32)
        p = jnp.exp(s - jnp.max(s, axis=-1, keepdims=True))
        oh = jnp.dot(p.astype(BF16), v, preferred_element_type=F32) / jnp.sum(p, axis=-1, keepdims=True)
        out = jnp.where(hm, oh, out)
    o_ref[0] = out.astype(o_ref.dtype)


def na_context(na_ctx):
    b, lc, _ = na_ctx.shape
    spec = lambda col: pl.BlockSpec((1, lc, BRANCH_W), lambda bi: (bi, 0, col))
    return pl.pallas_call(
        _na_ctx_kernel,
        grid=(b,),
        in_specs=[spec(0), spec(1), spec(2)],
        out_specs=pl.BlockSpec((1, lc, BRANCH_W), lambda bi: (bi, 0, 0)),
        out_shape=jax.ShapeDtypeStruct((b, lc, BRANCH_W), BF16),
        compiler_params=_params("parallel"),
        name="na_context",
    )(na_ctx, na_ctx, na_ctx)


def _sigmoid_tanh(x):
    return 0.5 * jnp.tanh(0.5 * x) + 0.5


def _merge_kernel(y5_ref, u5_ref, o1_ref, o2_ref, o3_ref, gate_ref, d5_ref, wglu_ref, wb_ref, wo_ref, x_ref, g1_ref,
                  nw_ref, sh_ref, sc_ref, rw_ref, xo_ref, h_ref, aff_ref):
    z = jax.nn.gelu(y5_ref[0] + d5_ref[...] * u5_ref[0].astype(F32))
    o_s5 = (z * _sigmoid_tanh(jnp.dot(z.astype(BF16), wglu_ref[...], preferred_element_type=F32))).astype(BF16)
    m = None
    for br in range(N_BRANCH):
        o_br = o_s5 if br == 0 else (o1_ref, o2_ref, o3_ref)[br - 1][0]
        proj = jnp.dot(o_br, wb_ref[br], preferred_element_type=F32)
        gate = _sigmoid_tanh(gate_ref[0, :, br * D_MODEL:(br + 1) * D_MODEL].astype(F32))
        m = gate * proj if m is None else m + gate * proj
    mix = jnp.dot(m.astype(BF16), wo_ref[...], preferred_element_type=F32)
    x_new = x_ref[0] + g1_ref[0] * mix
    xo_ref[0] = x_new
    h = _norm_mod(x_new, nw_ref[...], sh_ref[0], sc_ref[0])
    h_ref[0] = h.astype(BF16)
    logits = lax.dot_general(rw_ref[...], h, (((1,), (1,)), ((), ())),
                             precision=lax.Precision.HIGHEST, preferred_element_type=F32)
    logits = logits - jnp.max(logits, axis=0, keepdims=True)
    e = jnp.exp(logits)
    aff_ref[0] = e / jnp.sum(e, axis=0, keepdims=True)


def merge_and_route(y_s5, u_s5, outs, gates, s5_d, s5_glu_bf16, w_branch_bf16, w_out_bf16, x, g1,
                    norm_w, shift, scale, router_w_t, tm):
    b, n, d = x.shape
    tok = lambda bi, i: (bi, i, 0)
    per_b = lambda bi, i: (bi, 0, 0)
    const2 = lambda bi, i: (0, 0)
    const3 = lambda bi, i: (0, 0, 0)
    return pl.pallas_call(
        _merge_kernel,
        grid=(b, n // tm),
        in_specs=[pl.BlockSpec((1, tm, BRANCH_W), tok)] * (N_BRANCH + 1) + [
            pl.BlockSpec((1, tm, N_BRANCH * d), tok),
            pl.BlockSpec((1, BRANCH_W), const2),
            pl.BlockSpec((BRANCH_W, BRANCH_W), const2),
            pl.BlockSpec((N_BRANCH, BRANCH_W, d), const3),
            pl.BlockSpec((d, d), const2),
            pl.BlockSpec((1, tm, d), tok),
            pl.BlockSpec((1, 1, d), per_b),
            pl.BlockSpec((1, d), const2),
            pl.BlockSpec((1, 1, d), per_b),
            pl.BlockSpec((1, 1, d), per_b),
            pl.BlockSpec((N_EXPERTS, d), const2),
        ],
        out_specs=[
            pl.BlockSpec((1, tm, d), tok),
            pl.BlockSpec((1, tm, d), tok),
            pl.BlockSpec((1, N_EXPERTS, tm), lambda bi, i: (bi, 0, i)),
        ],
        out_shape=[
            jax.ShapeDtypeStruct((b, n, d), F32),
            jax.ShapeDtypeStruct((b, n, d), BF16),
            jax.ShapeDtypeStruct((b, N_EXPERTS, n), F32),
        ],
        compiler_params=_params("parallel", "parallel"),
        name="merge_route",
    )(y_s5, u_s5, *outs, gates, s5_d, s5_glu_bf16, w_branch_bf16, w_out_bf16, x, g1, norm_w, shift, scale,
      router_w_t)


MXU_N = 256


EXPERT_ROWS = 512
def _expert_kernel(x_ref, g_ref, g2_ref, wg_ref, wu_ref, wd_ref, y_ref, act_ref):
    nb, cap, d = x_ref.shape[1:]
    xb = x_ref[0].reshape(nb * cap, d)
    f = wg_ref.shape[2]
    for c0 in range(0, f, MXU_N):
        c1 = min(c0 + MXU_N, f)
        gate = jnp.dot(xb, wg_ref[0, :, c0:c1], preferred_element_type=F32)
        up = jnp.dot(xb, wu_ref[0, :, c0:c1], preferred_element_type=F32)
        act_ref[:, c0:c1] = (jax.nn.silu(gate) * up).astype(BF16)
    y = jnp.dot(act_ref[...], wd_ref[0], preferred_element_type=F32).reshape(nb, cap, d)
    y_ref[0] = (y * g_ref[0] * g2_ref[...]).astype(y_ref.dtype)


def expert_ffn(xe, ge, g2, w_gate, w_up, w_down):
    e, b, cap, d = xe.shape
    f = w_gate.shape[-1]
    nb = max(1, min(b, EXPERT_ROWS // cap))
    tok = lambda ei, i: (ei, i, 0, 0)
    return pl.pallas_call(
        _expert_kernel,
        grid=(e, b // nb),
        in_specs=[
            pl.BlockSpec((1, nb, cap, d), tok),
            pl.BlockSpec((1, nb, cap, 1), tok),
            pl.BlockSpec((nb, 1, d), lambda ei, i: (i, 0, 0)),
            pl.BlockSpec((1, d, f), lambda ei, i: (ei, 0, 0)),
            pl.BlockSpec((1, d, f), lambda ei, i: (ei, 0, 0)),
            pl.BlockSpec((1, f, d), lambda ei, i: (ei, 0, 0)),
        ],
        out_specs=pl.BlockSpec((1, nb, cap, d), tok),
        out_shape=jax.ShapeDtypeStruct((e, b, cap, d), BF16),
        scratch_shapes=[pltpu.VMEM((nb * cap, f), BF16)],
        compiler_params=_params("parallel", "arbitrary"),
        name="expert_ffn",
    )(xe, ge, g2, w_gate, w_up, w_down)


COMBINE_GROUP = 8


def _combine_kernel(idx_ref, y_ref, x_hbm, o_ref, yf_ref, sem):
    bi = pl.program_id(0)
    e = pl.program_id(1)
    cap = y_ref.shape[2]

    @pl.when(e == 0)
    def _():
        cp = pltpu.make_async_copy(x_hbm.at[bi], o_ref.at[0], sem)
        cp.start()
        cp.wait()

    yf_ref[...] = y_ref[0, 0].astype(F32)

    def body(c, carry):
        base = pl.multiple_of(c * COMBINE_GROUP, COMBINE_GROUP)
        ys = yf_ref[pl.ds(base, COMBINE_GROUP), :]
        rows = [idx_ref[0, 0, e * cap + base + k] for k in range(COMBINE_GROUP)]
        new = [o_ref[0, pl.ds(rows[k], 1), :] + ys[k:k + 1, :] for k in range(COMBINE_GROUP)]
        for k in range(COMBINE_GROUP):
            o_ref[0, pl.ds(rows[k], 1), :] = new[k]
        return carry

    lax.fori_loop(0, cap // COMBINE_GROUP, body, 0)


def moe_combine(x, y, idx):
    b, n, d = x.shape
    e, _, cap, _ = y.shape
    return pl.pallas_call(
        _combine_kernel,
        grid=(b, e),
        in_specs=[
            pl.BlockSpec((1, 1, e * cap), lambda bi, ei: (bi, 0, 0), memory_space=pltpu.SMEM),
            pl.BlockSpec((1, 1, cap, d), lambda bi, ei: (ei, bi, 0, 0)),
            pl.BlockSpec(memory_space=pl.ANY),
        ],
        out_specs=pl.BlockSpec((1, n, d), lambda bi, ei: (bi, 0, 0)),
        out_shape=jax.ShapeDtypeStruct((b, n, d), F32),
        scratch_shapes=[pltpu.VMEM((cap, d), F32), pltpu.SemaphoreType.DMA(())],
        compiler_params=_params("parallel", "arbitrary"),
        name="moe_combine",
    )(idx, y, x)


def expert_choice_ffn(x, g2, h_bf16, aff_t, w_gate, w_up, w_down):
    b, n, d = h_bf16.shape
    cap = EC_CAPACITY * n // N_EXPERTS
    g, idx = lax.top_k(aff_t, cap)
    idx_e = jnp.moveaxis(idx, 1, 0)
    g_e = jnp.moveaxis(g, 1, 0)[..., None]
    xe = h_bf16[jnp.arange(b)[None, :, None], idx_e]
    y = expert_ffn(xe, g_e, g2, w_gate, w_up, w_down)
    return moe_combine(x, y, idx.reshape(b, 1, N_EXPERTS * cap).astype(jnp.int32))


S5_BLK = 16
S5_ROW = S5_GROUPS * S5_BLK * S5_GROUP
S5_PAIRS = S5_GROUPS // 2
S5_LANES = S5_GROUPS * S5_STATE


def _pair_blockdiag(t):
    g, r, c = t.shape
    t = t.reshape(g // 2, 2, r, c)
    z = jnp.zeros_like(t[:, 0])
    top = jnp.concatenate([t[:, 0], z], axis=-1)
    bot = jnp.concatenate([z, t[:, 1]], axis=-1)
    return jnp.concatenate([top, bot], axis=-2)


def s5_operators(lam_re, lam_im, log_step, b_re, b_im, c_re, c_im):
    hp = lax.Precision.HIGHEST
    blk = S5_BLK
    dt = jnp.exp(log_step)[..., None]
    k = jnp.arange(blk + 1, dtype=F32)
    mag = jnp.exp((lam_re * dt)[..., None] * k)
    ang = (lam_im * dt)[..., None] * k
    pr, pi = mag * jnp.cos(ang), mag * jnp.sin(ang)
    ar, ai = pr[..., 1], pi[..., 1]
    den = lam_re * lam_re + lam_im * lam_im
    zr = ((ar - 1.0) * lam_re + ai * lam_im) / den
    zi = (ai * lam_re - (ar - 1.0) * lam_im) / den
    bb_re = zr[..., None] * b_re - zi[..., None] * b_im
    bb_im = zr[..., None] * b_im + zi[..., None] * b_re
    ca_re = c_re[..., None] * pr[:, :, None] - c_im[..., None] * pi[:, :, None]
    ca_im = c_re[..., None] * pi[:, :, None] + c_im[..., None] * pr[:, :, None]
    kern = (jnp.einsum('dgpnl,dgnq->dglpq', ca_re, bb_re, precision=hp)
            - jnp.einsum('dgpnl,dgnq->dglpq', ca_im, bb_im, precision=hp))
    j = np.arange(blk)[:, None]
    i = np.arange(blk)[None, :]
    ms, ws, rres, rims = [], [], [], []
    for d in range(2):
        lag = (i - j) if d == 0 else (j - i)
        valid = jnp.asarray(lag >= 0, F32)[None, :, None, :, None]
        kd = kern[d][:, np.clip(lag, 0, blk - 1)]
        m = jnp.transpose(kd, (0, 1, 4, 2, 3)) * valid
        ms.append(m.reshape(S5_GROUPS, blk * S5_GROUP, blk * S5_GROUP))
        pw = (blk - 1 - np.arange(blk)) if d == 0 else np.arange(blk)
        apr, api = pr[d][..., pw], pi[d][..., pw]
        w_re = apr[..., None] * bb_re[d][:, :, None] - api[..., None] * bb_im[d][:, :, None]
        w_im = apr[..., None] * bb_im[d][:, :, None] + api[..., None] * bb_re[d][:, :, None]
        to_w = lambda t: jnp.transpose(t, (0, 2, 3, 1)).reshape(S5_GROUPS, blk * S5_GROUP, S5_STATE)
        ws.append((_pair_blockdiag(to_w(w_re)), _pair_blockdiag(to_w(w_im))))
        ex = (np.arange(blk) + 1) if d == 0 else (blk - np.arange(blk))
        r_re = ca_re[d][..., ex]
        r_im = -ca_im[d][..., ex]
        to_r = lambda t: jnp.transpose(t, (0, 2, 3, 1)).reshape(S5_GROUPS, S5_STATE, blk * S5_GROUP)
        rres.append(_pair_blockdiag(to_r(r_re)))
        rims.append(_pair_blockdiag(to_r(r_im)))
    m_op = jnp.stack(ms).astype(BF16)
    w_re = jnp.stack([w[0] for w in ws]).astype(BF16)
    w_im = jnp.stack([w[1] for w in ws]).astype(BF16)
    r_re = jnp.stack(rres).astype(BF16)
    r_im = jnp.stack(rims).astype(BF16)
    a_blk = jnp.stack([pr[..., blk].reshape(2, 1, S5_LANES), pi[..., blk].reshape(2, 1, S5_LANES)], axis=1)
    return m_op, w_re, w_im, r_re, r_im, a_blk


def _s5_kernel(uc_ref, ul_ref, m_ref, wre_ref, wim_ref, rre_ref, rim_ref, a_ref, yc_ref, yl_ref,
               vre, vim, sre, sim):
    rc, rl = uc_ref.shape[1], ul_ref.shape[1]
    segs = ((uc_ref, yc_ref, 0, rc), (ul_ref, yl_ref, rc, rl))
    gw = S5_BLK * S5_GROUP
    for d in range(2):
        for u_ref, _, base, rows in segs:
            for h in range(S5_PAIRS):
                u_pair = u_ref[0, :, 2 * h * gw:2 * (h + 1) * gw]
                vre[base:base + rows, h * 128:(h + 1) * 128] = jnp.dot(
                    u_pair, wre_ref[d, h], preferred_element_type=F32)
                vim[base:base + rows, h * 128:(h + 1) * 128] = jnp.dot(
                    u_pair, wim_ref[d, h], preferred_element_type=F32)
        ar, ai = a_ref[d, 0], a_ref[d, 1]

        def run(base, rows, carry):
            def step(t, c):
                xr, xi = c
                idx = base + (t if d == 0 else rows - 1 - t)
                sre[pl.ds(idx, 1), :] = xr
                sim[pl.ds(idx, 1), :] = xi
                nr = ar * xr - ai * xi + vre[pl.ds(idx, 1), :]
                ni = ar * xi + ai * xr + vim[pl.ds(idx, 1), :]
                return nr, ni
            return lax.fori_loop(0, rows, step, carry)

        zero = jnp.zeros((1, S5_LANES), F32)
        carry = run(0, rc, (zero, zero))
        run(rc, rl, carry)
        for u_ref, y_ref, base, rows in segs:
            for h in range(S5_PAIRS):
                s_r = sre[base:base + rows, h * 128:(h + 1) * 128].astype(BF16)
                s_i = sim[base:base + rows, h * 128:(h + 1) * 128].astype(BF16)
                y = (jnp.dot(s_r, rre_ref[d, h], preferred_element_type=F32)
                     + jnp.dot(s_i, rim_ref[d, h], preferred_element_type=F32))
                for gl in range(2):
                    g = 2 * h + gl
                    yg = y[:, gl * gw:(gl + 1) * gw] + jnp.dot(
                        u_ref[0, :, g * gw:(g + 1) * gw], m_ref[d, g], preferred_element_type=F32)
                    if d == 0:
                        y_ref[0, :, g * gw:(g + 1) * gw] = yg
                    else:
                        y_ref[0, :, g * gw:(g + 1) * gw] += yg


def _to_s5_rows(u):
    b, n, _ = u.shape
    u = u.reshape(b, n // S5_BLK, S5_BLK, S5_GROUPS, S5_GROUP)
    return jnp.transpose(u, (0, 1, 3, 2, 4)).reshape(b, n // S5_BLK, S5_ROW)


def _from_s5_rows(y):
    b, r, _ = y.shape
    y = y.reshape(b, r, S5_GROUPS, S5_BLK, S5_GROUP)
    return jnp.transpose(y, (0, 1, 3, 2, 4)).reshape(b, r * S5_BLK, BRANCH_W)


def s5_scan_readout(u_ctx, u_lat, ops):
    m_op, w_re, w_im, r_re, r_im, a_blk = ops
    b = u_lat.shape[0]
    uc, ul = _to_s5_rows(u_ctx), _to_s5_rows(u_lat)
    rc, rl = uc.shape[1], ul.shape[1]
    per_b = lambda bi: (bi, 0, 0)
    c4 = lambda bi: (0, 0, 0, 0)
    yc, yl = pl.pallas_call(
        _s5_kernel,
        grid=(b,),
        in_specs=[
            pl.BlockSpec((1, rc, S5_ROW), per_b),
            pl.BlockSpec((1, rl, S5_ROW), per_b),
            pl.BlockSpec(m_op.shape, c4),
            pl.BlockSpec(w_re.shape, c4),
            pl.BlockSpec(w_im.shape, c4),
            pl.BlockSpec(r_re.shape, c4),
            pl.BlockSpec(r_im.shape, c4),
            pl.BlockSpec(a_blk.shape, c4),
        ],
        out_specs=[pl.BlockSpec((1, rc, S5_ROW), per_b), pl.BlockSpec((1, rl, S5_ROW), per_b)],
        out_shape=[jax.ShapeDtypeStruct((b, rc, S5_ROW), F32), jax.ShapeDtypeStruct((b, rl, S5_ROW), F32)],
        scratch_shapes=[pltpu.VMEM((rc + rl, S5_LANES), F32) for _ in range(4)],
        compiler_params=_params("parallel"),
        name="s5_scan",
    )(uc, ul, m_op, w_re, w_im, r_re, r_im, a_blk)
    return _from_s5_rows(yc), _from_s5_rows(yl)


LA_TILE = 128
LA_UNROLL = 4


def _stack_heads(q):
    return jnp.concatenate(
        [jnp.where(_head_lane_mask(h, q.shape), q, jnp.zeros_like(q)) for h in range(N_HEADS)], axis=0)


def _unstack_heads(o_stack, t):
    out = jnp.zeros((t, o_stack.shape[1]), o_stack.dtype)
    for h in range(N_HEADS):
        blk = o_stack[h * t:(h + 1) * t]
        out = jnp.where(_head_lane_mask(h, blk.shape), blk, out)
    return out


_NT = (((1,), (1,)), ((), ()))


def _state_update(st_ref, decay_lane, v, k_scaled, head_avg):
    vt = jnp.transpose(v.astype(F32)).astype(BF16)
    kv = jnp.dot(vt, k_scaled.astype(BF16), preferred_element_type=F32)
    st_ref[...] = decay_lane * st_ref[...] + jnp.where(head_avg > 0, kv, 0.0)


def _for_tiles(n_tiles, reverse, body):
    def step(i, carry):
        body(n_tiles - 1 - i if reverse else i)
        return carry
    lax.fori_loop(0, n_tiles, step, 0, unroll=LA_UNROLL)


def _ret_kernel(qc_ref, kc_ref, vc_ref, gc_ref, ql_ref, kl_ref, vl_ref, gl_ref, dmask_ref, xi_ref, zeta_ref,
                gam_ref, havg_ref, oc_ref, ol_ref, st_ref, accc_ref, accl_ref):
    t = LA_TILE
    segs = ((qc_ref, kc_ref, vc_ref, accc_ref), (ql_ref, kl_ref, vl_ref, accl_ref))
    for d in range(2):
        st_ref[...] = jnp.zeros_like(st_ref)
        for q_ref, k_ref, v_ref, acc_ref in segs:
            def tile(i, q_ref=q_ref, k_ref=k_ref, v_ref=v_ref, acc_ref=acc_ref):
                rows = pl.ds(pl.multiple_of(i * t, t), t)
                q, k, v = q_ref[0, rows, :], k_ref[0, rows, :], v_ref[0, rows, :]
                att = lax.dot_general(_stack_heads(q), k, _NT, preferred_element_type=F32) * dmask_ref[d]
                intra = _unstack_heads(jnp.dot(att.astype(BF16), v, preferred_element_type=F32), t)
                cross = lax.dot_general(q, st_ref[...].astype(BF16), _NT, preferred_element_type=F32)
                o = intra + cross * xi_ref[d]
                if d == 0:
                    acc_ref[rows, :] = o
                else:
                    acc_ref[rows, :] += o
                _state_update(st_ref, gam_ref[d], v, k.astype(F32) * zeta_ref[d], havg_ref[...])
            _for_tiles(q_ref.shape[1] // t, d == 1, tile)
    for acc_ref, g_ref, o_ref in ((accc_ref, gc_ref, oc_ref), (accl_ref, gl_ref, ol_ref)):
        o = acc_ref[...]
        o = o * lax.rsqrt(_head_mean_sq(o, havg_ref[...]) + EPS)
        o_ref[0] = (o * jax.nn.silu(g_ref[0].astype(F32))).astype(o_ref.dtype)


def retention_tables(decay_logit):
    t = LA_TILE
    lg = jax.nn.log_sigmoid(decay_logit)
    idx = jnp.arange(t, dtype=F32)
    diff = idx[:, None] - idx[None, :]
    lgm = lg[:, :, None, None]
    fwd = jnp.where(diff >= 0, jnp.exp(jnp.maximum(diff, 0.0) * lgm[0]), 0.0)
    bwd = jnp.where(diff <= 0, jnp.exp(jnp.maximum(-diff, 0.0) * lgm[1]), 0.0)
    dmask = jnp.stack([fwd.reshape(N_HEADS * t, t), bwd.reshape(N_HEADS * t, t)])
    lane = lambda a: jnp.repeat(a, HEAD_DIM, axis=-1)
    lg_l = lane(lg)[:, None, :]
    steps_q = jnp.stack([idx + 1.0, t - idx])[:, :, None]
    steps_k = jnp.stack([t - 1.0 - idx, idx])[:, :, None]
    xi = jnp.exp(steps_q * lg_l)
    zeta = jnp.exp(steps_k * lg_l)
    gam = jnp.exp(t * lg_l)
    return dmask, xi, zeta, gam


def retention_pallas(rt_ctx, rt_lat, decay_logit):
    b, l, _ = rt_lat.shape
    lc = rt_ctx.shape[1]
    dmask, xi, zeta, gam = retention_tables(decay_logit)
    col = lambda n, c: pl.BlockSpec((1, n, BRANCH_W), lambda bi: (bi, 0, c))
    const = lambda a: pl.BlockSpec(a.shape, lambda bi: (0,) * a.ndim)
    havg = head_avg_matrix()
    return pl.pallas_call(
        _ret_kernel,
        grid=(b,),
        in_specs=[col(lc, c) for c in range(4)] + [col(l, c) for c in range(4)]
        + [const(dmask), const(xi), const(zeta), const(gam), const(havg)],
        out_specs=[col(lc, 0), col(l, 0)],
        out_shape=[jax.ShapeDtypeStruct((b, lc, BRANCH_W), BF16), jax.ShapeDtypeStruct((b, l, BRANCH_W), BF16)],
        scratch_shapes=[pltpu.VMEM((BRANCH_W, BRANCH_W), F32), pltpu.VMEM((lc, BRANCH_W), F32),
                        pltpu.VMEM((l, BRANCH_W), F32)],
        compiler_params=_params("parallel"),
        name="retention",
    )(rt_ctx, rt_ctx, rt_ctx, rt_ctx, rt_lat, rt_lat, rt_lat, rt_lat, dmask, xi, zeta, gam, havg)


HG_LEVELS = (32, 64, 128)


def hg_level_masks():
    t = LA_TILE
    i = (np.arange(N_HEADS * t) % t)[:, None]
    j = np.arange(t)[None, :]
    out = []
    for d in range(2):
        masks = [(i // HG_CHUNK == j // HG_CHUNK) & ((j <= i) if d == 0 else (j >= i))]
        for blk in HG_LEVELS:
            qi_late = (i % blk) >= blk // 2
            kj_late = (j % blk) >= blk // 2
            cross = (qi_late & ~kj_late) if d == 0 else (~qi_late & kj_late)
            masks.append((i // blk == j // blk) & cross)
        out.append(np.stack(masks))
    return jnp.asarray(np.stack(out), F32)


def _exact_rows_sum(sel, x):
    hi = x.astype(BF16)
    r1 = x - hi.astype(F32)
    mid = r1.astype(BF16)
    lo = (r1 - mid.astype(F32)).astype(BF16)
    return (jnp.dot(sel, hi, preferred_element_type=F32) + jnp.dot(sel, mid, preferred_element_type=F32)
            + jnp.dot(sel, lo, preferred_element_type=F32))


def _hg_tile(d, q, k, v, logf, st_ref, tri, mask_ref):
    t = LA_TILE
    w = q.shape[1]
    g = _exact_rows_sum(tri, logf)
    nb = t // HG_CHUNK
    g3 = g.reshape(nb, HG_CHUNK, w)
    if d == 0:
        edge = g3[:, HG_CHUNK - 1:HG_CHUNK, :]
        prev = jnp.concatenate([jnp.zeros((1, 1, w), F32), edge[:-1]], axis=0)
    else:
        edge = g3[:, 0:1, :]
        prev = jnp.concatenate([edge[1:], jnp.zeros((1, 1, w), F32)], axis=0)
    cum = (g3 - prev).reshape(t, w)
    qs = [q * jnp.exp(cum)]
    ks = [k * jnp.exp(-cum)]
    for blk in HG_LEVELS:
        gb = g.reshape(t // blk, blk, w)
        row = blk // 2 - 1 if d == 0 else blk // 2
        mid = jnp.broadcast_to(gb[:, row:row + 1, :], gb.shape).reshape(t, w)
        decay_to_mid = jnp.exp(-jnp.abs(g - mid))
        qs.append(q * decay_to_mid)
        ks.append(k * decay_to_mid)
    att = None
    for lvl, (qq, kk) in enumerate(zip(qs, ks)):
        a = lax.dot_general(_stack_heads(qq.astype(BF16)), kk.astype(BF16), _NT, preferred_element_type=F32)
        a = jnp.where(mask_ref[d, lvl] > 0.5, a, 0.0)
        att = a if att is None else att + a
    intra = _unstack_heads(jnp.dot(att.astype(BF16), v, preferred_element_type=F32), t)
    cross = lax.dot_general((q * jnp.exp(g)).astype(BF16), st_ref[...].astype(BF16), _NT,
                            preferred_element_type=F32)
    total = g[t - 1:t, :] if d == 0 else g[0:1, :]
    return intra + cross, jnp.exp(total), k * jnp.exp(total - g)


def _hg_kernel(pc_ref, pl_ref, lb_ref, nw_ref, havg_ref, mask_ref, oc_ref, ol_ref, st_ref, accc_ref, accl_ref):
    t = LA_TILE
    w = BRANCH_W
    lb = lb_ref[...]
    r = lax.broadcasted_iota(jnp.int32, (t, t), 0)
    c = lax.broadcasted_iota(jnp.int32, (t, t), 1)
    for d in range(2):
        tri = jnp.where((c <= r) if d == 0 else (c >= r), 1.0, 0.0).astype(BF16)
        st_ref[...] = jnp.zeros_like(st_ref)
        for p_ref, acc_ref in ((pc_ref, accc_ref), (pl_ref, accl_ref)):
            def tile(i, p_ref=p_ref, acc_ref=acc_ref):
                rows = pl.ds(pl.multiple_of(i * t, t), t)
                q = jax.nn.silu(p_ref[0, rows, 0:w].astype(F32))
                f_logit = p_ref[0, rows, (1 + d) * w:(2 + d) * w].astype(F32)
                v = p_ref[0, rows, 3 * w:4 * w]
                fg = lb + (1.0 - lb) * jax.nn.sigmoid(f_logit)
                o, decay, k_end = _hg_tile(d, q, 1.0 - fg, v, jnp.log(fg), st_ref, tri, mask_ref)
                if d == 0:
                    acc_ref[rows, :] = o
                else:
                    acc_ref[rows, :] += o
                _state_update(st_ref, decay, v, k_end, havg_ref[...])
            _for_tiles(p_ref.shape[1] // t, d == 1, tile)
    for acc_ref, p_ref, o_ref in ((accc_ref, pc_ref, oc_ref), (accl_ref, pl_ref, ol_ref)):
        o = acc_ref[...]
        o = o * lax.rsqrt(_head_mean_sq(o, havg_ref[...]) + EPS) * nw_ref[...]
        o_ref[0] = (o * jax.nn.silu(p_ref[0, :, 4 * w:5 * w].astype(F32))).astype(o_ref.dtype)


def hgrn2_pallas(hg_ctx, hg_lat, lower_bound, norm_w):
    b, l, width = hg_lat.shape
    lc = hg_ctx.shape[1]
    full = lambda n: pl.BlockSpec((1, n, width), lambda bi: (bi, 0, 0))
    out = lambda n: pl.BlockSpec((1, n, BRANCH_W), lambda bi: (bi, 0, 0))
    vec = pl.BlockSpec((1, BRANCH_W), lambda bi: (0, 0))
    havg = head_avg_matrix()
    masks = hg_level_masks()
    return pl.pallas_call(
        _hg_kernel,
        grid=(b,),
        in_specs=[full(lc), full(l), vec, vec, pl.BlockSpec(havg.shape, lambda bi: (0, 0)),
                  pl.BlockSpec(masks.shape, lambda bi: (0, 0, 0, 0))],
        out_specs=[out(lc), out(l)],
        out_shape=[jax.ShapeDtypeStruct((b, lc, BRANCH_W), BF16), jax.ShapeDtypeStruct((b, l, BRANCH_W), BF16)],
        scratch_shapes=[pltpu.VMEM((BRANCH_W, BRANCH_W), F32), pltpu.VMEM((lc, BRANCH_W), F32),
                        pltpu.VMEM((l, BRANCH_W), F32)],
        compiler_params=_params("parallel"),
        name="hgrn2",
    )(hg_ctx, hg_lat, lower_bound[None], jnp.tile(norm_w, N_HEADS)[None], havg, masks)


def kernel(x, c, ctx, c_ctx, ada_w, ada_b, norm_mix_w, norm_ffn_w, w_in, s5_lam_re, s5_lam_im, s5_log_step,
           s5_b_re, s5_b_im, s5_c_re, s5_c_im, s5_d, s5_glu_w, na_q_norm, na_k_norm, na_rpb, hg_lower_bounds,
           hg_norm_w, ret_decay_logit, w_branch, w_out, router_w, ex_w_gate, ex_w_up, ex_w_down):
    b = x.shape[0]
    lb_p = jax.nn.softmax(hg_lower_bounds, axis=0)
    lower_bounds = jnp.cumsum(lb_p, axis=0) - lb_p[0]
    cond_rows = jnp.concatenate([c, jnp.broadcast_to(c_ctx[None], c.shape)], axis=0)
    mods = ada_modulation(cond_rows, ada_w, ada_b, D_MODEL)
    xc = ctx
    for li in range(DEPTH):
        last = li == DEPTH - 1
        mod_l = [m[:, None] for m in jnp.split(mods[li, :b], 6, axis=-1)]
        mod_c = [m[:, None] for m in jnp.split(mods[li, b:], 6, axis=-1)]
        sh1_l, sc1_l, g1_l, sh2_l, sc2_l, g2_l = mod_l
        sh1_c, sc1_c, g1_c, sh2_c, sc2_c, g2_c = mod_c
        w_in_b = w_in[li].astype(BF16)
        wb_b = w_branch[li].astype(BF16)
        wo_b = w_out[li].astype(BF16)
        rw_t = router_w[li].T
        nmw = norm_mix_w[li][None]
        nfw = norm_ffn_w[li][None]

        rt_off = sum(IN_SPLITS[:3])
        w_rot_b = jnp.concatenate(
            [_swap_head_halves_cols(w_in[li][:, rt_off + i * BRANCH_W:rt_off + (i + 1) * BRANCH_W]) for i in range(2)],
            axis=1).astype(BF16)
        qk_w = jnp.stack([jnp.tile(na_q_norm[li], N_HEADS), jnp.tile(na_k_norm[li], N_HEADS)])
        s5_ops = s5_operators(s5_lam_re[li], s5_lam_im[li], s5_log_step[li], s5_b_re[li], s5_b_im[li],
                              s5_c_re[li], s5_c_im[li])
        s5_dl = s5_d[li][None]
        glu_b = s5_glu_w[li].astype(BF16)

        *pl_, ex_wg, ex_wu, ex_wd = in_proj(x, nmw, sh1_l, sc1_l, w_in_b, w_rot_b, qk_w, True, 256,
                                            (ex_w_gate, ex_w_up, ex_w_down), li)
        pc_ = in_proj(xc, nmw, sh1_c, sc1_c, w_in_b, w_rot_b, qk_w, False, 256)
        y5_c, y5_l = s5_scan_readout(pc_[0], pl_[0], s5_ops)
        na_l = na_latent(pl_[1], pc_[1], na_bias(na_rpb[li], x.shape[1] // GRID_W))
        hg_c, hg_l = hgrn2_pallas(pc_[2], pl_[2], lower_bounds[li], hg_norm_w[li])
        rt_c, rt_l = retention_pallas(pc_[3], pl_[3], ret_decay_logit[li])

        x, h_l, aff_l = merge_and_route(y5_l, pl_[0], (na_l, hg_l, rt_l), pl_[4], s5_dl, glu_b, wb_b, wo_b, x, g1_l,
                                        nfw, sh2_l, sc2_l, rw_t, 512)
        ex_w = (ex_wg, ex_wu, ex_wd)
        x = expert_choice_ffn(x, g2_l, h_l, aff_l, *ex_w)
        if not last:
            na_c = na_context(pc_[1])
            xc, h_c, aff_c = merge_and_route(y5_c, pc_[0], (na_c, hg_c, rt_c), pc_[4], s5_dl, glu_b, wb_b, wo_b, xc,
                                             g1_c, nfw, sh2_c, sc2_c, rw_t, 256)
            xc = expert_choice_ffn(xc, g2_c, h_c, aff_c, *ex_w)
    return x
```

```python
import functools
import math

import jax
import jax.numpy as jnp
import numpy as np
from jax import lax
from jax.experimental import pallas as pl
from jax.experimental.pallas import tpu as pltpu

D_MODEL = 1024
DEPTH = 2
GRID_W = 64
N_BRANCH = 4
BRANCH_W = 256
HEAD_DIM = 64
N_HEADS = BRANCH_W // HEAD_DIM
S5_GROUP = 16
S5_GROUPS = BRANCH_W // S5_GROUP
S5_STATE = 64
NA_ROWS = 8
NA_COLS = 16
HG_CHUNK = 16
RET_CHUNK = 128
N_EXPERTS = 16
EC_CAPACITY = 2
D_EXPERT = 2816
ROPE_BASE = 10000.0
EPS = 1e-6
IN_SPLITS = (BRANCH_W, 3 * BRANCH_W, 5 * BRANCH_W, 4 * BRANCH_W, N_BRANCH * D_MODEL)
D_IN = sum(IN_SPLITS)

F32 = jnp.float32
BF16 = jnp.bfloat16

V7X_VMEM_BYTES = 64 * 1024 * 1024
VMEM_LIMIT = V7X_VMEM_BYTES - 8 * 1024 * 1024


def _params(*sem):
    return pltpu.CompilerParams(dimension_semantics=sem, vmem_limit_bytes=VMEM_LIMIT)


def _norm_mod(x, norm_w, shift, scale):
    y = x * lax.rsqrt(jnp.mean(x * x, axis=-1, keepdims=True) + EPS) * norm_w
    return y * (1.0 + scale) + shift


def _ada_kernel(c_ref, w_ref, b_ref, o_ref):
    cond = jax.nn.silu(c_ref[...])
    o_ref[0] = jnp.dot(cond, w_ref[0], precision=lax.Precision.HIGHEST, preferred_element_type=F32) + b_ref[0]


def ada_modulation(cond_in, ada_w, ada_b, tn):
    r, d = cond_in.shape
    depth, _, n = ada_w.shape
    return pl.pallas_call(
        _ada_kernel,
        grid=(depth, n // tn),
        in_specs=[
            pl.BlockSpec((r, d), lambda l, j: (0, 0)),
            pl.BlockSpec((1, d, tn), lambda l, j: (l, 0, j)),
            pl.BlockSpec((1, 1, tn), lambda l, j: (l, 0, j)),
        ],
        out_specs=pl.BlockSpec((1, r, tn), lambda l, j: (l, 0, j)),
        out_shape=jax.ShapeDtypeStruct((depth, r, n), F32),
        compiler_params=_params("parallel", "parallel"),
        name="ada_modulation",
    )(cond_in, ada_w, ada_b[:, None, :])


def _head_mean_sq(t, head_avg):
    sq = t * t
    hi = sq.astype(BF16)
    lo = (sq - hi.astype(F32)).astype(BF16)
    return (jnp.dot(hi, head_avg, preferred_element_type=F32)
            + jnp.dot(lo, head_avg, preferred_element_type=F32))


def _in_proj_kernel(x_ref, nw_ref, sh_ref, sc_ref, w_ref, wrot_ref, qkw_ref, havg_ref, cos_ref, sin_ref,
                    *refs, rope, n_cast):
    cast_in, (s5_ref, na_ref, hg_ref, rt_ref, gate_ref), cast_out = (
        refs[:n_cast], refs[n_cast:n_cast + 5], refs[n_cast + 5:])
    for src, dst in zip(cast_in, cast_out):
        dst[0] = src[0, 0].astype(BF16)
    hb = _norm_mod(x_ref[0], nw_ref[...], sh_ref[0], sc_ref[0]).astype(BF16)
    cw = BRANCH_W
    proj = lambda col: jnp.dot(hb, w_ref[:, col * cw:(col + 1) * cw], preferred_element_type=F32)
    col = 0
    s5_ref[0] = proj(col).astype(BF16)
    col += 1
    for part in range(3):
        t = proj(col + part)
        if part < 2:
            t = t * lax.rsqrt(_head_mean_sq(t, havg_ref[...]) + EPS) * qkw_ref[part:part + 1, :]
        if part == 0:
            t = t * (HEAD_DIM ** -0.5)
        na_ref[0, :, part * cw:(part + 1) * cw] = t.astype(BF16)
    col += 3
    for part in range(5):
        hg_ref[0, :, part * cw:(part + 1) * cw] = proj(col + part).astype(BF16)
    col += 5
    for part in range(4):
        t = proj(col + part)
        if part < 2 and rope:
            swapped = jnp.dot(hb, wrot_ref[:, part * cw:(part + 1) * cw], preferred_element_type=F32)
            t = t * cos_ref[...] + swapped * sin_ref[...]
        if part == 1:
            t = t * (HEAD_DIM ** -0.5)
        rt_ref[0, :, part * cw:(part + 1) * cw] = t.astype(BF16)
    col += 4
    for part in range(N_BRANCH * D_MODEL // cw):
        gate_ref[0, :, part * cw:(part + 1) * cw] = proj(col + part).astype(BF16)


def rope_tables(n_tokens):
    quarter = HEAD_DIM // 4
    t = jnp.arange(n_tokens, dtype=jnp.int32)
    inv = jnp.asarray(ROPE_BASE ** (-np.arange(quarter) / quarter), F32)
    ang = jnp.concatenate([(t // GRID_W).astype(F32)[:, None] * inv, (t % GRID_W).astype(F32)[:, None] * inv], axis=1)
    cos, sin = jnp.cos(ang), jnp.sin(ang)
    cos_h = jnp.concatenate([cos, cos], axis=1)
    sin_h = jnp.concatenate([-sin, sin], axis=1)
    return jnp.tile(cos_h, (1, N_HEADS)), jnp.tile(sin_h, (1, N_HEADS))


def _swap_head_halves_cols(w):
    d, c = w.shape
    w = w.reshape(d, c // HEAD_DIM, 2, HEAD_DIM // 2)
    return w[:, :, ::-1, :].reshape(d, c)


def head_avg_matrix():
    h = np.arange(BRANCH_W) // HEAD_DIM
    return jnp.asarray((h[:, None] == h[None, :]) / HEAD_DIM, BF16)


def in_proj(x, norm_w, shift, scale, w_in_bf16, w_rot_bf16, qk_norm_w, rope, tm, cast_stacks=(), layer=0):
    b, n, d = x.shape
    nt = n // tm
    tok = lambda bi, i: (bi, i, 0)
    per_b = lambda bi, i: (bi, 0, 0)
    const2 = lambda bi, i: (0, 0)
    cos, sin = rope_tables(n)
    cast_in_specs, cast_out_specs, cast_out_shapes = [], [], []
    for w in cast_stacks:
        _, e, r, c = w.shape
        per_expert = (b * nt) // e
        rows = r // per_expert
        assert per_expert * e == b * nt and rows * per_expert == r and rows % 16 == 0, (w.shape, b, nt)
        cast_in_specs.append(pl.BlockSpec(
            (1, 1, rows, c), lambda bi, i, pe=per_expert: (layer, (bi * nt + i) // pe, (bi * nt + i) % pe, 0)))
        cast_out_specs.append(pl.BlockSpec(
            (1, rows, c), lambda bi, i, pe=per_expert: ((bi * nt + i) // pe, (bi * nt + i) % pe, 0)))
        cast_out_shapes.append(jax.ShapeDtypeStruct((e, r, c), BF16))
    return pl.pallas_call(
        functools.partial(_in_proj_kernel, rope=rope, n_cast=len(cast_stacks)),
        grid=(b, nt),
        in_specs=[
            pl.BlockSpec((1, tm, d), tok),
            pl.BlockSpec((1, d), const2),
            pl.BlockSpec((1, 1, d), per_b),
            pl.BlockSpec((1, 1, d), per_b),
            pl.BlockSpec((d, D_IN), const2, pipeline_mode=pl.Buffered(1)),
            pl.BlockSpec((d, 2 * BRANCH_W), const2, pipeline_mode=pl.Buffered(1)),
            pl.BlockSpec((2, BRANCH_W), const2),
            pl.BlockSpec((BRANCH_W, BRANCH_W), const2),
            pl.BlockSpec((tm, BRANCH_W), lambda bi, i: (i, 0)),
            pl.BlockSpec((tm, BRANCH_W), lambda bi, i: (i, 0)),
        ] + cast_in_specs,
        out_specs=[pl.BlockSpec((1, tm, w), tok) for w in IN_SPLITS] + cast_out_specs,
        out_shape=[jax.ShapeDtypeStruct((b, n, w), BF16) for w in IN_SPLITS] + cast_out_shapes,
        compiler_params=_params("parallel", "parallel"),
        name="in_proj",
    )(x, norm_w, shift, scale, w_in_bf16, w_rot_bf16, qk_norm_w, head_avg_matrix(), cos, sin, *cast_stacks)


NA_QROWS = 4
NA_KROWS = 12
NA_TQ = NA_QROWS * GRID_W
NA_TK = NA_KROWS * GRID_W
NA_MASKED = -1e30


def na_bias(rpb, rows):
    hp = lax.Precision.HIGHEST
    nblk = rows // NA_QROWS
    n_dr, n_dc = 2 * NA_ROWS - 1, 2 * NA_COLS - 1
    qc = np.arange(GRID_W)[:, None]
    kc = np.arange(GRID_W)[None, :]
    cs = np.clip(qc - NA_COLS // 2, 0, GRID_W - NA_COLS)
    col_ok = (kc >= cs) & (kc < cs + NA_COLS)
    dc = np.clip(kc - qc + NA_COLS - 1, 0, n_dc - 1).reshape(-1)
    sel_dc = jnp.asarray(dc[None, :] == np.arange(n_dc)[:, None], F32)
    by_col = jnp.einsum('hrc,cx->hrx', rpb.astype(F32), sel_dc, precision=hp)
    pats = []
    for g in (0, 1, nblk - 1):
        ks = int(np.clip(NA_QROWS * g - NA_ROWS // 2, 0, rows - NA_KROWS))
        qr = (NA_QROWS * g + np.arange(NA_QROWS))[:, None]
        kr = (ks + np.arange(NA_KROWS))[None, :]
        band = np.clip(qr - NA_ROWS // 2, 0, rows - NA_ROWS)
        row_ok = (kr >= band) & (kr < band + NA_ROWS)
        dr = np.clip(kr - qr + NA_ROWS - 1, 0, n_dr - 1).reshape(-1)
        sel_dr = jnp.asarray(dr[:, None] == np.arange(n_dr)[None, :], F32)
        t = jnp.einsum('yr,hrx->hyx', sel_dr, by_col, precision=hp)
        t = t.reshape(N_HEADS, NA_QROWS, NA_KROWS, GRID_W, GRID_W)
        t = jnp.transpose(t, (0, 1, 3, 2, 4)).reshape(N_HEADS, NA_TQ, NA_TK)
        ok = (row_ok[:, None, :, None] & col_ok[None, :, None, :]).reshape(NA_TQ, NA_TK)
        pats.append(jnp.where(jnp.asarray(ok)[None], t, NA_MASKED))
    return jnp.stack(pats)


def _head_lane_mask(h, shape):
    lane = lax.broadcasted_iota(jnp.int32, shape, len(shape) - 1)
    return (lane >= h * HEAD_DIM) & (lane < (h + 1) * HEAD_DIM)


def _na_kernel(q_ref, k_ref, v_ref, kc_ref, vc_ref, bias_ref, o_ref, *, rows):
    g = pl.program_id(1)
    ks = jnp.clip(NA_QROWS * g - NA_ROWS // 2, 0, rows - NA_KROWS)
    start = pl.multiple_of(ks * GRID_W, GRID_W * (NA_ROWS // 2))
    q = q_ref[0]
    k_win = k_ref[0, pl.ds(start, NA_TK), :]
    v_win = v_ref[0, pl.ds(start, NA_TK), :]
    kc, vc = kc_ref[0], vc_ref[0]
    nt = (((1,), (1,)), ((), ()))
    out = jnp.zeros(q.shape, F32)
    for h in range(N_HEADS):
        hm = _head_lane_mask(h, q.shape)
        qh = jnp.where(hm, q, jnp.zeros_like(q))
        s_loc = lax.dot_general(qh, k_win, nt, preferred_element_type=F32) + bias_ref[0, h]
        s_ctx = lax.dot_general(qh, kc, nt, preferred_element_type=F32)
        m = jnp.maximum(jnp.max(s_loc, axis=-1, keepdims=True), jnp.max(s_ctx, axis=-1, keepdims=True))
        p_loc = jnp.exp(s_loc - m)
        p_ctx = jnp.exp(s_ctx - m)
        denom = jnp.sum(p_loc, axis=-1, keepdims=True) + jnp.sum(p_ctx, axis=-1, keepdims=True)
        oh = (jnp.dot(p_loc.astype(BF16), v_win, preferred_element_type=F32)
              + jnp.dot(p_ctx.astype(BF16), vc, preferred_element_type=F32)) / denom
        out = jnp.where(hm, oh, out)
    o_ref[0] = out.astype(o_ref.dtype)


def na_latent(na_lat, na_ctx, bias):
    b, l, _ = na_lat.shape
    lc = na_ctx.shape[1]
    rows = l // GRID_W
    nblk = rows // NA_QROWS
    pat = lambda bi, g: (jnp.where(g == 0, 0, jnp.where(g == nblk - 1, 2, 1)), 0, 0, 0)
    return pl.pallas_call(
        functools.partial(_na_kernel, rows=rows),
        grid=(b, nblk),
        in_specs=[
            pl.BlockSpec((1, NA_TQ, BRANCH_W), lambda bi, g: (bi, g, 0)),
            pl.BlockSpec((1, l, BRANCH_W), lambda bi, g: (bi, 0, 1)),
            pl.BlockSpec((1, l, BRANCH_W), lambda bi, g: (bi, 0, 2)),
            pl.BlockSpec((1, lc, BRANCH_W), lambda bi, g: (bi, 0, 1)),
            pl.BlockSpec((1, lc, BRANCH_W), lambda bi, g: (bi, 0, 2)),
            pl.BlockSpec((1, N_HEADS, NA_TQ, NA_TK), pat),
        ],
        out_specs=pl.BlockSpec((1, NA_TQ, BRANCH_W), lambda bi, g: (bi, g, 0)),
        out_shape=jax.ShapeDtypeStruct((b, l, BRANCH_W), BF16),
        compiler_params=_params("parallel", "arbitrary"),
        name="na_latent",
    )(na_lat, na_lat, na_lat, na_ctx, na_ctx, bias)


def _na_ctx_kernel(q_ref, k_ref, v_ref, o_ref):
    q, k, v = q_ref[0], k_ref[0], v_ref[0]
    nt = (((1,), (1,)), ((), ()))
    out = jnp.zeros(q.shape, F32)
    for h in range(N_HEADS):
        hm = _head_lane_mask(h, q.shape)
        s = lax.dot_general(jnp.where(hm, q, jnp.zeros_like(q)), k, nt, preferred_element_type=F32)
        p = jnp.exp(s - jnp.max(s, axis=-1, keepdims=True))
        oh = jnp.dot(p.astype(BF16), v, preferred_element_type=F32) / jnp.sum(p, axis=-1, keepdims=True)
        out = jnp.where(hm, oh, out)
    o_ref[0] = out.astype(o_ref.dtype)


def na_context(na_ctx):
    b, lc, _ = na_ctx.shape
    spec = lambda col: pl.BlockSpec((1, lc, BRANCH_W), lambda bi: (bi, 0, col))
    return pl.pallas_call(
        _na_ctx_kernel,
        grid=(b,),
        in_specs=[spec(0), spec(1), spec(2)],
        out_specs=pl.BlockSpec((1, lc, BRANCH_W), lambda bi: (bi, 0, 0)),
        out_shape=jax.ShapeDtypeStruct((b, lc, BRANCH_W), BF16),
        compiler_params=_params("parallel"),
        name="na_context",
    )(na_ctx, na_ctx, na_ctx)


def _sigmoid_tanh(x):
    return 0.5 * jnp.tanh(0.5 * x) + 0.5


def _merge_kernel(y5_ref, u5_ref, o1_ref, o2_ref, o3_ref, gate_ref, d5_ref, wglu_ref, wb_ref, wo_ref, x_ref, g1_ref,
                  nw_ref, sh_ref, sc_ref, rw_ref, xo_ref, h_ref, aff_ref):
    z = jax.nn.gelu(y5_ref[0] + d5_ref[...] * u5_ref[0].astype(F32))
    o_s5 = (z * _sigmoid_tanh(jnp.dot(z.astype(BF16), wglu_ref[...], preferred_element_type=F32))).astype(BF16)
    m = None
    for br in range(N_BRANCH):
        o_br = o_s5 if br == 0 else (o1_ref, o2_ref, o3_ref)[br - 1][0]
        proj = jnp.dot(o_br, wb_ref[br], preferred_element_type=F32)
        gate = _sigmoid_tanh(gate_ref[0, :, br * D_MODEL:(br + 1) * D_MODEL].astype(F32))
        m = gate * proj if m is None else m + gate * proj
    mix = jnp.dot(m.astype(BF16), wo_ref[...], preferred_element_type=F32)
    x_new = x_ref[0] + g1_ref[0] * mix
    xo_ref[0] = x_new
    h = _norm_mod(x_new, nw_ref[...], sh_ref[0], sc_ref[0])
    h_ref[0] = h.astype(BF16)
    logits = lax.dot_general(rw_ref[...], h, (((1,), (1,)), ((), ())),
                             precision=lax.Precision.HIGHEST, preferred_element_type=F32)
    logits = logits - jnp.max(logits, axis=0, keepdims=True)
    e = jnp.exp(logits)
    aff_ref[0] = e / jnp.sum(e, axis=0, keepdims=True)


def merge_and_route(y_s5, u_s5, outs, gates, s5_d, s5_glu_bf16, w_branch_bf16, w_out_bf16, x, g1,
                    norm_w, shift, scale, router_w_t, tm):
    b, n, d = x.shape
    tok = lambda bi, i: (bi, i, 0)
    per_b = lambda bi, i: (bi, 0, 0)
    const2 = lambda bi, i: (0, 0)
    const3 = lambda bi, i: (0, 0, 0)
    return pl.pallas_call(
        _merge_kernel,
        grid=(b, n // tm),
        in_specs=[pl.BlockSpec((1, tm, BRANCH_W), tok)] * (N_BRANCH + 1) + [
            pl.BlockSpec((1, tm, N_BRANCH * d), tok),
            pl.BlockSpec((1, BRANCH_W), const2),
            pl.BlockSpec((BRANCH_W, BRANCH_W), const2),
            pl.BlockSpec((N_BRANCH, BRANCH_W, d), const3),
            pl.BlockSpec((d, d), const2),
            pl.BlockSpec((1, tm, d), tok),
            pl.BlockSpec((1, 1, d), per_b),
            pl.BlockSpec((1, d), const2),
            pl.BlockSpec((1, 1, d), per_b),
            pl.BlockSpec((1, 1, d), per_b),
            pl.BlockSpec((N_EXPERTS, d), const2),
        ],
        out_specs=[
            pl.BlockSpec((1, tm, d), tok),
            pl.BlockSpec((1, tm, d), tok),
            pl.BlockSpec((1, N_EXPERTS, tm), lambda bi, i: (bi, 0, i)),
        ],
        out_shape=[
            jax.ShapeDtypeStruct((b, n, d), F32),
            jax.ShapeDtypeStruct((b, n, d), BF16),
            jax.ShapeDtypeStruct((b, N_EXPERTS, n), F32),
        ],
        compiler_params=_params("parallel", "parallel"),
        name="merge_route",
    )(y_s5, u_s5, *outs, gates, s5_d, s5_glu_bf16, w_branch_bf16, w_out_bf16, x, g1, norm_w, shift, scale,
      router_w_t)


MXU_N = 256


EXPERT_ROWS = 512
def _expert_kernel(x_ref, g_ref, g2_ref, wg_ref, wu_ref, wd_ref, y_ref, act_ref):
    nb, cap, d = x_ref.shape[1:]
    xb = x_ref[0].reshape(nb * cap, d)
    f = wg_ref.shape[2]
    for c0 in range(0, f, MXU_N):
        c1 = min(c0 + MXU_N, f)
        gate = jnp.dot(xb, wg_ref[0, :, c0:c1], preferred_element_type=F32)
        up = jnp.dot(xb, wu_ref[0, :, c0:c1], preferred_element_type=F32)
        act_ref[:, c0:c1] = (jax.nn.silu(gate) * up).astype(BF16)
    y = jnp.dot(act_ref[...], wd_ref[0], preferred_element_type=F32).reshape(nb, cap, d)
    y_ref[0] = (y * g_ref[0] * g2_ref[...]).astype(y_ref.dtype)


def expert_ffn(xe, ge, g2, w_gate, w_up, w_down):
    e, b, cap, d = xe.shape
    f = w_gate.shape[-1]
    nb = max(1, min(b, EXPERT_ROWS // cap))
    tok = lambda ei, i: (ei, i, 0, 0)
    return pl.pallas_call(
        _expert_kernel,
        grid=(e, b // nb),
        in_specs=[
            pl.BlockSpec((1, nb, cap, d), tok),
            pl.BlockSpec((1, nb, cap, 1), tok),
            pl.BlockSpec((nb, 1, d), lambda ei, i: (i, 0, 0)),
            pl.BlockSpec((1, d, f), lambda ei, i: (ei, 0, 0)),
            pl.BlockSpec((1, d, f), lambda ei, i: (ei, 0, 0)),
            pl.BlockSpec((1, f, d), lambda ei, i: (ei, 0, 0)),
        ],
        out_specs=pl.BlockSpec((1, nb, cap, d), tok),
        out_shape=jax.ShapeDtypeStruct((e, b, cap, d), BF16),
        scratch_shapes=[pltpu.VMEM((nb * cap, f), BF16)],
        compiler_params=_params("parallel", "arbitrary"),
        name="expert_ffn",
    )(xe, ge, g2, w_gate, w_up, w_down)


COMBINE_GROUP = 8


def _combine_kernel(idx_ref, y_ref, x_hbm, o_ref, yf_ref, sem):
    bi = pl.program_id(0)
    e = pl.program_id(1)
    cap = y_ref.shape[2]

    @pl.when(e == 0)
    def _():
        cp = pltpu.make_async_copy(x_hbm.at[bi], o_ref.at[0], sem)
        cp.start()
        cp.wait()

    yf_ref[...] = y_ref[0, 0].astype(F32)

    def body(c, carry):
        base = pl.multiple_of(c * COMBINE_GROUP, COMBINE_GROUP)
        ys = yf_ref[pl.ds(base, COMBINE_GROUP), :]
        rows = [idx_ref[0, 0, e * cap + base + k] for k in range(COMBINE_GROUP)]
        new = [o_ref[0, pl.ds(rows[k], 1), :] + ys[k:k + 1, :] for k in range(COMBINE_GROUP)]
        for k in range(COMBINE_GROUP):
            o_ref[0, pl.ds(rows[k], 1), :] = new[k]
        return carry

    lax.fori_loop(0, cap // COMBINE_GROUP, body, 0)


def moe_combine(x, y, idx):
    b, n, d = x.shape
    e, _, cap, _ = y.shape
    return pl.pallas_call(
        _combine_kernel,
        grid=(b, e),
        in_specs=[
            pl.BlockSpec((1, 1, e * cap), lambda bi, ei: (bi, 0, 0), memory_space=pltpu.SMEM),
            pl.BlockSpec((1, 1, cap, d), lambda bi, ei: (ei, bi, 0, 0)),
            pl.BlockSpec(memory_space=pl.ANY),
        ],
        out_specs=pl.BlockSpec((1, n, d), lambda bi, ei: (bi, 0, 0)),
        out_shape=jax.ShapeDtypeStruct((b, n, d), F32),
        scratch_shapes=[pltpu.VMEM((cap, d), F32), pltpu.SemaphoreType.DMA(())],
        compiler_params=_params("parallel", "arbitrary"),
        name="moe_combine",
    )(idx, y, x)


def expert_choice_ffn(x, g2, h_bf16, aff_t, w_gate, w_up, w_down):
    b, n, d = h_bf16.shape
    cap = EC_CAPACITY * n // N_EXPERTS
    g, idx = lax.top_k(aff_t, cap)
    idx_e = jnp.moveaxis(idx, 1, 0)
    g_e = jnp.moveaxis(g, 1, 0)[..., None]
    xe = h_bf16[jnp.arange(b)[None, :, None], idx_e]
    y = expert_ffn(xe, g_e, g2, w_gate, w_up, w_down)
    return moe_combine(x, y, idx.reshape(b, 1, N_EXPERTS * cap).astype(jnp.int32))


S5_BLK = 16
S5_ROW = S5_GROUPS * S5_BLK * S5_GROUP
S5_PAIRS = S5_GROUPS // 2
S5_LANES = S5_GROUPS * S5_STATE


def _pair_blockdiag(t):
    g, r, c = t.shape
    t = t.reshape(g // 2, 2, r, c)
    z = jnp.zeros_like(t[:, 0])
    top = jnp.concatenate([t[:, 0], z], axis=-1)
    bot = jnp.concatenate([z, t[:, 1]], axis=-1)
    return jnp.concatenate([top, bot], axis=-2)


def s5_operators(lam_re, lam_im, log_step, b_re, b_im, c_re, c_im):
    hp = lax.Precision.HIGHEST
    blk = S5_BLK
    dt = jnp.exp(log_step)[..., None]
    k = jnp.arange(blk + 1, dtype=F32)
    mag = jnp.exp((lam_re * dt)[..., None] * k)
    ang = (lam_im * dt)[..., None] * k
    pr, pi = mag * jnp.cos(ang), mag * jnp.sin(ang)
    ar, ai = pr[..., 1], pi[..., 1]
    den = lam_re * lam_re + lam_im * lam_im
    zr = ((ar - 1.0) * lam_re + ai * lam_im) / den
    zi = (ai * lam_re - (ar - 1.0) * lam_im) / den
    bb_re = zr[..., None] * b_re - zi[..., None] * b_im
    bb_im = zr[..., None] * b_im + zi[..., None] * b_re
    ca_re = c_re[..., None] * pr[:, :, None] - c_im[..., None] * pi[:, :, None]
    ca_im = c_re[..., None] * pi[:, :, None] + c_im[..., None] * pr[:, :, None]
    kern = (jnp.einsum('dgpnl,dgnq->dglpq', ca_re, bb_re, precision=hp)
            - jnp.einsum('dgpnl,dgnq->dglpq', ca_im, bb_im, precision=hp))
    j = np.arange(blk)[:, None]
    i = np.arange(blk)[None, :]
    ms, ws, rres, rims = [], [], [], []
    for d in range(2):
        lag = (i - j) if d == 0 else (j - i)
        valid = jnp.asarray(lag >= 0, F32)[None, :, None, :, None]
        kd = kern[d][:, np.clip(lag, 0, blk - 1)]
        m = jnp.transpose(kd, (0, 1, 4, 2, 3)) * valid
        ms.append(m.reshape(S5_GROUPS, blk * S5_GROUP, blk * S5_GROUP))
        pw = (blk - 1 - np.arange(blk)) if d == 0 else np.arange(blk)
        apr, api = pr[d][..., pw], pi[d][..., pw]
        w_re = apr[..., None] * bb_re[d][:, :, None] - api[..., None] * bb_im[d][:, :, None]
        w_im = apr[..., None] * bb_im[d][:, :, None] + api[..., None] * bb_re[d][:, :, None]
        to_w = lambda t: jnp.transpose(t, (0, 2, 3, 1)).reshape(S5_GROUPS, blk * S5_GROUP, S5_STATE)
        ws.append((_pair_blockdiag(to_w(w_re)), _pair_blockdiag(to_w(w_im))))
        ex = (np.arange(blk) + 1) if d == 0 else (blk - np.arange(blk))
        r_re = ca_re[d][..., ex]
        r_im = -ca_im[d][..., ex]
        to_r = lambda t: jnp.transpose(t, (0, 2, 3, 1)).reshape(S5_GROUPS, S5_STATE, blk * S5_GROUP)
        rres.append(_pair_blockdiag(to_r(r_re)))
        rims.append(_pair_blockdiag(to_r(r_im)))
    m_op = jnp.stack(ms).astype(BF16)
    w_re = jnp.stack([w[0] for w in ws]).astype(BF16)
    w_im = jnp.stack([w[1] for w in ws]).astype(BF16)
    r_re = jnp.stack(rres).astype(BF16)
    r_im = jnp.stack(rims).astype(BF16)
    a_blk = jnp.stack([pr[..., blk].reshape(2, 1, S5_LANES), pi[..., blk].reshape(2, 1, S5_LANES)], axis=1)
    return m_op, w_re, w_im, r_re, r_im, a_blk


def _s5_kernel(uc_ref, ul_ref, m_ref, wre_ref, wim_ref, rre_ref, rim_ref, a_ref, yc_ref, yl_ref,
               vre, vim, sre, sim):
    rc, rl = uc_ref.shape[1], ul_ref.shape[1]
    segs = ((uc_ref, yc_ref, 0, rc), (ul_ref, yl_ref, rc, rl))
    gw = S5_BLK * S5_GROUP
    for d in range(2):
        for u_ref, _, base, rows in segs:
            for h in range(S5_PAIRS):
                u_pair = u_ref[0, :, 2 * h * gw:2 * (h + 1) * gw]
                vre[base:base + rows, h * 128:(h + 1) * 128] = jnp.dot(
                    u_pair, wre_ref[d, h], preferred_element_type=F32)
                vim[base:base + rows, h * 128:(h + 1) * 128] = jnp.dot(
                    u_pair, wim_ref[d, h], preferred_element_type=F32)
        ar, ai = a_ref[d, 0], a_ref[d, 1]

        def run(base, rows, carry):
            def step(t, c):
                xr, xi = c
                idx = base + (t if d == 0 else rows - 1 - t)
                sre[pl.ds(idx, 1), :] = xr
                sim[pl.ds(idx, 1), :] = xi
                nr = ar * xr - ai * xi + vre[pl.ds(idx, 1), :]
                ni = ar * xi + ai * xr + vim[pl.ds(idx, 1), :]
                return nr, ni
            return lax.fori_loop(0, rows, step, carry)

        zero = jnp.zeros((1, S5_LANES), F32)
        carry = run(0, rc, (zero, zero))
        run(rc, rl, carry)
        for u_ref, y_ref, base, rows in segs:
            for h in range(S5_PAIRS):
                s_r = sre[base:base + rows, h * 128:(h + 1) * 128].astype(BF16)
                s_i = sim[base:base + rows, h * 128:(h + 1) * 128].astype(BF16)
                y = (jnp.dot(s_r, rre_ref[d, h], preferred_element_type=F32)
                     + jnp.dot(s_i, rim_ref[d, h], preferred_element_type=F32))
                for gl in range(2):
                    g = 2 * h + gl
                    yg = y[:, gl * gw:(gl + 1) * gw] + jnp.dot(
                        u_ref[0, :, g * gw:(g + 1) * gw], m_ref[d, g], preferred_element_type=F32)
                    if d == 0:
                        y_ref[0, :, g * gw:(g + 1) * gw] = yg
                    else:
                        y_ref[0, :, g * gw:(g + 1) * gw] += yg


def _to_s5_rows(u):
    b, n, _ = u.shape
    u = u.reshape(b, n // S5_BLK, S5_BLK, S5_GROUPS, S5_GROUP)
    return jnp.transpose(u, (0, 1, 3, 2, 4)).reshape(b, n // S5_BLK, S5_ROW)


def _from_s5_rows(y):
    b, r, _ = y.shape
    y = y.reshape(b, r, S5_GROUPS, S5_BLK, S5_GROUP)
    return jnp.transpose(y, (0, 1, 3, 2, 4)).reshape(b, r * S5_BLK, BRANCH_W)


def s5_scan_readout(u_ctx, u_lat, ops):
    m_op, w_re, w_im, r_re, r_im, a_blk = ops
    b = u_lat.shape[0]
    uc, ul = _to_s5_rows(u_ctx), _to_s5_rows(u_lat)
    rc, rl = uc.shape[1], ul.shape[1]
    per_b = lambda bi: (bi, 0, 0)
    c4 = lambda bi: (0, 0, 0, 0)
    yc, yl = pl.pallas_call(
        _s5_kernel,
        grid=(b,),
        in_specs=[
            pl.BlockSpec((1, rc, S5_ROW), per_b),
            pl.BlockSpec((1, rl, S5_ROW), per_b),
            pl.BlockSpec(m_op.shape, c4),
            pl.BlockSpec(w_re.shape, c4),
            pl.BlockSpec(w_im.shape, c4),
            pl.BlockSpec(r_re.shape, c4),
            pl.BlockSpec(r_im.shape, c4),
            pl.BlockSpec(a_blk.shape, c4),
        ],
        out_specs=[pl.BlockSpec((1, rc, S5_ROW), per_b), pl.BlockSpec((1, rl, S5_ROW), per_b)],
        out_shape=[jax.ShapeDtypeStruct((b, rc, S5_ROW), F32), jax.ShapeDtypeStruct((b, rl, S5_ROW), F32)],
        scratch_shapes=[pltpu.VMEM((rc + rl, S5_LANES), F32) for _ in range(4)],
        compiler_params=_params("parallel"),
        name="s5_scan",
    )(uc, ul, m_op, w_re, w_im, r_re, r_im, a_blk)
    return _from_s5_rows(yc), _from_s5_rows(yl)


LA_TILE = 128
LA_UNROLL = 4


def _stack_heads(q):
    return jnp.concatenate(
        [jnp.where(_head_lane_mask(h, q.shape), q, jnp.zeros_like(q)) for h in range(N_HEADS)], axis=0)


def _unstack_heads(o_stack, t):
    out = jnp.zeros((t, o_stack.shape[1]), o_stack.dtype)
    for h in range(N_HEADS):
        blk = o_stack[h * t:(h + 1) * t]
        out = jnp.where(_head_lane_mask(h, blk.shape), blk, out)
    return out


_NT = (((1,), (1,)), ((), ()))


def _state_update(st_ref, decay_lane, v, k_scaled, head_avg):
    vt = jnp.transpose(v.astype(F32)).astype(BF16)
    kv = jnp.dot(vt, k_scaled.astype(BF16), preferred_element_type=F32)
    st_ref[...] = decay_lane * st_ref[...] + jnp.where(head_avg > 0, kv, 0.0)


def _for_tiles(n_tiles, reverse, body):
    def step(i, carry):
        body(n_tiles - 1 - i if reverse else i)
        return carry
    lax.fori_loop(0, n_tiles, step, 0, unroll=LA_UNROLL)


def _ret_kernel(qc_ref, kc_ref, vc_ref, gc_ref, ql_ref, kl_ref, vl_ref, gl_ref, dmask_ref, xi_ref, zeta_ref,
                gam_ref, havg_ref, oc_ref, ol_ref, st_ref, accc_ref, accl_ref):
    t = LA_TILE
    segs = ((qc_ref, kc_ref, vc_ref, accc_ref), (ql_ref, kl_ref, vl_ref, accl_ref))
    for d in range(2):
        st_ref[...] = jnp.zeros_like(st_ref)
        for q_ref, k_ref, v_ref, acc_ref in segs:
            def tile(i, q_ref=q_ref, k_ref=k_ref, v_ref=v_ref, acc_ref=acc_ref):
                rows = pl.ds(pl.multiple_of(i * t, t), t)
                q, k, v = q_ref[0, rows, :], k_ref[0, rows, :], v_ref[0, rows, :]
                att = lax.dot_general(_stack_heads(q), k, _NT, preferred_element_type=F32) * dmask_ref[d]
                intra = _unstack_heads(jnp.dot(att.astype(BF16), v, preferred_element_type=F32), t)
                cross = lax.dot_general(q, st_ref[...].astype(BF16), _NT, preferred_element_type=F32)
                o = intra + cross * xi_ref[d]
                if d == 0:
                    acc_ref[rows, :] = o
                else:
                    acc_ref[rows, :] += o
                _state_update(st_ref, gam_ref[d], v, k.astype(F32) * zeta_ref[d], havg_ref[...])
            _for_tiles(q_ref.shape[1] // t, d == 1, tile)
    for acc_ref, g_ref, o_ref in ((accc_ref, gc_ref, oc_ref), (accl_ref, gl_ref, ol_ref)):
        o = acc_ref[...]
        o = o * lax.rsqrt(_head_mean_sq(o, havg_ref[...]) + EPS)
        o_ref[0] = (o * jax.nn.silu(g_ref[0].astype(F32))).astype(o_ref.dtype)


def retention_tables(decay_logit):
    t = LA_TILE
    lg = jax.nn.log_sigmoid(decay_logit)
    idx = jnp.arange(t, dtype=F32)
    diff = idx[:, None] - idx[None, :]
    lgm = lg[:, :, None, None]
    fwd = jnp.where(diff >= 0, jnp.exp(jnp.maximum(diff, 0.0) * lgm[0]), 0.0)
    bwd = jnp.where(diff <= 0, jnp.exp(jnp.maximum(-diff, 0.0) * lgm[1]), 0.0)
    dmask = jnp.stack([fwd.reshape(N_HEADS * t, t), bwd.reshape(N_HEADS * t, t)])
    lane = lambda a: jnp.repeat(a, HEAD_DIM, axis=-1)
    lg_l = lane(lg)[:, None, :]
    steps_q = jnp.stack([idx + 1.0, t - idx])[:, :, None]
    steps_k = jnp.stack([t - 1.0 - idx, idx])[:, :, None]
    xi = jnp.exp(steps_q * lg_l)
    zeta = jnp.exp(steps_k * lg_l)
    gam = jnp.exp(t * lg_l)
    return dmask, xi, zeta, gam


def retention_pallas(rt_ctx, rt_lat, decay_logit):
    b, l, _ = rt_lat.shape
    lc = rt_ctx.shape[1]
    dmask, xi, zeta, gam = retention_tables(decay_logit)
    col = lambda n, c: pl.BlockSpec((1, n, BRANCH_W), lambda bi: (bi, 0, c))
    const = lambda a: pl.BlockSpec(a.shape, lambda bi: (0,) * a.ndim)
    havg = head_avg_matrix()
    return pl.pallas_call(
        _ret_kernel,
        grid=(b,),
        in_specs=[col(lc, c) for c in range(4)] + [col(l, c) for c in range(4)]
        + [const(dmask), const(xi), const(zeta), const(gam), const(havg)],
        out_specs=[col(lc, 0), col(l, 0)],
        out_shape=[jax.ShapeDtypeStruct((b, lc, BRANCH_W), BF16), jax.ShapeDtypeStruct((b, l, BRANCH_W), BF16)],
        scratch_shapes=[pltpu.VMEM((BRANCH_W, BRANCH_W), F32), pltpu.VMEM((lc, BRANCH_W), F32),
                        pltpu.VMEM((l, BRANCH_W), F32)],
        compiler_params=_params("parallel"),
        name="retention",
    )(rt_ctx, rt_ctx, rt_ctx, rt_ctx, rt_lat, rt_lat, rt_lat, rt_lat, dmask, xi, zeta, gam, havg)


HG_LEVELS = (32, 64, 128)


def hg_level_masks():
    t = LA_TILE
    i = (np.arange(N_HEADS * t) % t)[:, None]
    j = np.arange(t)[None, :]
    out = []
    for d in range(2):
        masks = [(i // HG_CHUNK == j // HG_CHUNK) & ((j <= i) if d == 0 else (j >= i))]
        for blk in HG_LEVELS:
            qi_late = (i % blk) >= blk // 2
            kj_late = (j % blk) >= blk // 2
            cross = (qi_late & ~kj_late) if d == 0 else (~qi_late & kj_late)
            masks.append((i // blk == j // blk) & cross)
        out.append(np.stack(masks))
    return jnp.asarray(np.stack(out), F32)


def _exact_rows_sum(sel, x):
    hi = x.astype(BF16)
    r1 = x - hi.astype(F32)
    mid = r1.astype(BF16)
    lo = (r1 - mid.astype(F32)).astype(BF16)
    return (jnp.dot(sel, hi, preferred_element_type=F32) + jnp.dot(sel, mid, preferred_element_type=F32)
            + jnp.dot(sel, lo, preferred_element_type=F32))


def _hg_tile(d, q, k, v, logf, st_ref, tri, mask_ref):
    t = LA_TILE
    w = q.shape[1]
    g = _exact_rows_sum(tri, logf)
    nb = t // HG_CHUNK
    g3 = g.reshape(nb, HG_CHUNK, w)
    if d == 0:
        edge = g3[:, HG_CHUNK - 1:HG_CHUNK, :]
        prev = jnp.concatenate([jnp.zeros((1, 1, w), F32), edge[:-1]], axis=0)
    else:
        edge = g3[:, 0:1, :]
        prev = jnp.concatenate([edge[1:], jnp.zeros((1, 1, w), F32)], axis=0)
    cum = (g3 - prev).reshape(t, w)
    qs = [q * jnp.exp(cum)]
    ks = [k * jnp.exp(-cum)]
    for blk in HG_LEVELS:
        gb = g.reshape(t // blk, blk, w)
        row = blk // 2 - 1 if d == 0 else blk // 2
        mid = jnp.broadcast_to(gb[:, row:row + 1, :], gb.shape).reshape(t, w)
        decay_to_mid = jnp.exp(-jnp.abs(g - mid))
        qs.append(q * decay_to_mid)
        ks.append(k * decay_to_mid)
    att = None
    for lvl, (qq, kk) in enumerate(zip(qs, ks)):
        a = lax.dot_general(_stack_heads(qq.astype(BF16)), kk.astype(BF16), _NT, preferred_element_type=F32)
        a = jnp.where(mask_ref[d, lvl] > 0.5, a, 0.0)
        att = a if att is None else att + a
    intra = _unstack_heads(jnp.dot(att.astype(BF16), v, preferred_element_type=F32), t)
    cross = lax.dot_general((q * jnp.exp(g)).astype(BF16), st_ref[...].astype(BF16), _NT,
                            preferred_element_type=F32)
    total = g[t - 1:t, :] if d == 0 else g[0:1, :]
    return intra + cross, jnp.exp(total), k * jnp.exp(total - g)


def _hg_kernel(pc_ref, pl_ref, lb_ref, nw_ref, havg_ref, mask_ref, oc_ref, ol_ref, st_ref, accc_ref, accl_ref):
    t = LA_TILE
    w = BRANCH_W
    lb = lb_ref[...]
    r = lax.broadcasted_iota(jnp.int32, (t, t), 0)
    c = lax.broadcasted_iota(jnp.int32, (t, t), 1)
    for d in range(2):
        tri = jnp.where((c <= r) if d == 0 else (c >= r), 1.0, 0.0).astype(BF16)
        st_ref[...] = jnp.zeros_like(st_ref)
        for p_ref, acc_ref in ((pc_ref, accc_ref), (pl_ref, accl_ref)):
            def tile(i, p_ref=p_ref, acc_ref=acc_ref):
                rows = pl.ds(pl.multiple_of(i * t, t), t)
                q = jax.nn.silu(p_ref[0, rows, 0:w].astype(F32))
                f_logit = p_ref[0, rows, (1 + d) * w:(2 + d) * w].astype(F32)
                v = p_ref[0, rows, 3 * w:4 * w]
                fg = lb + (1.0 - lb) * jax.nn.sigmoid(f_logit)
                o, decay, k_end = _hg_tile(d, q, 1.0 - fg, v, jnp.log(fg), st_ref, tri, mask_ref)
                if d == 0:
                    acc_ref[rows, :] = o
                else:
                    acc_ref[rows, :] += o
                _state_update(st_ref, decay, v, k_end, havg_ref[...])
            _for_tiles(p_ref.shape[1] // t, d == 1, tile)
    for acc_ref, p_ref, o_ref in ((accc_ref, pc_ref, oc_ref), (accl_ref, pl_ref, ol_ref)):
        o = acc_ref[...]
        o = o * lax.rsqrt(_head_mean_sq(o, havg_ref[...]) + EPS) * nw_ref[...]
        o_ref[0] = (o * jax.nn.silu(p_ref[0, :, 4 * w:5 * w].astype(F32))).astype(o_ref.dtype)


def hgrn2_pallas(hg_ctx, hg_lat, lower_bound, norm_w):
    b, l, width = hg_lat.shape
    lc = hg_ctx.shape[1]
    full = lambda n: pl.BlockSpec((1, n, width), lambda bi: (bi, 0, 0))
    out = lambda n: pl.BlockSpec((1, n, BRANCH_W), lambda bi: (bi, 0, 0))
    vec = pl.BlockSpec((1, BRANCH_W), lambda bi: (0, 0))
    havg = head_avg_matrix()
    masks = hg_level_masks()
    return pl.pallas_call(
        _hg_kernel,
        grid=(b,),
        in_specs=[full(lc), full(l), vec, vec, pl.BlockSpec(havg.shape, lambda bi: (0, 0)),
                  pl.BlockSpec(masks.shape, lambda bi: (0, 0, 0, 0))],
        out_specs=[out(lc), out(l)],
        out_shape=[jax.ShapeDtypeStruct((b, lc, BRANCH_W), BF16), jax.ShapeDtypeStruct((b, l, BRANCH_W), BF16)],
        scratch_shapes=[pltpu.VMEM((BRANCH_W, BRANCH_W), F32), pltpu.VMEM((lc, BRANCH_W), F32),
                        pltpu.VMEM((l, BRANCH_W), F32)],
        compiler_params=_params("parallel"),
        name="hgrn2",
    )(hg_ctx, hg_lat, lower_bound[None], jnp.tile(norm_w, N_HEADS)[None], havg, masks)


def kernel(x, c, ctx, c_ctx, ada_w, ada_b, norm_mix_w, norm_ffn_w, w_in, s5_lam_re, s5_lam_im, s5_log_step,
           s5_b_re, s5_b_im, s5_c_re, s5_c_im, s5_d, s5_glu_w, na_q_norm, na_k_norm, na_rpb, hg_lower_bounds,
           hg_norm_w, ret_decay_logit, w_branch, w_out, router_w, ex_w_gate, ex_w_up, ex_w_down):
    b = x.shape[0]
    lb_p = jax.nn.softmax(hg_lower_bounds, axis=0)
    lower_bounds = jnp.cumsum(lb_p, axis=0) - lb_p[0]
    cond_rows = jnp.concatenate([c, jnp.broadcast_to(c_ctx[None], c.shape)], axis=0)
    mods = ada_modulation(cond_rows, ada_w, ada_b, D_MODEL)
    xc = ctx
    for li in range(DEPTH):
        last = li == DEPTH - 1
        mod_l = [m[:, None] for m in jnp.split(mods[li, :b], 6, axis=-1)]
        mod_c = [m[:, None] for m in jnp.split(mods[li, b:], 6, axis=-1)]
        sh1_l, sc1_l, g1_l, sh2_l, sc2_l, g2_l = mod_l
        sh1_c, sc1_c, g1_c, sh2_c, sc2_c, g2_c = mod_c
        w_in_b = w_in[li].astype(BF16)
        wb_b = w_branch[li].astype(BF16)
        wo_b = w_out[li].astype(BF16)
        rw_t = router_w[li].T
        nmw = norm_mix_w[li][None]
        nfw = norm_ffn_w[li][None]

        rt_off = sum(IN_SPLITS[:3])
        w_rot_b = jnp.concatenate(
            [_swap_head_halves_cols(w_in[li][:, rt_off + i * BRANCH_W:rt_off + (i + 1) * BRANCH_W]) for i in range(2)],
            axis=1).astype(BF16)
        qk_w = jnp.stack([jnp.tile(na_q_norm[li], N_HEADS), jnp.tile(na_k_norm[li], N_HEADS)])
        s5_ops = s5_operators(s5_lam_re[li], s5_lam_im[li], s5_log_step[li], s5_b_re[li], s5_b_im[li],
                              s5_c_re[li], s5_c_im[li])
        s5_dl = s5_d[li][None]
        glu_b = s5_glu_w[li].astype(BF16)

        *pl_, ex_wg, ex_wu, ex_wd = in_proj(x, nmw, sh1_l, sc1_l, w_in_b, w_rot_b, qk_w, True, 256,
                                            (ex_w_gate, ex_w_up, ex_w_down), li)
        pc_ = in_proj(xc, nmw, sh1_c, sc1_c, w_in_b, w_rot_b, qk_w, False, 256)
        y5_c, y5_l = s5_scan_readout(pc_[0], pl_[0], s5_ops)
        na_l = na_latent(pl_[1], pc_[1], na_bias(na_rpb[li], x.shape[1] // GRID_W))
        hg_c, hg_l = hgrn2_pallas(pc_[2], pl_[2], lower_bounds[li], hg_norm_w[li])
        rt_c, rt_l = retention_pallas(pc_[3], pl_[3], ret_decay_logit[li])

        x, h_l, aff_l = merge_and_route(y5_l, pl_[0], (na_l, hg_l, rt_l), pl_[4], s5_dl, glu_b, wb_b, wo_b, x, g1_l,
                                        nfw, sh2_l, sc2_l, rw_t, 512)
        ex_w = (ex_wg, ex_wu, ex_wd)
        x = expert_choice_ffn(x, g2_l, h_l, aff_l, *ex_w)
        if not last:
            na_c = na_context(pc_[1])
            xc, h_c, aff_c = merge_and_route(y5_c, pc_[0], (na_c, hg_c, rt_c), pc_[4], s5_dl, glu_b, wb_b, wo_b, xc,
                                             g1_c, nfw, sh2_c, sc2_c, rw_t, 256)
            xc = expert_choice_ffn(xc, g2_c, h_c, aff_c, *ex_w)
    return x
```
